```python
import jax
import jax.numpy as jnp
from jax import lax
import numpy as np


D_MODEL = 1024
BATCH = 8
SEQ = 4096
DEPTH = 4

HEAD_DIM = 64
MLA_HEADS = 6
MLA_NOPE = 64
MLA_ROPE = 32
MLA_V = 64
MLA_Q_RANK = 256
MLA_KV_RANK = 128
ROPE_THETA = 10000.0
MLA_Q_BLOCK = 128
MOBA_HEADS = 6
MOBA_BLOCK = 256
MOBA_TOPK = 3
MOBA_Q_CHUNK = 32
DIL_PATTERNS = ((128, 1), (512, 4), (2048, 16))
DIL_GROUP_HEADS = 4
DIL_HEADS = DIL_GROUP_HEADS * len(DIL_PATTERNS)
N_ALIBI = DIL_HEADS + MOBA_HEADS
N_BRANCHES = 3
D_FF = 2816
NORM_EPS = 1e-6
NEG_INF = -1e30
MLA_IN = MLA_Q_RANK + MLA_KV_RANK + MLA_ROPE
MOBA_IN = 3 * MOBA_HEADS * HEAD_DIM
DIL_IN = 3 * DIL_HEADS * HEAD_DIM
GATE_IN = N_BRANCHES * D_MODEL
IN_COLS = MLA_IN + MOBA_IN + DIL_IN + GATE_IN
MLA_OUT = MLA_HEADS * MLA_V
MOBA_OUT = MOBA_HEADS * HEAD_DIM
DIL_OUT = DIL_GROUP_HEADS * HEAD_DIM

kernel_name = 'hybrid_mla_moba_dilated_block'


def rms_norm(x, gain):
    xf = x.astype(jnp.float32)
    y = xf * lax.rsqrt(jnp.mean(xf * xf, axis=-1, keepdims=True) + NORM_EPS)
    return (y * gain.astype(jnp.float32)).astype(x.dtype)


def swiglu(h, w_gate, w_up, w_down):
    a = jnp.einsum('bsd,df->bsf', h, w_gate)
    u = jnp.einsum('bsd,df->bsf', h, w_up)
    return jnp.einsum('bsf,fd->bsd', jax.nn.silu(a) * u, w_down)


def alibi_slopes():
    return 2.0 ** (-8.0 * jnp.arange(1, N_ALIBI + 1, dtype=jnp.float32) / N_ALIBI)


def rope_tables(seq):
    pos = jnp.arange(seq, dtype=jnp.float32)
    inv = ROPE_THETA ** (-jnp.arange(0, MLA_ROPE, 2, dtype=jnp.float32) / MLA_ROPE)
    ang = pos[:, None] * inv[None, :]
    return jnp.cos(ang), jnp.sin(ang)


def apply_rope(t, cos, sin):
    cos = cos.astype(t.dtype)
    sin = sin.astype(t.dtype)
    t1, t2 = jnp.split(t, 2, axis=-1)
    return jnp.concatenate([t1 * cos - t2 * sin, t1 * sin + t2 * cos], axis=-1)


def mla_attention(h_cq, h_ckv, h_krope, q_norm, w_uq, kv_norm, w_ukv):
    B, S, _ = h_cq.shape
    cos, sin = rope_tables(S)
    c_q = rms_norm(h_cq, q_norm)
    q = jnp.einsum('bsr,rk->bsk', c_q, w_uq).reshape(B, S, MLA_HEADS, MLA_NOPE + MLA_ROPE)
    q = jnp.concatenate([q[..., :MLA_NOPE], apply_rope(q[..., MLA_NOPE:], cos[:, None], sin[:, None])], axis=-1)
    c_kv = rms_norm(h_ckv, kv_norm)
    kv = jnp.einsum('bsr,rk->bsk', c_kv, w_ukv).reshape(B, S, MLA_HEADS, MLA_NOPE + MLA_V)
    k_rope = apply_rope(h_krope, cos, sin)
    k = jnp.concatenate([kv[..., :MLA_NOPE], jnp.broadcast_to(k_rope[:, :, None, :], (B, S, MLA_HEADS, MLA_ROPE))], axis=-1)
    v = kv[..., MLA_NOPE:]
    q, k, v = (t.transpose(0, 2, 1, 3) for t in (q, k, v))
    scale = (MLA_NOPE + MLA_ROPE) ** -0.5
    nb = S // MLA_Q_BLOCK
    q_blocks = q.reshape(B, MLA_HEADS, nb, MLA_Q_BLOCK, MLA_NOPE + MLA_ROPE).transpose(2, 0, 1, 3, 4)
    kpos = jnp.arange(S)

    def block(args):
        qi, i = args
        s = jnp.einsum('bhqd,bhkd->bhqk', qi, k).astype(jnp.float32) * scale
        qpos = i * MLA_Q_BLOCK + jnp.arange(MLA_Q_BLOCK)
        s = jnp.where(kpos[None, :] <= qpos[:, None], s, NEG_INF)
        p = jax.nn.softmax(s, axis=-1).astype(v.dtype)
        return jnp.einsum('bhqk,bhkd->bhqd', p, v)

    o = lax.map(block, (q_blocks, jnp.arange(nb)))
    return o.transpose(1, 0, 3, 2, 4).reshape(B, S, MLA_OUT)


def moba_attention(q, k, v, slopes):
    B, H, S, D = q.shape
    nb = -(-S // MOBA_BLOCK)
    sp = nb * MOBA_BLOCK
    pad = ((0, 0), (0, 0), (0, sp - S), (0, 0))
    qp, kp, vp = (jnp.pad(t, pad) for t in (q, k, v))
    kb = kp.reshape(B, H, nb, MOBA_BLOCK, D)
    vb = vp.reshape(B, H, nb, MOBA_BLOCK, D)
    k_mean = jnp.mean(kb, axis=3)
    k_sel_n = min(MOBA_TOPK, nb)
    scale = D ** -0.5
    bi = jnp.arange(B)[:, None, None, None]
    hi = jnp.arange(H)[None, :, None, None]
    blk_ids = jnp.arange(nb)
    offs = jnp.arange(MOBA_BLOCK)
    slope5 = slopes[None, :, None, None, None]

    def chunk(c):
        start = c * MOBA_Q_CHUNK
        qc = lax.dynamic_slice_in_dim(qp, start, MOBA_Q_CHUNK, axis=2)
        b_own = start // MOBA_BLOCK
        tpos = start + jnp.arange(MOBA_Q_CHUNK)
        gate = jnp.einsum('bhqd,bhnd->bhqn', qc, k_mean).astype(jnp.float32)
        gate = jnp.where(blk_ids < b_own, gate, NEG_INF)
        _, idx = lax.top_k(gate, k_sel_n)
        sel_ok = idx < b_own
        k_sel = kb[bi, hi, idx]
        v_sel = vb[bi, hi, idx]
        s_sel = jnp.einsum('bhqd,bhqnpd->bhqnp', qc, k_sel).astype(jnp.float32) * scale
        kpos_sel = idx[..., None] * MOBA_BLOCK + offs
        dist_sel = (tpos[None, None, :, None, None] - kpos_sel).astype(jnp.float32)
        s_sel = jnp.where(sel_ok[..., None], s_sel - slope5 * dist_sel, NEG_INF)
        k_own = lax.dynamic_index_in_dim(kb, b_own, axis=2, keepdims=False)
        v_own = lax.dynamic_index_in_dim(vb, b_own, axis=2, keepdims=False)
        kpos_own = b_own * MOBA_BLOCK + offs
        dist_own = (tpos[:, None] - kpos_own[None, :]).astype(jnp.float32)
        s_own = jnp.einsum('bhqd,bhpd->bhqp', qc, k_own).astype(jnp.float32) * scale
        s_own = jnp.where(dist_own >= 0, s_own - slopes[None, :, None, None] * dist_own, NEG_INF)
        n_sel = k_sel_n * MOBA_BLOCK
        s = jnp.concatenate([s_sel.reshape(B, H, MOBA_Q_CHUNK, n_sel), s_own], axis=-1)
        p = jax.nn.softmax(s, axis=-1).astype(v.dtype)
        p_sel = p[..., :n_sel].reshape(B, H, MOBA_Q_CHUNK, k_sel_n, MOBA_BLOCK)
        p_own = p[..., n_sel:]
        return (jnp.einsum('bhqnp,bhqnpd->bhqd', p_sel, v_sel)
                + jnp.einsum('bhqp,bhpd->bhqd', p_own, v_own))

    o = lax.map(chunk, jnp.arange(sp // MOBA_Q_CHUNK))
    return o.transpose(1, 0, 3, 2, 4).reshape(B, sp, H * D)[:, :S]


def dilated_group_attention(q, k, v, window, dilation, slopes):
    B, H, S, D = q.shape
    L = window // dilation
    span = dilation * L
    P = -(-S // span) * span
    M = P // dilation
    nbk = M // L

    def to_blocks(t):
        t = jnp.pad(t, ((0, 0), (0, 0), (0, P - S), (0, 0)))
        return t.reshape(B, H, M, dilation, D).transpose(0, 1, 3, 2, 4).reshape(B, H, dilation, nbk, L, D)

    def window_keys(t):
        prev = jnp.pad(t[:, :, :, :-1], ((0, 0), (0, 0), (0, 0), (1, 0), (0, 0), (0, 0)))
        return jnp.concatenate([prev, t], axis=4)

    qb = to_blocks(q)
    kw = window_keys(to_blocks(k))
    vw = window_keys(to_blocks(v))
    s = jnp.einsum('bhrnqd,bhrnkd->bhrnqk', qb, kw).astype(jnp.float32) * (D ** -0.5)
    steps = L + jnp.arange(L)[:, None] - jnp.arange(2 * L)[None, :]
    band = (steps >= 0) & (steps <= L)
    before_start = (jnp.arange(nbk)[:, None, None] == 0) & (jnp.arange(2 * L) < L)[None, None, :]
    valid = band[None] & ~before_start
    dist = (steps * dilation).astype(jnp.float32)
    s = jnp.where(valid, s - slopes[None, :, None, None, None, None] * dist, NEG_INF)
    m = jnp.max(s, axis=-1, keepdims=True)
    e = jnp.exp(s - m)
    den = jnp.sum(e, axis=-1)
    o = jnp.einsum('bhrnqk,bhrnkd->bhrnqd', (e / den[..., None]).astype(v.dtype), vw)
    lse = m[..., 0] + jnp.log(den)
    o = o.reshape(B, H, dilation, M, D).transpose(0, 1, 3, 2, 4).reshape(B, H, P, D)[:, :, :S]
    lse = lse.reshape(B, H, dilation, M).transpose(0, 1, 3, 2).reshape(B, H, P)[:, :, :S]
    return o, lse


def hybrid_mixer(h, w_in, q_norm, w_uq, kv_norm, w_ukv, w_br_mla, w_br_moba, w_br_dil, w_out):
    B, S, _ = h.shape
    proj = jnp.einsum('bsd,dk->bsk', h, w_in)
    i0 = MLA_Q_RANK
    i1 = i0 + MLA_KV_RANK
    i2 = i1 + MLA_ROPE
    i3 = i2 + MOBA_IN
    i4 = i3 + DIL_IN
    h_cq, h_ckv, h_kr, h_moba, h_dil, h_gate = jnp.split(proj, [i0, i1, i2, i3, i4], axis=-1)
    slopes = alibi_slopes()
    a = mla_attention(h_cq, h_ckv, h_kr, q_norm, w_uq, kv_norm, w_ukv)
    mqkv = h_moba.reshape(B, S, 3, MOBA_HEADS, HEAD_DIM).transpose(2, 0, 3, 1, 4)
    b = moba_attention(mqkv[0], mqkv[1], mqkv[2], slopes[DIL_HEADS:])
    dqkv = h_dil.reshape(B, S, 3, DIL_HEADS, HEAD_DIM).transpose(2, 0, 3, 1, 4)
    outs, lses = [], []
    for g, (window, dilation) in enumerate(DIL_PATTERNS):
        hs = slice(g * DIL_GROUP_HEADS, (g + 1) * DIL_GROUP_HEADS)
        o_g, l_g = dilated_group_attention(dqkv[0][:, hs], dqkv[1][:, hs], dqkv[2][:, hs], window, dilation, slopes[hs])
        outs.append(o_g)
        lses.append(l_g)
    alpha = jax.nn.softmax(jnp.stack(lses), axis=0)
    c = jnp.einsum('gbhs,gbhsd->bhsd', alpha.astype(outs[0].dtype), jnp.stack(outs))
    c = c.transpose(0, 2, 1, 3).reshape(B, S, DIL_OUT)
    gates = jax.nn.sigmoid(h_gate).reshape(B, S, N_BRANCHES, D_MODEL)
    merged = (gates[:, :, 0] * jnp.einsum('bsk,kd->bsd', a, w_br_mla)
              + gates[:, :, 1] * jnp.einsum('bsk,kd->bsd', b, w_br_moba)
              + gates[:, :, 2] * jnp.einsum('bsk,kd->bsd', c, w_br_dil))
    return jnp.einsum('bsd,de->bse', merged, w_out)


def setup_inputs(seed: int = 0) -> dict:
    key = jax.random.key(seed)
    ks = jax.random.split(key, 20)

    def nrm(k, shape, fan_in):
        return jax.random.normal(k, shape, jnp.float32) * (fan_in ** -0.5)

    def gain(k, shape):
        return 1.0 + 0.01 * jax.random.normal(k, shape, jnp.float32)

    return {
        'x': jax.random.normal(ks[0], (BATCH, SEQ, D_MODEL), jnp.float32),
        'ffn1_norm': gain(ks[1], (DEPTH, D_MODEL)),
        'ffn1_w_gate': nrm(ks[2], (DEPTH, D_MODEL, D_FF), D_MODEL),
        'ffn1_w_up': nrm(ks[3], (DEPTH, D_MODEL, D_FF), D_MODEL),
        'ffn1_w_down': nrm(ks[4], (DEPTH, D_FF, D_MODEL), D_FF),
        'mix_norm': gain(ks[5], (DEPTH, D_MODEL)),
        'w_in': nrm(ks[6], (DEPTH, D_MODEL, IN_COLS), D_MODEL),
        'q_norm': gain(ks[7], (DEPTH, MLA_Q_RANK)),
        'w_uq': nrm(ks[8], (DEPTH, MLA_Q_RANK, MLA_HEADS * (MLA_NOPE + MLA_ROPE)), MLA_Q_RANK),
        'kv_norm': gain(ks[9], (DEPTH, MLA_KV_RANK)),
        'w_ukv': nrm(ks[10], (DEPTH, MLA_KV_RANK, MLA_HEADS * (MLA_NOPE + MLA_V)), MLA_KV_RANK),
        'w_br_mla': nrm(ks[11], (DEPTH, MLA_OUT, D_MODEL), MLA_OUT),
        'w_br_moba': nrm(ks[12], (DEPTH, MOBA_OUT, D_MODEL), MOBA_OUT),
        'w_br_dil': nrm(ks[13], (DEPTH, DIL_OUT, D_MODEL), DIL_OUT),
        'w_out': nrm(ks[14], (DEPTH, D_MODEL, D_MODEL), D_MODEL),
        'ffn2_norm': gain(ks[15], (DEPTH, D_MODEL)),
        'ffn2_w_gate': nrm(ks[16], (DEPTH, D_MODEL, D_FF), D_MODEL),
        'ffn2_w_up': nrm(ks[17], (DEPTH, D_MODEL, D_FF), D_MODEL),
        'ffn2_w_down': nrm(ks[18], (DEPTH, D_FF, D_MODEL), D_FF),
        'final_norm': gain(ks[19], (D_MODEL,)),
    }


def reference(x, ffn1_norm, ffn1_w_gate, ffn1_w_up, ffn1_w_down, mix_norm, w_in, q_norm, w_uq,
              kv_norm, w_ukv, w_br_mla, w_br_moba, w_br_dil, w_out, ffn2_norm, ffn2_w_gate,
              ffn2_w_up, ffn2_w_down, final_norm):
    for l in range(DEPTH):
        x = x + 0.5 * swiglu(rms_norm(x, ffn1_norm[l]), ffn1_w_gate[l], ffn1_w_up[l], ffn1_w_down[l])
        x = x + hybrid_mixer(rms_norm(x, mix_norm[l]), w_in[l], q_norm[l], w_uq[l], kv_norm[l], w_ukv[l],
                             w_br_mla[l], w_br_moba[l], w_br_dil[l], w_out[l])
        x = x + 0.5 * swiglu(rms_norm(x, ffn2_norm[l]), ffn2_w_gate[l], ffn2_w_up[l], ffn2_w_down[l])
    return rms_norm(x, final_norm)
```

```python
import functools

import numpy as np
import jax
import jax.numpy as jnp
from jax import lax
from jax.experimental import pallas as pl
from jax.experimental.pallas import tpu as pltpu

F32 = jnp.float32
BF16 = jnp.bfloat16

D_MODEL = 1024
DEPTH = 4
HEAD_DIM = 64
MLA_HEADS = 6
MLA_NOPE = 64
MLA_ROPE = 32
MLA_V = 64
MLA_Q_RANK = 256
MLA_KV_RANK = 128
ROPE_THETA = 10000.0
MOBA_HEADS = 6
MOBA_BLOCK = 256
MOBA_TOPK = 3
DIL_PATTERNS = ((128, 1), (512, 4), (2048, 16))
DIL_GROUP_HEADS = 4
DIL_HEADS = DIL_GROUP_HEADS * len(DIL_PATTERNS)
N_ALIBI = DIL_HEADS + MOBA_HEADS
N_BRANCHES = 3
D_FF = 2816
NORM_EPS = 1e-6
NEG_INF = -1e30
MLA_IN = MLA_Q_RANK + MLA_KV_RANK + MLA_ROPE
MOBA_IN = 3 * MOBA_HEADS * HEAD_DIM
DIL_IN = 3 * DIL_HEADS * HEAD_DIM
MLA_OUT = MLA_HEADS * MLA_V
MOBA_OUT = MOBA_HEADS * HEAD_DIM
DIL_OUT = DIL_GROUP_HEADS * HEAD_DIM

LANES = 128
MLA_QK = MLA_NOPE + MLA_ROPE
MLA_SCALE = MLA_QK ** -0.5
HEAD_SCALE = HEAD_DIM ** -0.5
DIL_L = DIL_PATTERNS[0][0] // DIL_PATTERNS[0][1]
DIL_COLS = 3 * DIL_GROUP_HEADS * HEAD_DIM

C_CQ = 0
C_CKV = C_CQ + MLA_Q_RANK
C_KR = C_CKV + MLA_KV_RANK
C_KR2 = C_KR + LANES
C_MOBA = C_KR2 + LANES
C_DIL = C_MOBA + MOBA_IN
C_END = C_DIL + DIL_IN

VMEM_LIMIT = 56 * 1024 * 1024

TM_FFN = 512
TM_PROJ = 512
TM_MERGE = 512
FF_CHUNK = 256
ATT_TQ = 256
DIL_QB = 256


def _cparams(sem):
    return pltpu.CompilerParams(dimension_semantics=sem, vmem_limit_bytes=VMEM_LIMIT)


def _resident(shape):
    nd = len(shape)
    return pl.BlockSpec(shape, lambda *_: (0,) * nd, pipeline_mode=pl.Buffered(1))


def _rms(x, gain):
    ms = jnp.mean(x * x, axis=-1, keepdims=True)
    return x * lax.rsqrt(ms + NORM_EPS) * gain


def _dot(a, b):
    return jnp.dot(a, b, preferred_element_type=F32)


def _dot_nt(a, b):
    return lax.dot_general(a, b, (((1,), (1,)), ((), ())), preferred_element_type=F32)


def _ffn_kernel(x_ref, g_ref, wg_ref, wu_ref, wd_ref, fg_ref, o_ref, *, final):
    x = x_ref[...]
    h = _rms(x, g_ref[...]).astype(BF16)
    acc = jnp.zeros(x.shape, F32)
    for c in range(D_FF // FF_CHUNK):
        sl = slice(c * FF_CHUNK, (c + 1) * FF_CHUNK)
        a = _dot(h, wg_ref[:, sl])
        u = _dot(h, wu_ref[:, sl])
        g = (a * jax.nn.sigmoid(a) * u).astype(BF16)
        acc = acc + _dot(g, wd_ref[sl, :])
    y = x + 0.5 * acc
    if final:
        y = _rms(y, fg_ref[...])
    o_ref[...] = y


def _ffn(x2, gain, wg, wu, wd, final_gain, final):
    n = x2.shape[0]
    return pl.pallas_call(
        functools.partial(_ffn_kernel, final=final),
        grid=(n // TM_FFN,),
        in_specs=[
            pl.BlockSpec((TM_FFN, D_MODEL), lambda i: (i, 0)),
            _resident((1, D_MODEL)),
            _resident((D_MODEL, D_FF)),
            _resident((D_MODEL, D_FF)),
            _resident((D_FF, D_MODEL)),
            _resident((1, D_MODEL)),
        ],
        out_specs=pl.BlockSpec((TM_FFN, D_MODEL), lambda i: (i, 0)),
        out_shape=jax.ShapeDtypeStruct((n, D_MODEL), F32),
        compiler_params=_cparams(("parallel",)),
        name="ffn_final" if final else "ffn",
    )(x2, gain, wg, wu, wd, final_gain)


def _inproj_kernel(x_ref, gn_ref, w1_ref, qn_ref, wuq_ref, kvn_ref, wukv_ref, cos_ref, sin_ref,
                   mq_ref, mk_ref, mv_ref, moba_ref, d0_ref, d1_ref, d2_ref):
    x = x_ref[...]
    h = _rms(x, gn_ref[...]).astype(BF16)

    pm = _dot(h, w1_ref[:, C_CQ:C_MOBA])
    cq = _rms(pm[:, C_CQ:C_CKV], qn_ref[...]).astype(BF16)
    ckv = _rms(pm[:, C_CKV:C_KR], kvn_ref[...]).astype(BF16)
    cos = cos_ref[...]
    sin = sin_ref[...]
    k_rope = pm[:, C_KR:C_KR2] * cos + pm[:, C_KR2:C_MOBA] * sin
    qq = _dot(cq, wuq_ref[...])
    kv = _dot(ckv, wukv_ref[...])
    half = MLA_HEADS * LANES
    for hh in range(MLA_HEADS):
        sl = slice(hh * LANES, (hh + 1) * LANES)
        sl2 = slice(half + hh * LANES, half + (hh + 1) * LANES)
        mq_ref[:, sl] = ((qq[:, sl] * cos + qq[:, sl2] * sin) * MLA_SCALE).astype(BF16)
        mk_ref[:, sl] = (kv[:, sl] + k_rope).astype(BF16)
    mv_ref[...] = kv[:, half:].astype(BF16)

    mo = _dot(h, w1_ref[:, C_MOBA:C_DIL])
    moba_ref[:, :MOBA_OUT] = (mo[:, :MOBA_OUT] * HEAD_SCALE).astype(BF16)
    moba_ref[:, MOBA_OUT:] = mo[:, MOBA_OUT:].astype(BF16)

    for g, d_ref in enumerate((d0_ref, d1_ref, d2_ref)):
        dd = _dot(h, w1_ref[:, C_DIL + g * DIL_COLS:C_DIL + (g + 1) * DIL_COLS])
        d_ref[:, :DIL_OUT] = (dd[:, :DIL_OUT] * HEAD_SCALE).astype(BF16)
        d_ref[:, DIL_OUT:] = dd[:, DIL_OUT:].astype(BF16)


def _inproj(x2, gain, w1, qn, wuq, kvn, wukv, cos_t, sin_t, seq):
    n = x2.shape[0]
    tm = TM_PROJ
    sb = seq // tm
    row = lambda c: pl.BlockSpec((tm, c), lambda i: (i, 0))
    outs = [MLA_HEADS * LANES, MLA_HEADS * LANES, MLA_OUT, MOBA_IN, DIL_COLS, DIL_COLS, DIL_COLS]
    return pl.pallas_call(
        _inproj_kernel,
        grid=(n // tm,),
        in_specs=[
            row(D_MODEL),
            _resident((1, D_MODEL)),
            _resident(w1.shape),
            _resident((1, MLA_Q_RANK)),
            _resident(wuq.shape),
            _resident((1, MLA_KV_RANK)),
            _resident(wukv.shape),
            pl.BlockSpec((tm, LANES), lambda i: (i % sb, 0)),
            pl.BlockSpec((tm, LANES), lambda i: (i % sb, 0)),
        ],
        out_specs=[row(c) for c in outs],
        out_shape=[jax.ShapeDtypeStruct((n, c), BF16) for c in outs],
        compiler_params=_cparams(("parallel",)),
        name="inproj",
    )(x2, gain, w1, qn, wuq, kvn, wukv, cos_t, sin_t)


def _softmax_first(s, v):
    m = jnp.max(s, axis=-1, keepdims=True)
    p = jnp.exp(s - m)
    l = jnp.sum(p, axis=-1, keepdims=True)
    return m, l, _dot(p.astype(BF16), v)


def _softmax_update(state, s, v):
    m, l, acc = state
    m_new = jnp.maximum(m, jnp.max(s, axis=-1, keepdims=True))
    alpha = jnp.exp(m - m_new)
    p = jnp.exp(s - m_new)
    l = alpha * l + jnp.sum(p, axis=-1, keepdims=True)
    acc = alpha * acc + _dot(p.astype(BF16), v)
    return m_new, l, acc


def _lane_low_half():
    return lax.broadcasted_iota(jnp.int32, (1, LANES), 1) < HEAD_DIM


def _causal_mask(t):
    row = lax.broadcasted_iota(jnp.int32, (t, t), 0)
    col = lax.broadcasted_iota(jnp.int32, (t, t), 1)
    return col <= row


def _mla_kernel(q0_ref, q1_ref, k0_ref, k1_ref, v_ref, o_ref):
    i = pl.program_id(2)
    t = ATT_TQ
    qs = (q0_ref[0], q1_ref[0])
    k_refs = (k0_ref, k1_ref)

    def kv_block(j):
        start = pl.multiple_of(j * t, t)
        return [r[0, pl.ds(start, t), :] for r in k_refs], v_ref[0, pl.ds(start, t), :]

    ks, v = kv_block(i)
    causal = _causal_mask(t)
    state = tuple(
        _softmax_first(jnp.where(causal, _dot_nt(qs[hh], ks[hh]), NEG_INF), v) for hh in range(2))

    def body(j, state):
        ks, v = kv_block(j)
        return tuple(_softmax_update(state[hh], _dot_nt(qs[hh], ks[hh]), v) for hh in range(2))

    state = lax.fori_loop(0, i, body, state)
    outs = [acc / l for (_, l, acc) in state]
    o_ref[0] = jnp.where(_lane_low_half(), outs[0], outs[1]).astype(o_ref.dtype)


def _mla(mq, mk, mv, batch, seq):
    t = ATT_TQ
    pairs = MLA_HEADS // 2
    mq3 = mq.reshape(batch, seq, -1)
    mk3 = mk.reshape(batch, seq, -1)
    mv3 = mv.reshape(batch, seq, -1)
    qspec = lambda o: pl.BlockSpec((1, t, LANES), lambda b, p, i: (b, i, 2 * p + o))
    kspec = lambda o: pl.BlockSpec((1, seq, LANES), lambda b, p, i: (b, 0, 2 * p + o))
    return pl.pallas_call(
        _mla_kernel,
        grid=(batch, pairs, seq // t),
        in_specs=[qspec(0), qspec(1), kspec(0), kspec(1),
                  pl.BlockSpec((1, seq, LANES), lambda b, p, i: (b, 0, p))],
        out_specs=pl.BlockSpec((1, t, LANES), lambda b, p, i: (b, i, p)),
        out_shape=jax.ShapeDtypeStruct((batch, seq, MLA_OUT), BF16),
        compiler_params=_cparams(("parallel", "parallel", "arbitrary")),
        name="mla_attn",
    )(mq3, mq3, mk3, mk3, mv3)


def _moba_kernel(q_ref, k_ref, v_ref, kb_ref, o_ref, kmh_ref, kml_ref, sel_ref, *, nblk):
    i = pl.program_id(2)
    t = MOBA_BLOCK
    low = _lane_low_half()

    @pl.when(i == 0)
    def _():
        kf = k_ref[0].astype(F32).reshape(nblk, t, LANES)
        km = jnp.sum(kf, axis=1) * (1.0 / t)
        hi = km.astype(BF16)
        kmh_ref[...] = hi
        kml_ref[...] = (km - hi.astype(F32)).astype(BF16)

    qp = q_ref[0]
    zero = jnp.zeros_like(qp)
    qs = (jnp.where(low, qp, zero), jnp.where(low, zero, qp))

    blk = lax.broadcasted_iota(jnp.int32, (1, nblk), 1)
    for hh in range(2):
        gate = _dot_nt(qs[hh], kmh_ref[...]) + _dot_nt(qs[hh], kml_ref[...])
        rank = jnp.zeros(gate.shape, F32)
        for m in range(nblk):
            gm = gate[:, m:m + 1]
            ahead = jnp.where(gm > gate, 1.0, jnp.where((gm == gate) & (blk > m), 1.0, 0.0))
            rank = rank + jnp.where(m < i, ahead, 0.0)
        chosen = (rank < MOBA_TOPK) & (blk < i)
        bias = jnp.where(chosen, 0.0, NEG_INF)
        for n in range(nblk):
            sel_ref[hh, n] = jnp.broadcast_to(bias[:, n:n + 1], (t, LANES))

    def kv_block(j):
        start = pl.multiple_of(j * t, t)
        k = k_ref[0, pl.ds(start, t), :]
        v = v_ref[0, pl.ds(start, t), :]
        kb = [kb_ref[0, hh:hh + 1, pl.ds(start, t)] for hh in range(2)]
        return k, v, kb

    k, v, kb = kv_block(i)
    causal = _causal_mask(t)
    state = tuple(
        _softmax_first(jnp.where(causal, _dot_nt(qs[hh], k) + kb[hh], NEG_INF), v)
        for hh in range(2))

    def body(j, state):
        k, v, kb = kv_block(j)
        new = []
        for hh in range(2):
            sel = sel_ref[hh, j]
            s = _dot_nt(qs[hh], k) + kb[hh] + jnp.concatenate([sel] * (t // LANES), axis=1)
            new.append(_softmax_update(state[hh], s, v))
        return tuple(new)

    state = lax.fori_loop(0, i, body, state)
    outs = [acc / l for (_, l, acc) in state]
    o_ref[0] = jnp.where(low, outs[0], outs[1]).astype(o_ref.dtype)


def _moba(moba, kbias, batch, seq):
    t = MOBA_BLOCK
    pairs = MOBA_HEADS // 2
    nblk = seq // t
    m3 = moba.reshape(batch, seq, -1)
    return pl.pallas_call(
        functools.partial(_moba_kernel, nblk=nblk),
        grid=(batch, pairs, nblk),
        in_specs=[
            pl.BlockSpec((1, t, LANES), lambda b, p, i: (b, i, p)),
            pl.BlockSpec((1, seq, LANES), lambda b, p, i: (b, 0, pairs + p)),
            pl.BlockSpec((1, seq, LANES), lambda b, p, i: (b, 0, 2 * pairs + p)),
            pl.BlockSpec((1, 2, seq), lambda b, p, i: (p, 0, 0)),
        ],
        out_specs=pl.BlockSpec((1, t, LANES), lambda b, p, i: (b, i, p)),
        out_shape=jax.ShapeDtypeStruct((batch, seq, MOBA_OUT), BF16),
        scratch_shapes=[
            pltpu.VMEM((nblk, LANES), BF16),
            pltpu.VMEM((nblk, LANES), BF16),
            pltpu.VMEM((2, nblk, t, LANES), F32),
        ],
        compiler_params=_cparams(("parallel", "parallel", "arbitrary")),
        name="moba_attn",
    )(m3, m3, m3, kbias)


def _dil_kernel(own_ref, prev_ref, bias_ref, o_ref, lse_ref, *, qb):
    n = pl.program_id(2)
    L = DIL_L
    low = _lane_low_half()
    own = own_ref[0]
    prev = prev_ref[0]
    first_cols = lax.broadcasted_iota(jnp.int32, (1, 2 * L), 1) < L
    hg = DIL_OUT // LANES
    for pair in range(hg):
        qsl = slice(pair * LANES, (pair + 1) * LANES)
        ksl = slice(DIL_OUT + pair * LANES, DIL_OUT + (pair + 1) * LANES)
        vsl = slice(2 * DIL_OUT + pair * LANES, 2 * DIL_OUT + (pair + 1) * LANES)
        qp = own[:, qsl]
        zero = jnp.zeros_like(qp)
        qs = (jnp.where(low, qp, zero), jnp.where(low, zero, qp))
        for blk in range(qb // L):
            rows = slice(blk * L, (blk + 1) * L)
            if blk == 0:
                keys = jnp.concatenate([prev[:, ksl], own[:L, ksl]], axis=0)
                vals = jnp.concatenate([prev[:, vsl], own[:L, vsl]], axis=0)
            else:
                keys = own[(blk - 1) * L:(blk + 1) * L, ksl]
                vals = own[(blk - 1) * L:(blk + 1) * L, vsl]
            o_pair = []
            lse_pair = []
            for hh in range(2):
                s = _dot_nt(qs[hh][rows], keys) + bias_ref[2 * pair + hh]
                if blk == 0:
                    s = jnp.where(first_cols & (n == 0), NEG_INF, s)
                m = jnp.max(s, axis=-1, keepdims=True)
                e = jnp.exp(s - m)
                den = jnp.sum(e, axis=-1, keepdims=True)
                o_pair.append(_dot(e.astype(BF16), vals) / den)
                lse_pair.append(jnp.broadcast_to(m + jnp.log(den), (L, LANES)))
            o_ref[0, rows, qsl] = jnp.where(low, o_pair[0], o_pair[1])
            lse_ref[0, rows, qsl] = jnp.where(low, lse_pair[0], lse_pair[1])


def _dil(dg, bias, batch, seq, dilation):
    rows = seq // dilation
    qb = min(DIL_QB, rows)
    L = DIL_L
    per = qb // L
    view = dg.reshape(batch, rows, dilation * DIL_COLS)
    out_sds = jax.ShapeDtypeStruct((batch, rows, dilation * DIL_OUT), F32)
    o, lse = pl.pallas_call(
        functools.partial(_dil_kernel, qb=qb),
        grid=(batch, dilation, rows // qb),
        in_specs=[
            pl.BlockSpec((1, qb, DIL_COLS), lambda b, r, n: (b, n, r)),
            pl.BlockSpec((1, L, DIL_COLS), lambda b, r, n: (b, jnp.maximum(n * per - 1, 0), r)),
            _resident(bias.shape),
        ],
        out_specs=[pl.BlockSpec((1, qb, DIL_OUT), lambda b, r, n: (b, n, r))] * 2,
        out_shape=[out_sds, out_sds],
        compiler_params=_cparams(("parallel", "parallel", "arbitrary")),
        name="dil_attn_d%d" % dilation,
    )(view, view, bias)
    return o.reshape(batch * seq, DIL_OUT), lse.reshape(batch * seq, DIL_OUT)


def _merge_kernel(x_ref, gn_ref, wgate_ref, a_ref, b_ref, o0_ref, o1_ref, o2_ref,
                  l0_ref, l1_ref, l2_ref, wa_ref, wb_ref, wc_ref, wo_ref, out_ref):
    x = x_ref[...]
    h = _rms(x, gn_ref[...]).astype(BF16)

    lses = (l0_ref[...], l1_ref[...], l2_ref[...])
    outs = (o0_ref[...], o1_ref[...], o2_ref[...])
    mx = jnp.maximum(jnp.maximum(lses[0], lses[1]), lses[2])
    es = [jnp.exp(l - mx) for l in lses]
    den = es[0] + es[1] + es[2]
    c = (es[0] / den) * outs[0] + (es[1] / den) * outs[1] + (es[2] / den) * outs[2]

    branches = ((a_ref[...], wa_ref), (b_ref[...], wb_ref), (c.astype(BF16), wc_ref))
    merged = jnp.zeros(x.shape, F32)
    for k, (act, w_ref) in enumerate(branches):
        gate = jax.nn.sigmoid(_dot(h, wgate_ref[:, k * D_MODEL:(k + 1) * D_MODEL]))
        merged = merged + gate * _dot(act, w_ref[...])
    out_ref[...] = x + _dot(merged.astype(BF16), wo_ref[...])


def _merge(x2, gain, wgate, a, b, dil_outs, wa, wb, wc, wo):
    n = x2.shape[0]
    tm = TM_MERGE
    row = lambda c: pl.BlockSpec((tm, c), lambda i: (i, 0))
    (o0, l0), (o1, l1), (o2, l2) = dil_outs
    return pl.pallas_call(
        _merge_kernel,
        grid=(n // tm,),
        in_specs=[
            row(D_MODEL), _resident((1, D_MODEL)), _resident(wgate.shape),
            row(MLA_OUT), row(MOBA_OUT),
            row(DIL_OUT), row(DIL_OUT), row(DIL_OUT),
            row(DIL_OUT), row(DIL_OUT), row(DIL_OUT),
            _resident(wa.shape), _resident(wb.shape), _resident(wc.shape), _resident(wo.shape),
        ],
        out_specs=row(D_MODEL),
        out_shape=jax.ShapeDtypeStruct((n, D_MODEL), F32),
        compiler_params=_cparams(("parallel",)),
        name="merge",
    )(x2, gain, wgate, a, b, o0, o1, o2, l0, l1, l2, wa, wb, wc, wo)


def _alibi_slopes():
    return (2.0 ** (-8.0 * np.arange(1, N_ALIBI + 1, dtype=np.float32) / N_ALIBI)).astype(np.float32)


def _rope_tables(seq):
    pos = jnp.arange(seq, dtype=F32)
    inv = ROPE_THETA ** (-jnp.arange(0, MLA_ROPE, 2, dtype=F32) / MLA_ROPE)
    ang = pos[:, None] * inv[None, :]
    cos, sin = jnp.cos(ang), jnp.sin(ang)
    pad = jnp.zeros((seq, LANES - MLA_QK), F32)
    cos_t = jnp.concatenate([jnp.ones((seq, MLA_NOPE), F32), cos, cos, pad], axis=1)
    sin_t = jnp.concatenate([jnp.zeros((seq, MLA_NOPE), F32), -sin, sin, pad], axis=1)
    return cos_t, sin_t


def _dil_bias(group, dilation):
    L = DIL_L
    slopes = _alibi_slopes()[group * DIL_GROUP_HEADS:(group + 1) * DIL_GROUP_HEADS]
    steps = L + np.arange(L)[:, None] - np.arange(2 * L)[None, :]
    valid = (steps >= 0) & (steps <= L)
    dist = (steps * dilation).astype(np.float32)
    bias = np.where(valid[None], -slopes[:, None, None] * dist[None], np.float32(NEG_INF))
    return jnp.asarray(bias.astype(np.float32))


def _moba_key_bias(seq):
    slopes = _alibi_slopes()[DIL_HEADS:]
    kb = slopes[:, None] * np.arange(seq, dtype=np.float32)[None, :]
    return jnp.asarray(kb.reshape(MOBA_HEADS // 2, 2, seq).astype(np.float32))


def _layer_weights(w_in, w_uq, w_ukv):
    d = w_in.shape[0]
    z = lambda c: jnp.zeros((d, c), w_in.dtype)
    i0 = MLA_Q_RANK + MLA_KV_RANK
    kr = w_in[:, i0:i0 + MLA_ROPE]
    hr = MLA_ROPE // 2
    kr_pad = jnp.concatenate([z(MLA_NOPE), kr, z(LANES - MLA_QK)], axis=1)
    kr_swap = jnp.concatenate([z(MLA_NOPE), kr[:, hr:], kr[:, :hr], z(LANES - MLA_QK)], axis=1)
    moba = w_in[:, MLA_IN:MLA_IN + MOBA_IN]
    dil = w_in[:, MLA_IN + MOBA_IN:MLA_IN + MOBA_IN + DIL_IN].reshape(d, 3, DIL_HEADS, HEAD_DIM)
    groups = [dil[:, :, g * DIL_GROUP_HEADS:(g + 1) * DIL_GROUP_HEADS].reshape(d, DIL_COLS)
              for g in range(len(DIL_PATTERNS))]
    w1 = jnp.concatenate([w_in[:, :i0], kr_pad, kr_swap, moba] + groups, axis=1).astype(BF16)
    wgate = w_in[:, MLA_IN + MOBA_IN + DIL_IN:].astype(BF16)

    r = w_uq.shape[0]
    uq = w_uq.reshape(r, MLA_HEADS, MLA_QK)
    zq = jnp.zeros((r, MLA_HEADS, LANES - MLA_QK), w_uq.dtype)
    zn = jnp.zeros((r, MLA_HEADS, MLA_NOPE), w_uq.dtype)
    uq_pad = jnp.concatenate([uq, zq], axis=2).reshape(r, MLA_HEADS * LANES)
    uq_swap = jnp.concatenate(
        [zn, uq[:, :, MLA_NOPE + hr:], uq[:, :, MLA_NOPE:MLA_NOPE + hr], zq], axis=2
    ).reshape(r, MLA_HEADS * LANES)
    wuq = jnp.concatenate([uq_pad, uq_swap], axis=1).astype(BF16)

    rk = w_ukv.shape[0]
    ukv = w_ukv.reshape(rk, MLA_HEADS, MLA_NOPE + MLA_V)
    zk = jnp.zeros((rk, MLA_HEADS, LANES - MLA_NOPE), w_ukv.dtype)
    k_pad = jnp.concatenate([ukv[:, :, :MLA_NOPE], zk], axis=2).reshape(rk, MLA_HEADS * LANES)
    v_all = ukv[:, :, MLA_NOPE:].reshape(rk, MLA_OUT)
    wukv = jnp.concatenate([k_pad, v_all], axis=1).astype(BF16)
    return w1, wgate, wuq, wukv


def kernel(x, ffn1_norm, ffn1_w_gate, ffn1_w_up, ffn1_w_down, mix_norm, w_in, q_norm, w_uq, kv_norm, w_ukv, w_br_mla, w_br_moba, w_br_dil, w_out, ffn2_norm, ffn2_w_gate, ffn2_w_up, ffn2_w_down, final_norm):
    batch, seq, d = x.shape
    assert d == D_MODEL and seq % (DIL_PATTERNS[-1][1] * DIL_L) == 0 and seq % TM_PROJ == 0
    x2 = x.reshape(batch * seq, d)
    cos_t, sin_t = _rope_tables(seq)
    kbias = _moba_key_bias(seq)
    dil_bias = [_dil_bias(g, dil) for g, (_, dil) in enumerate(DIL_PATTERNS)]
    final_gain = final_norm.reshape(1, d)
    bf = lambda w: w.astype(BF16)
    row = lambda v: v.reshape(1, -1)

    for l in range(DEPTH):
        x2 = _ffn(x2, row(ffn1_norm[l]), bf(ffn1_w_gate[l]), bf(ffn1_w_up[l]), bf(ffn1_w_down[l]),
                  final_gain, False)
        w1, wgate, wuq, wukv = _layer_weights(w_in[l], w_uq[l], w_ukv[l])
        mq, mk, mv, moba, d0, d1, d2 = _inproj(
            x2, row(mix_norm[l]), w1, row(q_norm[l]), wuq, row(kv_norm[l]), wukv, cos_t, sin_t, seq)
        a = _mla(mq, mk, mv, batch, seq).reshape(batch * seq, MLA_OUT)
        b = _moba(moba, kbias, batch, seq).reshape(batch * seq, MOBA_OUT)
        dil_outs = [_dil(dg, dil_bias[g], batch, seq, DIL_PATTERNS[g][1])
                    for g, dg in enumerate((d0, d1, d2))]
        x2 = _merge(x2, row(mix_norm[l]), wgate, a, b, dil_outs,
                    bf(w_br_mla[l]), bf(w_br_moba[l]), bf(w_br_dil[l]), bf(w_out[l]))
        x2 = _ffn(x2, row(ffn2_norm[l]), bf(ffn2_w_gate[l]), bf(ffn2_w_up[l]), bf(ffn2_w_down[l]),
                  final_gain, l == DEPTH - 1)
    return x2.reshape(batch, seq, d)
```

```python
import functools

import numpy as np
import jax
import jax.numpy as jnp
from jax import lax
from jax.experimental import pallas as pl
from jax.experimental.pallas import tpu as pltpu

F32 = jnp.float32
BF16 = jnp.bfloat16

D_MODEL = 1024
DEPTH = 4
HEAD_DIM = 64
MLA_HEADS = 6
MLA_NOPE = 64
MLA_ROPE = 32
MLA_V = 64
MLA_Q_RANK = 256
MLA_KV_RANK = 128
ROPE_THETA = 10000.0
MOBA_HEADS = 6
MOBA_BLOCK = 256
MOBA_TOPK = 3
DIL_PATTERNS = ((128, 1), (512, 4), (2048, 16))
DIL_GROUP_HEADS = 4
DIL_HEADS = DIL_GROUP_HEADS * len(DIL_PATTERNS)
N_ALIBI = DIL_HEADS + MOBA_HEADS
N_BRANCHES = 3
D_FF = 2816
NORM_EPS = 1e-6
NEG_INF = -1e30
MLA_IN = MLA_Q_RANK + MLA_KV_RANK + MLA_ROPE
MOBA_IN = 3 * MOBA_HEADS * HEAD_DIM
DIL_IN = 3 * DIL_HEADS * HEAD_DIM
MLA_OUT = MLA_HEADS * MLA_V
MOBA_OUT = MOBA_HEADS * HEAD_DIM
DIL_OUT = DIL_GROUP_HEADS * HEAD_DIM

LANES = 128
MLA_QK = MLA_NOPE + MLA_ROPE
MLA_QPAD = MLA_HEADS * LANES
LOG2E = 1.4426950408889634
MLA_SCALE = MLA_QK ** -0.5
HEAD_SCALE = HEAD_DIM ** -0.5
DIL_L = DIL_PATTERNS[0][0] // DIL_PATTERNS[0][1]
DIL_COLS = 3 * DIL_GROUP_HEADS * HEAD_DIM

C_CQ = 0
C_CKV = C_CQ + MLA_Q_RANK
C_KR = C_CKV + MLA_KV_RANK
C_KR2 = C_KR + LANES
C_MOBA_K = C_KR2 + LANES
C_DIL = C_MOBA_K + MOBA_OUT
C_END = C_DIL + DIL_IN

VMEM_LIMIT = 56 * 1024 * 1024

TM_FFN = 512
TM_PROJ = 512
TM_MERGE = 512
FF_CHUNK = 256
MLA_T = 256
MLA_NH = 6
MOBA_NH = 6
DIL_QB = 256


def _cparams(sem):
    return pltpu.CompilerParams(dimension_semantics=sem, vmem_limit_bytes=VMEM_LIMIT)


def _resident(shape):
    nd = len(shape)
    return pl.BlockSpec(shape, lambda *_: (0,) * nd, pipeline_mode=pl.Buffered(1))


def _rms(x, gain):
    ms = jnp.mean(x * x, axis=-1, keepdims=True)
    return x * lax.rsqrt(ms + NORM_EPS) * gain


def _dot(a, b):
    return jnp.dot(a, b, preferred_element_type=F32)


def _dot_nt(a, b):
    return lax.dot_general(a, b, (((1,), (1,)), ((), ())), preferred_element_type=F32)


def _ffn_kernel(x_ref, g_ref, wg_ref, wu_ref, wd_ref, fg_ref, o_ref, *, final):
    x = x_ref[...]
    h = _rms(x, g_ref[...]).astype(BF16)
    acc = jnp.zeros(x.shape, F32)
    for c in range(D_FF // FF_CHUNK):
        sl = slice(c * FF_CHUNK, (c + 1) * FF_CHUNK)
        a = _dot(h, wg_ref[:, sl])
        u = _dot(h, wu_ref[:, sl])
        g = (a * jax.nn.sigmoid(a) * u).astype(BF16)
        acc = acc + _dot(g, wd_ref[sl, :])
    y = x + 0.5 * acc
    if final:
        y = _rms(y, fg_ref[...])
    o_ref[...] = y


def _ffn(x2, gain, wg, wu, wd, final_gain, final):
    n = x2.shape[0]
    return pl.pallas_call(
        functools.partial(_ffn_kernel, final=final),
        grid=(n // TM_FFN,),
        in_specs=[
            pl.BlockSpec((TM_FFN, D_MODEL), lambda i: (i, 0)),
            _resident((1, D_MODEL)),
            _resident((D_MODEL, D_FF)),
            _resident((D_MODEL, D_FF)),
            _resident((D_FF, D_MODEL)),
            _resident((1, D_MODEL)),
        ],
        out_specs=pl.BlockSpec((TM_FFN, D_MODEL), lambda i: (i, 0)),
        out_shape=jax.ShapeDtypeStruct((n, D_MODEL), F32),
        compiler_params=_cparams(("parallel",)),
        name="ffn_final" if final else "ffn",
    )(x2, gain, wg, wu, wd, final_gain)


def _inproj_kernel(x_ref, gn_ref, w1_ref, wt_ref, qn_ref, wuqt_ref, kvn_ref, wk_ref, wvt_ref,
                   cos_ref, sin_ref, cost_ref, sint_ref,
                   mqt_ref, mk_ref, mvt_ref, bqt_ref, bk_ref, bvt_ref, d0_ref, d1_ref, d2_ref):
    x = x_ref[0]
    h = _rms(x, gn_ref[...]).astype(BF16)

    pm = _dot(h, w1_ref[:, C_CQ:C_MOBA_K])
    cq = _rms(pm[:, C_CQ:C_CKV], qn_ref[...]).astype(BF16)
    ckv = _rms(pm[:, C_CKV:C_KR], kvn_ref[...]).astype(BF16)
    k_rope = pm[:, C_KR:C_KR2] * cos_ref[...] + pm[:, C_KR2:C_MOBA_K] * sin_ref[...]
    kk = _dot(ckv, wk_ref[...])
    qqt = _dot_nt(wuqt_ref[...], cq)
    cost = cost_ref[...]
    sint = sint_ref[...]
    for hh in range(MLA_HEADS):
        sl = slice(hh * LANES, (hh + 1) * LANES)
        sl2 = slice(MLA_QPAD + hh * LANES, MLA_QPAD + (hh + 1) * LANES)
        mqt_ref[0, sl, :] = ((qqt[sl] * cost + qqt[sl2] * sint) * (MLA_SCALE * LOG2E)).astype(BF16)
        mk_ref[0, :, sl] = (kk[:, sl] + k_rope).astype(BF16)
    mvt_ref[0] = _dot_nt(wvt_ref[...], ckv).astype(BF16)

    bt = _dot_nt(wt_ref[...], h)
    bqt_ref[0] = (bt[:MOBA_OUT] * (HEAD_SCALE * LOG2E)).astype(BF16)
    bvt_ref[0] = bt[MOBA_OUT:].astype(BF16)
    bk_ref[0] = _dot(h, w1_ref[:, C_MOBA_K:C_DIL]).astype(BF16)

    for g, d_ref in enumerate((d0_ref, d1_ref, d2_ref)):
        dd = _dot(h, w1_ref[:, C_DIL + g * DIL_COLS:C_DIL + (g + 1) * DIL_COLS])
        d_ref[0, :, :DIL_OUT] = (dd[:, :DIL_OUT] * HEAD_SCALE).astype(BF16)
        d_ref[0, :, DIL_OUT:] = dd[:, DIL_OUT:].astype(BF16)


def _inproj(x3, gain, lw, tables):
    batch, seq, _ = x3.shape
    tm = TM_PROJ
    cos_t, sin_t, cos_tt, sin_tt = tables
    tok = lambda c: pl.BlockSpec((1, tm, c), lambda b, i: (b, i, 0))
    feat = lambda c: pl.BlockSpec((1, c, tm), lambda b, i: (b, 0, i))
    tok_sds = lambda c: jax.ShapeDtypeStruct((batch, seq, c), BF16)
    feat_sds = lambda c: jax.ShapeDtypeStruct((batch, c, seq), BF16)
    return pl.pallas_call(
        _inproj_kernel,
        grid=(batch, seq // tm),
        in_specs=[
            tok(D_MODEL),
            _resident((1, D_MODEL)),
            _resident(lw["w1"].shape),
            _resident(lw["wt"].shape),
            _resident((1, MLA_Q_RANK)),
            _resident(lw["wuqt"].shape),
            _resident((1, MLA_KV_RANK)),
            _resident(lw["wk"].shape),
            _resident(lw["wvt"].shape),
            pl.BlockSpec((tm, LANES), lambda b, i: (i, 0)),
            pl.BlockSpec((tm, LANES), lambda b, i: (i, 0)),
            pl.BlockSpec((LANES, tm), lambda b, i: (0, i)),
            pl.BlockSpec((LANES, tm), lambda b, i: (0, i)),
        ],
        out_specs=[feat(MLA_QPAD), tok(MLA_QPAD), feat(MLA_OUT),
                   feat(MOBA_OUT), tok(MOBA_OUT), feat(MOBA_OUT),
                   tok(DIL_COLS), tok(DIL_COLS), tok(DIL_COLS)],
        out_shape=[feat_sds(MLA_QPAD), tok_sds(MLA_QPAD), feat_sds(MLA_OUT),
                   feat_sds(MOBA_OUT), tok_sds(MOBA_OUT), feat_sds(MOBA_OUT),
                   tok_sds(DIL_COLS), tok_sds(DIL_COLS), tok_sds(DIL_COLS)],
        compiler_params=_cparams(("parallel", "parallel")),
        name="inproj",
    )(x3, gain, lw["w1"], lw["wt"], lw["qn"], lw["wuqt"], lw["kvn"], lw["wk"], lw["wvt"],
      cos_t, sin_t, cos_tt, sin_tt)


SUM_ROWS = 16


def _flash_heads(nh, i, t, dv, score_fn, value_fn, bias_fn, diag_fn, s_buf, p_buf):
    heads = range(nh)
    ones = jnp.ones((SUM_ROWS, t), BF16)

    def publish(sc):
        for h in heads:
            s_buf[h] = sc[h]

    def stage_softmax(m, bias):
        m_out, alpha = [], []
        for h in heads:
            s = bias(h, s_buf[h])
            m_new = jnp.maximum(m[h], jnp.max(s, axis=0, keepdims=True))
            alpha.append(jnp.exp2(m[h] - m_new))
            p_buf[h] = jnp.exp2(s - m_new).astype(BF16)
            m_out.append(m_new)
        return tuple(m_out), tuple(alpha)

    def stage_values(j, alpha, l, acc):
        jc = jnp.maximum(j, 0)
        l_out, acc_out = [], []
        for h in heads:
            vt = jnp.concatenate([value_fn(jc, h), ones], axis=0)
            pv = _dot(vt, p_buf[h])
            acc_out.append(alpha[h] * acc[h] + pv[:dv])
            l_out.append(alpha[h] * l[h] + pv[dv:dv + 1])
        return tuple(l_out), tuple(acc_out)

    p_buf[...] = jnp.zeros(p_buf.shape, BF16)
    publish(score_fn(0))
    row = lambda v: tuple(jnp.full((1, t), v, F32) for _ in heads)
    carry = (row(NEG_INF), row(1.0), row(0.0), tuple(jnp.zeros((dv, t), F32) for _ in heads))

    def body(j, carry):
        m, alpha, l, acc = carry
        sc = score_fn(j + 1)
        l, acc = stage_values(j - 1, alpha, l, acc)
        m, alpha = stage_softmax(m, lambda h, s: bias_fn(j, h, s))
        publish(sc)
        return m, alpha, l, acc

    m, alpha, l, acc = lax.fori_loop(0, i, body, carry)
    l, acc = stage_values(i - 1, alpha, l, acc)
    m, alpha = stage_softmax(m, diag_fn)
    l, acc = stage_values(i, alpha, l, acc)
    return [acc[h] / l[h] for h in heads]


def _store_token_major(o_ref, outs):
    for p in range(len(outs) // 2):
        pair = jnp.concatenate([outs[2 * p], outs[2 * p + 1]], axis=0)
        o_ref[0, :, p * LANES:(p + 1) * LANES] = pair.T.astype(o_ref.dtype)


def _flash_scratch(nh, t):
    return [pltpu.VMEM((nh, t, t), F32), pltpu.VMEM((nh, t, t), BF16)]


def _causal_mask_t(t):
    key = lax.broadcasted_iota(jnp.int32, (t, t), 0)
    qry = lax.broadcasted_iota(jnp.int32, (t, t), 1)
    return key <= qry


def _mla_kernel(qt_ref, k_ref, vt_ref, o_ref, s_buf, p_buf):
    i = pl.program_id(2)
    t = MLA_T
    nh = MLA_NH
    qs = [qt_ref[0, h * LANES:(h + 1) * LANES, :] for h in range(nh)]

    def scores(j):
        start = pl.multiple_of(j * t, t)
        return [_dot(k_ref[0, pl.ds(start, t), h * LANES:(h + 1) * LANES], qs[h]) for h in range(nh)]

    def values(j, h):
        return vt_ref[0, h * MLA_V:(h + 1) * MLA_V, pl.ds(pl.multiple_of(j * t, t), t)]

    causal = _causal_mask_t(t)
    outs = _flash_heads(nh, i, t, MLA_V, scores, values,
                        lambda j, h, s: s,
                        lambda h, s: jnp.where(causal, s, NEG_INF),
                        s_buf, p_buf)
    _store_token_major(o_ref, outs)


def _mla(mqt, mk, mvt):
    batch, seq, _ = mk.shape
    t = MLA_T
    nh = MLA_NH
    return pl.pallas_call(
        _mla_kernel,
        grid=(batch, MLA_HEADS // nh, seq // t),
        in_specs=[pl.BlockSpec((1, nh * LANES, t), lambda b, g, i: (b, g, i)),
                  pl.BlockSpec((1, seq, nh * LANES), lambda b, g, i: (b, 0, g)),
                  pl.BlockSpec((1, nh * MLA_V, seq), lambda b, g, i: (b, g, 0))],
        out_specs=pl.BlockSpec((1, t, nh * MLA_V), lambda b, g, i: (b, i, g)),
        out_shape=jax.ShapeDtypeStruct((batch, seq, MLA_OUT), BF16),
        scratch_shapes=_flash_scratch(nh, t),
        compiler_params=_cparams(("parallel", "parallel", "arbitrary")),
        name="mla_attn",
    )(mqt, mk, mvt)


def _moba_kernel(qt_ref, k_ref, vt_ref, kb_ref, o_ref, kmh_ref, kml_ref, sel_ref, s_buf, p_buf,
                 *, nblk):
    i = pl.program_id(2)
    t = MOBA_BLOCK
    nh = MOBA_NH

    @pl.when(i == 0)
    def _():
        kf = k_ref[0].astype(F32).reshape(nblk, t, nh * HEAD_DIM)
        km = jnp.sum(kf, axis=1) * (1.0 / t)
        hi = km.astype(BF16)
        kmh_ref[...] = hi
        kml_ref[...] = (km - hi.astype(F32)).astype(BF16)

    low = lax.broadcasted_iota(jnp.int32, (LANES, 1), 0) < HEAD_DIM
    qs = []
    for p in range(nh // 2):
        qp = qt_ref[0, p * LANES:(p + 1) * LANES, :]
        zero = jnp.zeros_like(qp)
        qs += [jnp.where(low, qp, zero), jnp.where(low, zero, qp)]
    pair_cols = lambda h: slice((h // 2) * LANES, (h // 2 + 1) * LANES)

    blk = lax.broadcasted_iota(jnp.int32, (nblk, 1), 0)
    for h in range(nh):
        gate = (_dot(kmh_ref[:, pair_cols(h)], qs[h])
                + _dot(kml_ref[:, pair_cols(h)], qs[h]))
        rank = jnp.zeros(gate.shape, F32)
        for m in range(nblk):
            gm = gate[m:m + 1, :]
            ahead = jnp.where(gm > gate, 1.0, jnp.where((gm == gate) & (blk > m), 1.0, 0.0))
            rank = rank + jnp.where(m < i, ahead, 0.0)
        chosen = (rank < MOBA_TOPK) & (blk < i)
        sel_ref[h] = jnp.where(chosen, 0.0, NEG_INF)

    def scores(j):
        start = pl.multiple_of(j * t, t)
        return [_dot(k_ref[0, pl.ds(start, t), pair_cols(h)], qs[h]) for h in range(nh)]

    def values(j, h):
        return vt_ref[0, h * HEAD_DIM:(h + 1) * HEAD_DIM, pl.ds(pl.multiple_of(j * t, t), t)]

    slopes = [kb_ref[h, 1:2, 0:1] * float(t) for h in range(nh)]

    def bias(j, h, s):
        row = sel_ref[h, pl.ds(j, 1), :] + slopes[h] * (j - i).astype(F32)
        return s + kb_ref[h] + row

    causal = _causal_mask_t(t)
    outs = _flash_heads(nh, i, t, HEAD_DIM, scores, values, bias,
                        lambda h, s: jnp.where(causal, s + kb_ref[h], NEG_INF),
                        s_buf, p_buf)
    _store_token_major(o_ref, outs)


def _moba(bqt, bk, bvt, kbias):
    batch, seq, _ = bk.shape
    t = MOBA_BLOCK
    nh = MOBA_NH
    nblk = seq // t
    cols = nh * HEAD_DIM
    return pl.pallas_call(
        functools.partial(_moba_kernel, nblk=nblk),
        grid=(batch, MOBA_HEADS // nh, nblk),
        in_specs=[
            pl.BlockSpec((1, cols, t), lambda b, g, i: (b, g, i)),
            pl.BlockSpec((1, seq, cols), lambda b, g, i: (b, 0, g)),
            pl.BlockSpec((1, cols, seq), lambda b, g, i: (b, g, 0)),
            pl.BlockSpec((nh, t, t), lambda b, g, i: (g, 0, 0)),
        ],
        out_specs=pl.BlockSpec((1, t, cols), lambda b, g, i: (b, i, g)),
        out_shape=jax.ShapeDtypeStruct((batch, seq, MOBA_OUT), BF16),
        scratch_shapes=[
            pltpu.VMEM((nblk, cols), BF16),
            pltpu.VMEM((nblk, cols), BF16),
            pltpu.VMEM((nh, nblk, t), F32),
        ] + _flash_scratch(nh, t),
        compiler_params=_cparams(("parallel", "parallel", "arbitrary")),
        name="moba_attn",
    )(bqt, bk, bvt, kbias)


def _dil_kernel(own_ref, prev_ref, bias_ref, o_ref, lse_ref, *, qb):
    n = pl.program_id(2)
    L = DIL_L
    low = lax.broadcasted_iota(jnp.int32, (1, LANES), 1) < HEAD_DIM
    own = own_ref[0]
    prev = prev_ref[0]
    first_cols = lax.broadcasted_iota(jnp.int32, (1, 2 * L), 1) < L
    hg = DIL_OUT // LANES
    for pair in range(hg):
        qsl = slice(pair * LANES, (pair + 1) * LANES)
        ksl = slice(DIL_OUT + pair * LANES, DIL_OUT + (pair + 1) * LANES)
        vsl = slice(2 * DIL_OUT + pair * LANES, 2 * DIL_OUT + (pair + 1) * LANES)
        qp = own[:, qsl]
        zero = jnp.zeros_like(qp)
        qs = (jnp.where(low, qp, zero), jnp.where(low, zero, qp))
        for blk in range(qb // L):
            rows = slice(blk * L, (blk + 1) * L)
            if blk == 0:
                keys = jnp.concatenate([prev[:, ksl], own[:L, ksl]], axis=0)
                vals = jnp.concatenate([prev[:, vsl], own[:L, vsl]], axis=0)
            else:
                keys = own[(blk - 1) * L:(blk + 1) * L, ksl]
                vals = own[(blk - 1) * L:(blk + 1) * L, vsl]
            o_pair = []
            lse_pair = []
            for hh in range(2):
                s = _dot_nt(qs[hh][rows], keys) + bias_ref[2 * pair + hh]
                if blk == 0:
                    s = jnp.where(first_cols & (n == 0), NEG_INF, s)
                m = jnp.max(s, axis=-1, keepdims=True)
                e = jnp.exp(s - m)
                den = jnp.sum(e, axis=-1, keepdims=True)
                o_pair.append(_dot(e.astype(BF16), vals) / den)
                lse_pair.append(jnp.broadcast_to(m + jnp.log(den), (L, LANES)))
            o_ref[0, rows, qsl] = jnp.where(low, o_pair[0], o_pair[1])
            lse_ref[0, rows, qsl] = jnp.where(low, lse_pair[0], lse_pair[1])


def _dil(dg, bias, dilation):
    batch, seq, _ = dg.shape
    rows = seq // dilation
    qb = min(DIL_QB, rows)
    L = DIL_L
    per = qb // L
    view = dg.reshape(batch, rows, dilation * DIL_COLS)
    out_sds = jax.ShapeDtypeStruct((batch, rows, dilation * DIL_OUT), F32)
    o, lse = pl.pallas_call(
        functools.partial(_dil_kernel, qb=qb),
        grid=(batch, dilation, rows // qb),
        in_specs=[
            pl.BlockSpec((1, qb, DIL_COLS), lambda b, r, n: (b, n, r)),
            pl.BlockSpec((1, L, DIL_COLS), lambda b, r, n: (b, jnp.maximum(n * per - 1, 0), r)),
            _resident(bias.shape),
        ],
        out_specs=[pl.BlockSpec((1, qb, DIL_OUT), lambda b, r, n: (b, n, r))] * 2,
        out_shape=[out_sds, out_sds],
        compiler_params=_cparams(("parallel", "parallel", "arbitrary")),
        name="dil_attn_d%d" % dilation,
    )(view, view, bias)
    return o.reshape(batch * seq, DIL_OUT), lse.reshape(batch * seq, DIL_OUT)


def _merge_kernel(x_ref, gn_ref, wgate_ref, a_ref, b_ref, o0_ref, o1_ref, o2_ref,
                  l0_ref, l1_ref, l2_ref, wa_ref, wb_ref, wc_ref, wo_ref, out_ref):
    x = x_ref[...]
    h = _rms(x, gn_ref[...]).astype(BF16)

    lses = (l0_ref[...], l1_ref[...], l2_ref[...])
    outs = (o0_ref[...], o1_ref[...], o2_ref[...])
    mx = jnp.maximum(jnp.maximum(lses[0], lses[1]), lses[2])
    es = [jnp.exp(l - mx) for l in lses]
    den = es[0] + es[1] + es[2]
    c = (es[0] / den) * outs[0] + (es[1] / den) * outs[1] + (es[2] / den) * outs[2]

    branches = ((a_ref[...], wa_ref), (b_ref[...], wb_ref), (c.astype(BF16), wc_ref))
    merged = jnp.zeros(x.shape, F32)
    for k, (act, w_ref) in enumerate(branches):
        gate = jax.nn.sigmoid(_dot(h, wgate_ref[:, k * D_MODEL:(k + 1) * D_MODEL]))
        merged = merged + gate * _dot(act, w_ref[...])
    out_ref[...] = x + _dot(merged.astype(BF16), wo_ref[...])


def _merge(x2, gain, wgate, a, b, dil_outs, wa, wb, wc, wo):
    n = x2.shape[0]
    tm = TM_MERGE
    row = lambda c: pl.BlockSpec((tm, c), lambda i: (i, 0))
    (o0, l0), (o1, l1), (o2, l2) = dil_outs
    return pl.pallas_call(
        _merge_kernel,
        grid=(n // tm,),
        in_specs=[
            row(D_MODEL), _resident((1, D_MODEL)), _resident(wgate.shape),
            row(MLA_OUT), row(MOBA_OUT),
            row(DIL_OUT), row(DIL_OUT), row(DIL_OUT),
            row(DIL_OUT), row(DIL_OUT), row(DIL_OUT),
            _resident(wa.shape), _resident(wb.shape), _resident(wc.shape), _resident(wo.shape),
        ],
        out_specs=row(D_MODEL),
        out_shape=jax.ShapeDtypeStruct((n, D_MODEL), F32),
        compiler_params=_cparams(("parallel",)),
        name="merge",
    )(x2, gain, wgate, a, b, o0, o1, o2, l0, l1, l2, wa, wb, wc, wo)


def _alibi_slopes():
    return (2.0 ** (-8.0 * np.arange(1, N_ALIBI + 1, dtype=np.float32) / N_ALIBI)).astype(np.float32)


def _rope_tables(seq):
    pos = jnp.arange(seq, dtype=F32)
    inv = ROPE_THETA ** (-jnp.arange(0, MLA_ROPE, 2, dtype=F32) / MLA_ROPE)
    ang = pos[:, None] * inv[None, :]
    cos, sin = jnp.cos(ang), jnp.sin(ang)
    pad = jnp.zeros((seq, LANES - MLA_QK), F32)
    cos_t = jnp.concatenate([jnp.ones((seq, MLA_NOPE), F32), cos, cos, pad], axis=1)
    sin_t = jnp.concatenate([jnp.zeros((seq, MLA_NOPE), F32), -sin, sin, pad], axis=1)
    return cos_t, sin_t, cos_t.T, sin_t.T


def _dil_bias(group, dilation):
    L = DIL_L
    slopes = _alibi_slopes()[group * DIL_GROUP_HEADS:(group + 1) * DIL_GROUP_HEADS]
    steps = L + np.arange(L)[:, None] - np.arange(2 * L)[None, :]
    valid = (steps >= 0) & (steps <= L)
    dist = (steps * dilation).astype(np.float32)
    bias = np.where(valid[None], -slopes[:, None, None] * dist[None], np.float32(NEG_INF))
    return jnp.asarray(bias.astype(np.float32))


def _moba_key_bias():
    t = MOBA_BLOCK
    slopes = _alibi_slopes()[DIL_HEADS:] * np.float32(LOG2E)
    col = slopes[:, None, None] * np.arange(t, dtype=np.float32)[None, :, None]
    return jnp.asarray(np.broadcast_to(col, (MOBA_HEADS, t, t)).astype(np.float32))


def _layer_weights(w_in, w_uq, w_ukv, q_norm, kv_norm):
    d = w_in.shape[0]
    z = lambda c: jnp.zeros((d, c), w_in.dtype)
    i0 = MLA_Q_RANK + MLA_KV_RANK
    kr = w_in[:, i0:i0 + MLA_ROPE]
    hr = MLA_ROPE // 2
    kr_pad = jnp.concatenate([z(MLA_NOPE), kr, z(LANES - MLA_QK)], axis=1)
    kr_swap = jnp.concatenate([z(MLA_NOPE), kr[:, hr:], kr[:, :hr], z(LANES - MLA_QK)], axis=1)
    moba = w_in[:, MLA_IN:MLA_IN + MOBA_IN]
    bq, bk, bv = (moba[:, c * MOBA_OUT:(c + 1) * MOBA_OUT] for c in range(3))
    dil = w_in[:, MLA_IN + MOBA_IN:MLA_IN + MOBA_IN + DIL_IN].reshape(d, 3, DIL_HEADS, HEAD_DIM)
    groups = [dil[:, :, g * DIL_GROUP_HEADS:(g + 1) * DIL_GROUP_HEADS].reshape(d, DIL_COLS)
              for g in range(len(DIL_PATTERNS))]
    w1 = jnp.concatenate([w_in[:, :i0], kr_pad, kr_swap, bk] + groups, axis=1).astype(BF16)
    wt = jnp.concatenate([bq, bv], axis=1).T.astype(BF16)
    wgate = w_in[:, MLA_IN + MOBA_IN + DIL_IN:].astype(BF16)

    r = w_uq.shape[0]
    uq = w_uq.reshape(r, MLA_HEADS, MLA_QK)
    zq = jnp.zeros((r, MLA_HEADS, LANES - MLA_QK), w_uq.dtype)
    zn = jnp.zeros((r, MLA_HEADS, MLA_NOPE), w_uq.dtype)
    uq_pad = jnp.concatenate([uq, zq], axis=2).reshape(r, MLA_QPAD)
    uq_swap = jnp.concatenate(
        [zn, uq[:, :, MLA_NOPE + hr:], uq[:, :, MLA_NOPE:MLA_NOPE + hr], zq], axis=2
    ).reshape(r, MLA_QPAD)
    wuqt = jnp.concatenate([uq_pad, uq_swap], axis=1).T.astype(BF16)

    rk = w_ukv.shape[0]
    ukv = w_ukv.reshape(rk, MLA_HEADS, MLA_NOPE + MLA_V)
    zk = jnp.zeros((rk, MLA_HEADS, LANES - MLA_NOPE), w_ukv.dtype)
    wk = jnp.concatenate([ukv[:, :, :MLA_NOPE], zk], axis=2).reshape(rk, MLA_QPAD).astype(BF16)
    wvt = ukv[:, :, MLA_NOPE:].reshape(rk, MLA_OUT).T.astype(BF16)
    return dict(w1=w1, wt=wt, wgate=wgate, wuqt=wuqt, wk=wk, wvt=wvt,
                qn=q_norm.reshape(1, -1), kvn=kv_norm.reshape(1, -1))


def kernel(x, ffn1_norm, ffn1_w_gate, ffn1_w_up, ffn1_w_down, mix_norm, w_in, q_norm, w_uq, kv_norm, w_ukv, w_br_mla, w_br_moba, w_br_dil, w_out, ffn2_norm, ffn2_w_gate, ffn2_w_up, ffn2_w_down, final_norm):
    batch, seq, d = x.shape
    assert d == D_MODEL and seq % (DIL_PATTERNS[-1][1] * DIL_L) == 0 and seq % TM_PROJ == 0
    n = batch * seq
    x2 = x.reshape(n, d)
    tables = _rope_tables(seq)
    kbias = _moba_key_bias()
    dil_bias = [_dil_bias(g, dil) for g, (_, dil) in enumerate(DIL_PATTERNS)]
    final_gain = final_norm.reshape(1, d)
    bf = lambda w: w.astype(BF16)
    row = lambda v: v.reshape(1, -1)

    for l in range(DEPTH):
        x2 = _ffn(x2, row(ffn1_norm[l]), bf(ffn1_w_gate[l]), bf(ffn1_w_up[l]), bf(ffn1_w_down[l]),
                  final_gain, False)
        lw = _layer_weights(w_in[l], w_uq[l], w_ukv[l], q_norm[l], kv_norm[l])
        mqt, mk, mvt, bqt, bk, bvt, d0, d1, d2 = _inproj(
            x2.reshape(batch, seq, d), row(mix_norm[l]), lw, tables)
        a = _mla(mqt, mk, mvt).reshape(n, MLA_OUT)
        b = _moba(bqt, bk, bvt, kbias).reshape(n, MOBA_OUT)
        dil_outs = [_dil(dg, dil_bias[g], DIL_PATTERNS[g][1]) for g, dg in enumerate((d0, d1, d2))]
        x2 = _merge(x2, row(mix_norm[l]), lw["wgate"], a, b, dil_outs,
                    bf(w_br_mla[l]), bf(w_br_moba[l]), bf(w_br_dil[l]), bf(w_out[l]))
        x2 = _ffn(x2, row(ffn2_norm[l]), bf(ffn2_w_gate[l]), bf(ffn2_w_up[l]), bf(ffn2_w_down[l]),
                  final_gain, l == DEPTH - 1)
    return x2.reshape(batch, seq, d)
```

```python
import functools

import numpy as np
import jax
import jax.numpy as jnp
from jax import lax
from jax.experimental import pallas as pl
from jax.experimental.pallas import tpu as pltpu

F32 = jnp.float32
BF16 = jnp.bfloat16

D_MODEL = 1024
DEPTH = 4
HEAD_DIM = 64
MLA_HEADS = 6
MLA_NOPE = 64
MLA_ROPE = 32
MLA_V = 64
MLA_Q_RANK = 256
MLA_KV_RANK = 128
ROPE_THETA = 10000.0
MOBA_HEADS = 6
MOBA_BLOCK = 256
MOBA_TOPK = 3
DIL_PATTERNS = ((128, 1), (512, 4), (2048, 16))
DIL_GROUP_HEADS = 4
DIL_HEADS = DIL_GROUP_HEADS * len(DIL_PATTERNS)
N_ALIBI = DIL_HEADS + MOBA_HEADS
N_BRANCHES = 3
D_FF = 2816
NORM_EPS = 1e-6
NEG_INF = -1e30
MLA_IN = MLA_Q_RANK + MLA_KV_RANK + MLA_ROPE
MOBA_IN = 3 * MOBA_HEADS * HEAD_DIM
DIL_IN = 3 * DIL_HEADS * HEAD_DIM
MLA_OUT = MLA_HEADS * MLA_V
MOBA_OUT = MOBA_HEADS * HEAD_DIM
DIL_OUT = DIL_GROUP_HEADS * HEAD_DIM

LANES = 128
MLA_QK = MLA_NOPE + MLA_ROPE
MLA_QPAD = MLA_HEADS * LANES
LOG2E = 1.4426950408889634
MLA_SCALE = MLA_QK ** -0.5
HEAD_SCALE = HEAD_DIM ** -0.5
DIL_L = DIL_PATTERNS[0][0] // DIL_PATTERNS[0][1]
DIL_COLS = 3 * DIL_GROUP_HEADS * HEAD_DIM

C_CQ = 0
C_CKV = C_CQ + MLA_Q_RANK
C_KR = C_CKV + MLA_KV_RANK
C_KR2 = C_KR + LANES
C_MOBA_K = C_KR2 + LANES
C_DIL = C_MOBA_K + MOBA_OUT
C_END = C_DIL + DIL_IN

VMEM_LIMIT = 56 * 1024 * 1024

TM_FFN = 512
TM_PROJ = 512
TM_MERGE = 512
FF_CHUNK = 256
MLA_T = 256
MLA_NH = 6
MOBA_NH = 6
DIL_STEP_ROWS = 512


def _cparams(sem):
    return pltpu.CompilerParams(dimension_semantics=sem, vmem_limit_bytes=VMEM_LIMIT)


def _resident(shape):
    nd = len(shape)
    return pl.BlockSpec(shape, lambda *_: (0,) * nd, pipeline_mode=pl.Buffered(1))


def _rms(x, gain):
    ms = jnp.mean(x * x, axis=-1, keepdims=True)
    return x * lax.rsqrt(ms + NORM_EPS) * gain


def _dot(a, b):
    return jnp.dot(a, b, preferred_element_type=F32)


def _dot_nt(a, b):
    return lax.dot_general(a, b, (((1,), (1,)), ((), ())), preferred_element_type=F32)


def _ffn_kernel(x_ref, g_ref, wg_ref, wu_ref, wd_ref, fg_ref, o_ref, *, final):
    x = x_ref[...]
    h = _rms(x, g_ref[...]).astype(BF16)
    acc = jnp.zeros(x.shape, F32)
    for c in range(D_FF // FF_CHUNK):
        sl = slice(c * FF_CHUNK, (c + 1) * FF_CHUNK)
        a = _dot(h, wg_ref[:, sl])
        u = _dot(h, wu_ref[:, sl])
        g = (a * jax.nn.sigmoid(a) * u).astype(BF16)
        acc = acc + _dot(g, wd_ref[sl, :])
    y = x + 0.5 * acc
    if final:
        y = _rms(y, fg_ref[...])
    o_ref[...] = y


def _ffn(x2, gain, wg, wu, wd, final_gain, final):
    n = x2.shape[0]
    return pl.pallas_call(
        functools.partial(_ffn_kernel, final=final),
        grid=(n // TM_FFN,),
        in_specs=[
            pl.BlockSpec((TM_FFN, D_MODEL), lambda i: (i, 0)),
            _resident((1, D_MODEL)),
            _resident((D_MODEL, D_FF)),
            _resident((D_MODEL, D_FF)),
            _resident((D_FF, D_MODEL)),
            _resident((1, D_MODEL)),
        ],
        out_specs=pl.BlockSpec((TM_FFN, D_MODEL), lambda i: (i, 0)),
        out_shape=jax.ShapeDtypeStruct((n, D_MODEL), F32),
        compiler_params=_cparams(("parallel",)),
        name="ffn_final" if final else "ffn",
    )(x2, gain, wg, wu, wd, final_gain)


def _inproj_kernel(x_ref, gn_ref, w1_ref, wt_ref, qn_ref, wuqt_ref, kvn_ref, wk_ref, wvt_ref,
                   cos_ref, sin_ref, cost_ref, sint_ref,
                   mqt_ref, mk_ref, mvt_ref, bqt_ref, bk_ref, bvt_ref, d0_ref, d1_ref, d2_ref,
                   stage_ref):
    x = x_ref[0]
    h = _rms(x, gn_ref[...]).astype(BF16)

    pm = _dot(h, w1_ref[:, C_CQ:C_MOBA_K])
    cq = _rms(pm[:, C_CQ:C_CKV], qn_ref[...]).astype(BF16)
    ckv = _rms(pm[:, C_CKV:C_KR], kvn_ref[...]).astype(BF16)
    k_rope = pm[:, C_KR:C_KR2] * cos_ref[...] + pm[:, C_KR2:C_MOBA_K] * sin_ref[...]
    kk = _dot(ckv, wk_ref[...])
    qqt = _dot_nt(wuqt_ref[...], cq)
    cost = cost_ref[...]
    sint = sint_ref[...]
    for hh in range(MLA_HEADS):
        sl = slice(hh * LANES, (hh + 1) * LANES)
        sl2 = slice(MLA_QPAD + hh * LANES, MLA_QPAD + (hh + 1) * LANES)
        mqt_ref[0, sl, :] = ((qqt[sl] * cost + qqt[sl2] * sint) * (MLA_SCALE * LOG2E)).astype(BF16)
        mk_ref[0, :, sl] = (kk[:, sl] + k_rope).astype(BF16)
    mvt_ref[0] = _dot_nt(wvt_ref[...], ckv).astype(BF16)

    bt = _dot_nt(wt_ref[...], h)
    bqt_ref[0] = (bt[:MOBA_OUT] * (HEAD_SCALE * LOG2E)).astype(BF16)
    bvt_ref[0] = bt[MOBA_OUT:].astype(BF16)
    bk_ref[0] = _dot(h, w1_ref[:, C_MOBA_K:C_DIL]).astype(BF16)

    tm = x.shape[0]
    for g, d_ref in enumerate((d0_ref, d1_ref, d2_ref)):
        dil = DIL_PATTERNS[g][1]
        dd = _dot(h, w1_ref[:, C_DIL + g * DIL_COLS:C_DIL + (g + 1) * DIL_COLS])
        for c in range(DIL_COLS // LANES):
            cols = slice(c * LANES, (c + 1) * LANES)
            blk = dd[:, cols]
            stage_ref[c] = blk * HEAD_SCALE if c < DIL_OUT // LANES else blk
            for r in range(dil):
                d_ref[0, r, :, cols] = stage_ref[c, pl.ds(r, tm // dil, stride=dil), :].astype(BF16)


def _inproj(x3, gain, lw, tables):
    batch, seq, _ = x3.shape
    tm = TM_PROJ
    cos_t, sin_t, cos_tt, sin_tt = tables
    tok = lambda c: pl.BlockSpec((1, tm, c), lambda b, i: (b, i, 0))
    feat = lambda c: pl.BlockSpec((1, c, tm), lambda b, i: (b, 0, i))
    tok_sds = lambda c: jax.ShapeDtypeStruct((batch, seq, c), BF16)
    feat_sds = lambda c: jax.ShapeDtypeStruct((batch, c, seq), BF16)
    dils = [dil for _, dil in DIL_PATTERNS]
    stream = lambda dil: pl.BlockSpec((1, dil, tm // dil, DIL_COLS), lambda b, i: (b, 0, i, 0))
    stream_sds = lambda dil: jax.ShapeDtypeStruct((batch, dil, seq // dil, DIL_COLS), BF16)
    return pl.pallas_call(
        _inproj_kernel,
        grid=(batch, seq // tm),
        in_specs=[
            tok(D_MODEL),
            _resident((1, D_MODEL)),
            _resident(lw["w1"].shape),
            _resident(lw["wt"].shape),
            _resident((1, MLA_Q_RANK)),
            _resident(lw["wuqt"].shape),
            _resident((1, MLA_KV_RANK)),
            _resident(lw["wk"].shape),
            _resident(lw["wvt"].shape),
            pl.BlockSpec((tm, LANES), lambda b, i: (i, 0)),
            pl.BlockSpec((tm, LANES), lambda b, i: (i, 0)),
            pl.BlockSpec((LANES, tm), lambda b, i: (0, i)),
            pl.BlockSpec((LANES, tm), lambda b, i: (0, i)),
        ],
        out_specs=[feat(MLA_QPAD), tok(MLA_QPAD), feat(MLA_OUT),
                   feat(MOBA_OUT), tok(MOBA_OUT), feat(MOBA_OUT)] + [stream(dil) for dil in dils],
        out_shape=[feat_sds(MLA_QPAD), tok_sds(MLA_QPAD), feat_sds(MLA_OUT),
                   feat_sds(MOBA_OUT), tok_sds(MOBA_OUT), feat_sds(MOBA_OUT)]
        + [stream_sds(dil) for dil in dils],
        scratch_shapes=[pltpu.VMEM((DIL_COLS // LANES, tm, LANES), F32)],
        compiler_params=_cparams(("parallel", "parallel")),
        name="inproj",
    )(x3, gain, lw["w1"], lw["wt"], lw["qn"], lw["wuqt"], lw["kvn"], lw["wk"], lw["wvt"],
      cos_t, sin_t, cos_tt, sin_tt)


SUM_ROWS = 16


def _flash_heads(nh, i, t, dv, score_fn, value_fn, bias_fn, diag_fn, s_buf, p_buf):
    heads = range(nh)
    ones = jnp.ones((SUM_ROWS, t), BF16)

    def publish(sc):
        for h in heads:
            s_buf[h] = sc[h]

    def stage_softmax(m, bias):
        m_out, alpha = [], []
        for h in heads:
            s = bias(h, s_buf[h])
            m_new = jnp.maximum(m[h], jnp.max(s, axis=0, keepdims=True))
            alpha.append(jnp.exp2(m[h] - m_new))
            p_buf[h] = jnp.exp2(s - m_new).astype(BF16)
            m_out.append(m_new)
        return tuple(m_out), tuple(alpha)

    def stage_values(j, alpha, l, acc):
        jc = jnp.maximum(j, 0)
        l_out, acc_out = [], []
        for h in heads:
            vt = jnp.concatenate([value_fn(jc, h), ones], axis=0)
            pv = _dot(vt, p_buf[h])
            acc_out.append(alpha[h] * acc[h] + pv[:dv])
            l_out.append(alpha[h] * l[h] + pv[dv:dv + 1])
        return tuple(l_out), tuple(acc_out)

    p_buf[...] = jnp.zeros(p_buf.shape, BF16)
    publish(score_fn(0))
    row = lambda v: tuple(jnp.full((1, t), v, F32) for _ in heads)
    carry = (row(NEG_INF), row(1.0), row(0.0), tuple(jnp.zeros((dv, t), F32) for _ in heads))

    def body(j, carry):
        m, alpha, l, acc = carry
        sc = score_fn(j + 1)
        l, acc = stage_values(j - 1, alpha, l, acc)
        m, alpha = stage_softmax(m, lambda h, s: bias_fn(j, h, s))
        publish(sc)
        return m, alpha, l, acc

    m, alpha, l, acc = lax.fori_loop(0, i, body, carry)
    l, acc = stage_values(i - 1, alpha, l, acc)
    m, alpha = stage_softmax(m, diag_fn)
    l, acc = stage_values(i, alpha, l, acc)
    return [acc[h] / l[h] for h in heads]


def _store_token_major(o_ref, outs):
    for p in range(len(outs) // 2):
        pair = jnp.concatenate([outs[2 * p], outs[2 * p + 1]], axis=0)
        o_ref[0, :, p * LANES:(p + 1) * LANES] = pair.T.astype(o_ref.dtype)


def _flash_scratch(nh, t):
    return [pltpu.VMEM((nh, t, t), F32), pltpu.VMEM((nh, t, t), BF16)]


def _causal_mask_t(t):
    key = lax.broadcasted_iota(jnp.int32, (t, t), 0)
    qry = lax.broadcasted_iota(jnp.int32, (t, t), 1)
    return key <= qry


def _mla_kernel(qt_ref, k_ref, vt_ref, o_ref, s_buf, p_buf):
    i = pl.program_id(2)
    t = MLA_T
    nh = MLA_NH
    qs = [qt_ref[0, h * LANES:(h + 1) * LANES, :] for h in range(nh)]

    def scores(j):
        start = pl.multiple_of(j * t, t)
        return [_dot(k_ref[0, pl.ds(start, t), h * LANES:(h + 1) * LANES], qs[h]) for h in range(nh)]

    def values(j, h):
        return vt_ref[0, h * MLA_V:(h + 1) * MLA_V, pl.ds(pl.multiple_of(j * t, t), t)]

    causal = _causal_mask_t(t)
    outs = _flash_heads(nh, i, t, MLA_V, scores, values,
                        lambda j, h, s: s,
                        lambda h, s: jnp.where(causal, s, NEG_INF),
                        s_buf, p_buf)
    _store_token_major(o_ref, outs)


def _mla(mqt, mk, mvt):
    batch, seq, _ = mk.shape
    t = MLA_T
    nh = MLA_NH
    return pl.pallas_call(
        _mla_kernel,
        grid=(batch, MLA_HEADS // nh, seq // t),
        in_specs=[pl.BlockSpec((1, nh * LANES, t), lambda b, g, i: (b, g, i)),
                  pl.BlockSpec((1, seq, nh * LANES), lambda b, g, i: (b, 0, g)),
                  pl.BlockSpec((1, nh * MLA_V, seq), lambda b, g, i: (b, g, 0))],
        out_specs=pl.BlockSpec((1, t, nh * MLA_V), lambda b, g, i: (b, i, g)),
        out_shape=jax.ShapeDtypeStruct((batch, seq, MLA_OUT), BF16),
        scratch_shapes=_flash_scratch(nh, t),
        compiler_params=_cparams(("parallel", "parallel", "arbitrary")),
        name="mla_attn",
    )(mqt, mk, mvt)


def _moba_kernel(qt_ref, k_ref, vt_ref, kb_ref, o_ref, kmh_ref, kml_ref, sel_ref, s_buf, p_buf,
                 *, nblk):
    i = pl.program_id(2)
    t = MOBA_BLOCK
    nh = MOBA_NH

    @pl.when(i == 0)
    def _():
        kf = k_ref[0].astype(F32).reshape(nblk, t, nh * HEAD_DIM)
        km = jnp.sum(kf, axis=1) * (1.0 / t)
        hi = km.astype(BF16)
        kmh_ref[...] = hi
        kml_ref[...] = (km - hi.astype(F32)).astype(BF16)

    low = lax.broadcasted_iota(jnp.int32, (LANES, 1), 0) < HEAD_DIM
    qs = []
    for p in range(nh // 2):
        qp = qt_ref[0, p * LANES:(p + 1) * LANES, :]
        zero = jnp.zeros_like(qp)
        qs += [jnp.where(low, qp, zero), jnp.where(low, zero, qp)]
    pair_cols = lambda h: slice((h // 2) * LANES, (h // 2 + 1) * LANES)

    blk = lax.broadcasted_iota(jnp.int32, (nblk, 1), 0)
    for h in range(nh):
        gate = (_dot(kmh_ref[:, pair_cols(h)], qs[h])
                + _dot(kml_ref[:, pair_cols(h)], qs[h]))
        rank = jnp.zeros(gate.shape, F32)
        for m in range(nblk):
            gm = gate[m:m + 1, :]
            ahead = jnp.where(gm > gate, 1.0, jnp.where((gm == gate) & (blk > m), 1.0, 0.0))
            rank = rank + jnp.where(m < i, ahead, 0.0)
        chosen = (rank < MOBA_TOPK) & (blk < i)
        sel_ref[h] = jnp.where(chosen, 0.0, NEG_INF)

    def scores(j):
        start = pl.multiple_of(j * t, t)
        return [_dot(k_ref[0, pl.ds(start, t), pair_cols(h)], qs[h]) for h in range(nh)]

    def values(j, h):
        return vt_ref[0, h * HEAD_DIM:(h + 1) * HEAD_DIM, pl.ds(pl.multiple_of(j * t, t), t)]

    slopes = [kb_ref[h, 1:2, 0:1] * float(t) for h in range(nh)]

    def bias(j, h, s):
        row = sel_ref[h, pl.ds(j, 1), :] + slopes[h] * (j - i).astype(F32)
        return s + kb_ref[h] + row

    causal = _causal_mask_t(t)
    outs = _flash_heads(nh, i, t, HEAD_DIM, scores, values, bias,
                        lambda h, s: jnp.where(causal, s + kb_ref[h], NEG_INF),
                        s_buf, p_buf)
    _store_token_major(o_ref, outs)


def _moba(bqt, bk, bvt, kbias):
    batch, seq, _ = bk.shape
    t = MOBA_BLOCK
    nh = MOBA_NH
    nblk = seq // t
    cols = nh * HEAD_DIM
    return pl.pallas_call(
        functools.partial(_moba_kernel, nblk=nblk),
        grid=(batch, MOBA_HEADS // nh, nblk),
        in_specs=[
            pl.BlockSpec((1, cols, t), lambda b, g, i: (b, g, i)),
            pl.BlockSpec((1, seq, cols), lambda b, g, i: (b, 0, g)),
            pl.BlockSpec((1, cols, seq), lambda b, g, i: (b, g, 0)),
            pl.BlockSpec((nh, t, t), lambda b, g, i: (g, 0, 0)),
        ],
        out_specs=pl.BlockSpec((1, t, cols), lambda b, g, i: (b, i, g)),
        out_shape=jax.ShapeDtypeStruct((batch, seq, MOBA_OUT), BF16),
        scratch_shapes=[
            pltpu.VMEM((nblk, cols), BF16),
            pltpu.VMEM((nblk, cols), BF16),
            pltpu.VMEM((nh, nblk, t), F32),
        ] + _flash_scratch(nh, t),
        compiler_params=_cparams(("parallel", "parallel", "arbitrary")),
        name="moba_attn",
    )(bqt, bk, bvt, kbias)


def _dil_kernel(own_ref, prev_ref, bias_ref, o_ref, lse_ref, *, ns, qb):
    n = pl.program_id(2)
    L = DIL_L
    low = lax.broadcasted_iota(jnp.int32, (1, LANES), 1) < HEAD_DIM
    first_cols = lax.broadcasted_iota(jnp.int32, (1, 2 * L), 1) < L
    problems = [(si, pair, blk) for si in range(ns) for pair in range(DIL_OUT // LANES)
                for blk in range(qb // L)]

    def window(si, blk, cols):
        if blk == 0:
            return jnp.concatenate([prev_ref[0, si, :, cols], own_ref[0, si, :L, cols]], axis=0)
        return own_ref[0, si, (blk - 1) * L:(blk + 1) * L, cols]

    scores = []
    for si, pair, blk in problems:
        qp = own_ref[0, si, blk * L:(blk + 1) * L, pair * LANES:(pair + 1) * LANES]
        zero = jnp.zeros_like(qp)
        keys = window(si, blk, slice(DIL_OUT + pair * LANES, DIL_OUT + (pair + 1) * LANES))
        scores.append([_dot_nt(jnp.where(low, qp, zero), keys),
                       _dot_nt(jnp.where(low, zero, qp), keys)])

    probs = []
    for (si, pair, blk), sc in zip(problems, scores):
        per_head = []
        for hh in range(2):
            s = sc[hh] + bias_ref[2 * pair + hh]
            if blk == 0:
                s = jnp.where(first_cols & (n == 0), NEG_INF, s)
            m = jnp.max(s, axis=-1, keepdims=True)
            e = jnp.exp(s - m)
            den = jnp.sum(e, axis=-1, keepdims=True)
            per_head.append((e.astype(BF16), den, m + jnp.log(den)))
        probs.append(per_head)

    for (si, pair, blk), per_head in zip(problems, probs):
        vals = window(si, blk, slice(2 * DIL_OUT + pair * LANES, 2 * DIL_OUT + (pair + 1) * LANES))
        o_pair = [_dot(e, vals) / den for e, den, _ in per_head]
        lse_pair = [jnp.broadcast_to(lse, (L, LANES)) for _, _, lse in per_head]
        dst = (0, si, slice(blk * L, (blk + 1) * L), slice(pair * LANES, (pair + 1) * LANES))
        o_ref[dst] = jnp.where(low, o_pair[0], o_pair[1])
        lse_ref[dst] = jnp.where(low, lse_pair[0], lse_pair[1])


def _dil(dg, bias):
    batch, dilation, rows, _ = dg.shape
    L = DIL_L
    qb = min(DIL_STEP_ROWS, rows)
    ns = min(DIL_STEP_ROWS // qb, dilation)
    per = qb // L
    out_sds = jax.ShapeDtypeStruct((batch, dilation, rows, DIL_OUT), F32)
    return pl.pallas_call(
        functools.partial(_dil_kernel, ns=ns, qb=qb),
        grid=(batch, dilation // ns, rows // qb),
        in_specs=[
            pl.BlockSpec((1, ns, qb, DIL_COLS), lambda b, r, n: (b, r, n, 0)),
            pl.BlockSpec((1, ns, L, DIL_COLS),
                         lambda b, r, n: (b, r, jnp.maximum(n * per - 1, 0), 0)),
            _resident(bias.shape),
        ],
        out_specs=[pl.BlockSpec((1, ns, qb, DIL_OUT), lambda b, r, n: (b, r, n, 0))] * 2,
        out_shape=[out_sds, out_sds],
        compiler_params=_cparams(("parallel", "parallel", "arbitrary")),
        name="dil_attn_d%d" % dilation,
    )(dg, dg, bias)


def _merge_kernel(x_ref, gn_ref, wgate_ref, a_ref, b_ref, o0_ref, o1_ref, o2_ref,
                  l0_ref, l1_ref, l2_ref, wa_ref, wb_ref, wc_ref, wo_ref, out_ref, *scratch):
    x = x_ref[0]
    h = _rms(x, gn_ref[...]).astype(BF16)
    tm = x.shape[0]
    scratch = list(scratch)

    def token_order(ref):
        dil = ref.shape[1]
        if dil == 1:
            return ref[0, 0]
        buf = scratch.pop()
        for c in range(buf.shape[0]):
            for r in range(dil):
                buf[c, pl.ds(r, tm // dil, stride=dil), :] = ref[0, r, :, c * LANES:(c + 1) * LANES]
        return jnp.concatenate([buf[c] for c in range(buf.shape[0])], axis=1)

    lses = [token_order(r) for r in (l0_ref, l1_ref, l2_ref)]
    outs = [token_order(r) for r in (o0_ref, o1_ref, o2_ref)]
    mx = jnp.maximum(jnp.maximum(lses[0], lses[1]), lses[2])
    es = [jnp.exp(l - mx) for l in lses]
    den = es[0] + es[1] + es[2]
    c = (es[0] / den) * outs[0] + (es[1] / den) * outs[1] + (es[2] / den) * outs[2]

    branches = ((a_ref[0], wa_ref), (b_ref[0], wb_ref), (c.astype(BF16), wc_ref))
    merged = jnp.zeros(x.shape, F32)
    for k, (act, w_ref) in enumerate(branches):
        gate = jax.nn.sigmoid(_dot(h, wgate_ref[:, k * D_MODEL:(k + 1) * D_MODEL]))
        merged = merged + gate * _dot(act, w_ref[...])
    out_ref[0] = x + _dot(merged.astype(BF16), wo_ref[...])


def _merge(x3, gain, wgate, a, b, dil_outs, wa, wb, wc, wo):
    batch, seq, _ = x3.shape
    tm = TM_MERGE
    tok = lambda c: pl.BlockSpec((1, tm, c), lambda b, i: (b, i, 0))
    stream = lambda arr: pl.BlockSpec((1, arr.shape[1], tm // arr.shape[1], DIL_OUT),
                                      lambda b, i: (b, 0, i, 0))
    (o0, l0), (o1, l1), (o2, l2) = dil_outs
    streams = (o0, o1, o2, l0, l1, l2)
    n_reordered = sum(arr.shape[1] > 1 for arr in streams)
    return pl.pallas_call(
        _merge_kernel,
        grid=(batch, seq // tm),
        in_specs=[tok(D_MODEL), _resident((1, D_MODEL)), _resident(wgate.shape),
                  tok(MLA_OUT), tok(MOBA_OUT)]
        + [stream(arr) for arr in streams]
        + [_resident(wa.shape), _resident(wb.shape), _resident(wc.shape), _resident(wo.shape)],
        out_specs=tok(D_MODEL),
        out_shape=jax.ShapeDtypeStruct((batch, seq, D_MODEL), F32),
        scratch_shapes=[pltpu.VMEM((DIL_OUT // LANES, tm, LANES), F32)] * n_reordered,
        compiler_params=_cparams(("parallel", "parallel")),
        name="merge",
    )(x3, gain, wgate, a, b, *streams, wa, wb, wc, wo)


def _alibi_slopes():
    return (2.0 ** (-8.0 * np.arange(1, N_ALIBI + 1, dtype=np.float32) / N_ALIBI)).astype(np.float32)


def _rope_tables(seq):
    pos = jnp.arange(seq, dtype=F32)
    inv = ROPE_THETA ** (-jnp.arange(0, MLA_ROPE, 2, dtype=F32) / MLA_ROPE)
    ang = pos[:, None] * inv[None, :]
    cos, sin = jnp.cos(ang), jnp.sin(ang)
    pad = jnp.zeros((seq, LANES - MLA_QK), F32)
    cos_t = jnp.concatenate([jnp.ones((seq, MLA_NOPE), F32), cos, cos, pad], axis=1)
    sin_t = jnp.concatenate([jnp.zeros((seq, MLA_NOPE), F32), -sin, sin, pad], axis=1)
    return cos_t, sin_t, cos_t.T, sin_t.T


def _dil_bias(group, dilation):
    L = DIL_L
    slopes = _alibi_slopes()[group * DIL_GROUP_HEADS:(group + 1) * DIL_GROUP_HEADS]
    steps = L + np.arange(L)[:, None] - np.arange(2 * L)[None, :]
    valid = (steps >= 0) & (steps <= L)
    dist = (steps * dilation).astype(np.float32)
    bias = np.where(valid[None], -slopes[:, None, None] * dist[None], np.float32(NEG_INF))
    return jnp.asarray(bias.astype(np.float32))


def _moba_key_bias():
    t = MOBA_BLOCK
    slopes = _alibi_slopes()[DIL_HEADS:] * np.float32(LOG2E)
    col = slopes[:, None, None] * np.arange(t, dtype=np.float32)[None, :, None]
    return jnp.asarray(np.broadcast_to(col, (MOBA_HEADS, t, t)).astype(np.float32))


def _layer_weights(w_in, w_uq, w_ukv, q_norm, kv_norm):
    d = w_in.shape[0]
    z = lambda c: jnp.zeros((d, c), w_in.dtype)
    i0 = MLA_Q_RANK + MLA_KV_RANK
    kr = w_in[:, i0:i0 + MLA_ROPE]
    hr = MLA_ROPE // 2
    kr_pad = jnp.concatenate([z(MLA_NOPE), kr, z(LANES - MLA_QK)], axis=1)
    kr_swap = jnp.concatenate([z(MLA_NOPE), kr[:, hr:], kr[:, :hr], z(LANES - MLA_QK)], axis=1)
    moba = w_in[:, MLA_IN:MLA_IN + MOBA_IN]
    bq, bk, bv = (moba[:, c * MOBA_OUT:(c + 1) * MOBA_OUT] for c in range(3))
    dil = w_in[:, MLA_IN + MOBA_IN:MLA_IN + MOBA_IN + DIL_IN].reshape(d, 3, DIL_HEADS, HEAD_DIM)
    groups = [dil[:, :, g * DIL_GROUP_HEADS:(g + 1) * DIL_GROUP_HEADS].reshape(d, DIL_COLS)
              for g in range(len(DIL_PATTERNS))]
    w1 = jnp.concatenate([w_in[:, :i0], kr_pad, kr_swap, bk] + groups, axis=1).astype(BF16)
    wt = jnp.concatenate([bq, bv], axis=1).T.astype(BF16)
    wgate = w_in[:, MLA_IN + MOBA_IN + DIL_IN:].astype(BF16)

    r = w_uq.shape[0]
    uq = w_uq.reshape(r, MLA_HEADS, MLA_QK)
    zq = jnp.zeros((r, MLA_HEADS, LANES - MLA_QK), w_uq.dtype)
    zn = jnp.zeros((r, MLA_HEADS, MLA_NOPE), w_uq.dtype)
    uq_pad = jnp.concatenate([uq, zq], axis=2).reshape(r, MLA_QPAD)
    uq_swap = jnp.concatenate(
        [zn, uq[:, :, MLA_NOPE + hr:], uq[:, :, MLA_NOPE:MLA_NOPE + hr], zq], axis=2
    ).reshape(r, MLA_QPAD)
    wuqt = jnp.concatenate([uq_pad, uq_swap], axis=1).T.astype(BF16)

    rk = w_ukv.shape[0]
    ukv = w_ukv.reshape(rk, MLA_HEADS, MLA_NOPE + MLA_V)
    zk = jnp.zeros((rk, MLA_HEADS, LANES - MLA_NOPE), w_ukv.dtype)
    wk = jnp.concatenate([ukv[:, :, :MLA_NOPE], zk], axis=2).reshape(rk, MLA_QPAD).astype(BF16)
    wvt = ukv[:, :, MLA_NOPE:].reshape(rk, MLA_OUT).T.astype(BF16)
    return dict(w1=w1, wt=wt, wgate=wgate, wuqt=wuqt, wk=wk, wvt=wvt,
                qn=q_norm.reshape(1, -1), kvn=kv_norm.reshape(1, -1))


def kernel(x, ffn1_norm, ffn1_w_gate, ffn1_w_up, ffn1_w_down, mix_norm, w_in, q_norm, w_uq, kv_norm, w_ukv, w_br_mla, w_br_moba, w_br_dil, w_out, ffn2_norm, ffn2_w_gate, ffn2_w_up, ffn2_w_down, final_norm):
    batch, seq, d = x.shape
    assert d == D_MODEL and seq % (DIL_PATTERNS[-1][1] * DIL_L) == 0 and seq % TM_PROJ == 0
    n = batch * seq
    x2 = x.reshape(n, d)
    tables = _rope_tables(seq)
    kbias = _moba_key_bias()
    dil_bias = [_dil_bias(g, dil) for g, (_, dil) in enumerate(DIL_PATTERNS)]
    final_gain = final_norm.reshape(1, d)
    bf = lambda w: w.astype(BF16)
    row = lambda v: v.reshape(1, -1)

    for l in range(DEPTH):
        x2 = _ffn(x2, row(ffn1_norm[l]), bf(ffn1_w_gate[l]), bf(ffn1_w_up[l]), bf(ffn1_w_down[l]),
                  final_gain, False)
        lw = _layer_weights(w_in[l], w_uq[l], w_ukv[l], q_norm[l], kv_norm[l])
        mqt, mk, mvt, bqt, bk, bvt, d0, d1, d2 = _inproj(
            x2.reshape(batch, seq, d), row(mix_norm[l]), lw, tables)
        a = _mla(mqt, mk, mvt)
        b = _moba(bqt, bk, bvt, kbias)
        dil_outs = [_dil(dg, dil_bias[g]) for g, dg in enumerate((d0, d1, d2))]
        x2 = _merge(x2.reshape(batch, seq, d), row(mix_norm[l]), lw["wgate"], a, b, dil_outs,
                    bf(w_br_mla[l]), bf(w_br_moba[l]), bf(w_br_dil[l]), bf(w_out[l])).reshape(n, d)
        x2 = _ffn(x2, row(ffn2_norm[l]), bf(ffn2_w_gate[l]), bf(ffn2_w_up[l]), bf(ffn2_w_down[l]),
                  final_gain, l == DEPTH - 1)
    return x2.reshape(batch, seq, d)
```

```python
import functools

import numpy as np
import jax
import jax.numpy as jnp
from jax import lax
from jax.experimental import pallas as pl
from jax.experimental.pallas import tpu as pltpu

F32 = jnp.float32
BF16 = jnp.bfloat16

D_MODEL = 1024
DEPTH = 4
HEAD_DIM = 64
MLA_HEADS = 6
MLA_NOPE = 64
MLA_ROPE = 32
MLA_V = 64
MLA_Q_RANK = 256
MLA_KV_RANK = 128
ROPE_THETA = 10000.0
MOBA_HEADS = 6
MOBA_BLOCK = 256
MOBA_TOPK = 3
DIL_PATTERNS = ((128, 1), (512, 4), (2048, 16))
DIL_GROUP_HEADS = 4
DIL_HEADS = DIL_GROUP_HEADS * len(DIL_PATTERNS)
N_ALIBI = DIL_HEADS + MOBA_HEADS
N_BRANCHES = 3
D_FF = 2816
NORM_EPS = 1e-6
NEG_INF = -1e30
MLA_IN = MLA_Q_RANK + MLA_KV_RANK + MLA_ROPE
MOBA_IN = 3 * MOBA_HEADS * HEAD_DIM
DIL_IN = 3 * DIL_HEADS * HEAD_DIM
MLA_OUT = MLA_HEADS * MLA_V
MOBA_OUT = MOBA_HEADS * HEAD_DIM
DIL_OUT = DIL_GROUP_HEADS * HEAD_DIM

LANES = 128
MLA_QK = MLA_NOPE + MLA_ROPE
MLA_QPAD = MLA_HEADS * LANES
LOG2E = 1.4426950408889634
MLA_SCALE = MLA_QK ** -0.5
HEAD_SCALE = HEAD_DIM ** -0.5
DIL_L = DIL_PATTERNS[0][0] // DIL_PATTERNS[0][1]
DIL_COLS = 3 * DIL_GROUP_HEADS * HEAD_DIM

C_CQ = 0
C_CKV = C_CQ + MLA_Q_RANK
C_KR = C_CKV + MLA_KV_RANK
C_KR2 = C_KR + LANES
C_MOBA_K = C_KR2 + LANES
MOBA_KPAD = MOBA_HEADS * LANES
C_DIL = C_MOBA_K + MOBA_KPAD
C_END = C_DIL + DIL_IN

AUX_POS = HEAD_DIM
AUX_POS_PARTS = 3
AUX_SEL = HEAD_DIM + 16

VMEM_LIMIT = 56 * 1024 * 1024

TM_FFN = 512
TM_PROJ = 512
TM_MERGE = 512
FF_CHUNK = 256
MLA_T = 256
MLA_NH = 6
MOBA_NH = 6
DIL_STEP_ROWS = 512


def _cparams(sem):
    return pltpu.CompilerParams(dimension_semantics=sem, vmem_limit_bytes=VMEM_LIMIT)


def _resident(shape):
    nd = len(shape)
    return pl.BlockSpec(shape, lambda *_: (0,) * nd, pipeline_mode=pl.Buffered(1))


def _rms(x, gain):
    ms = jnp.mean(x * x, axis=-1, keepdims=True)
    return x * lax.rsqrt(ms + NORM_EPS) * gain


def _dot(a, b):
    return jnp.dot(a, b, preferred_element_type=F32)


def _dot_nt(a, b):
    return lax.dot_general(a, b, (((1,), (1,)), ((), ())), preferred_element_type=F32)


def _ffn_kernel(x_ref, g_ref, wg_ref, wu_ref, wd_ref, fg_ref, o_ref, *, final):
    x = x_ref[...]
    h = _rms(x, g_ref[...]).astype(BF16)
    acc = jnp.zeros(x.shape, F32)
    for c in range(D_FF // FF_CHUNK):
        sl = slice(c * FF_CHUNK, (c + 1) * FF_CHUNK)
        a = _dot(h, wg_ref[:, sl])
        u = _dot(h, wu_ref[:, sl])
        g = (a * jax.nn.sigmoid(a) * u).astype(BF16)
        acc = acc + _dot(g, wd_ref[sl, :])
    y = x + 0.5 * acc
    if final:
        y = _rms(y, fg_ref[...])
    o_ref[...] = y


def _ffn(x2, gain, wg, wu, wd, final_gain, final):
    n = x2.shape[0]
    return pl.pallas_call(
        functools.partial(_ffn_kernel, final=final),
        grid=(n // TM_FFN,),
        in_specs=[
            pl.BlockSpec((TM_FFN, D_MODEL), lambda i: (i, 0)),
            _resident((1, D_MODEL)),
            _resident((D_MODEL, D_FF)),
            _resident((D_MODEL, D_FF)),
            _resident((D_FF, D_MODEL)),
            _resident((1, D_MODEL)),
        ],
        out_specs=pl.BlockSpec((TM_FFN, D_MODEL), lambda i: (i, 0)),
        out_shape=jax.ShapeDtypeStruct((n, D_MODEL), F32),
        compiler_params=_cparams(("parallel",)),
        name="ffn_final" if final else "ffn",
    )(x2, gain, wg, wu, wd, final_gain)


def _inproj_kernel(x_ref, gn_ref, w1_ref, wt_ref, qn_ref, wuqt_ref, kvn_ref, wk_ref, wvt_ref,
                   cos_ref, sin_ref, cost_ref, sint_ref, kaux_ref,
                   mqt_ref, mk_ref, mvt_ref, bqt_ref, bk_ref, bvt_ref, d0_ref, d1_ref, d2_ref,
                   stage_ref):
    x = x_ref[0]
    h = _rms(x, gn_ref[...]).astype(BF16)

    pm = _dot(h, w1_ref[:, C_CQ:C_MOBA_K])
    cq = _rms(pm[:, C_CQ:C_CKV], qn_ref[...]).astype(BF16)
    ckv = _rms(pm[:, C_CKV:C_KR], kvn_ref[...]).astype(BF16)
    k_rope = pm[:, C_KR:C_KR2] * cos_ref[...] + pm[:, C_KR2:C_MOBA_K] * sin_ref[...]
    kk = _dot(ckv, wk_ref[...])
    qqt = _dot_nt(wuqt_ref[...], cq)
    cost = cost_ref[...]
    sint = sint_ref[...]
    for hh in range(MLA_HEADS):
        sl = slice(hh * LANES, (hh + 1) * LANES)
        sl2 = slice(MLA_QPAD + hh * LANES, MLA_QPAD + (hh + 1) * LANES)
        mqt_ref[0, sl, :] = ((qqt[sl] * cost + qqt[sl2] * sint) * (MLA_SCALE * LOG2E)).astype(BF16)
        mk_ref[0, :, sl] = (kk[:, sl] + k_rope).astype(BF16)
    mvt_ref[0] = _dot_nt(wvt_ref[...], ckv).astype(BF16)

    bt = _dot_nt(wt_ref[...], h)
    bqt_ref[0] = (bt[:MOBA_OUT] * (HEAD_SCALE * LOG2E)).astype(BF16)
    bvt_ref[0] = bt[MOBA_OUT:].astype(BF16)
    bk_ref[0] = (_dot(h, w1_ref[:, C_MOBA_K:C_DIL]) + kaux_ref[...].astype(F32)).astype(BF16)

    tm = x.shape[0]
    for g, d_ref in enumerate((d0_ref, d1_ref, d2_ref)):
        dil = DIL_PATTERNS[g][1]
        dd = _dot(h, w1_ref[:, C_DIL + g * DIL_COLS:C_DIL + (g + 1) * DIL_COLS])
        for c in range(DIL_COLS // LANES):
            cols = slice(c * LANES, (c + 1) * LANES)
            blk = dd[:, cols]
            stage_ref[c] = blk * HEAD_SCALE if c < DIL_OUT // LANES else blk
            for r in range(dil):
                d_ref[0, r, :, cols] = stage_ref[c, pl.ds(r, tm // dil, stride=dil), :].astype(BF16)


def _inproj(x3, gain, lw, tables):
    batch, seq, _ = x3.shape
    tm = TM_PROJ
    cos_t, sin_t, cos_tt, sin_tt, kaux = tables
    tok = lambda c: pl.BlockSpec((1, tm, c), lambda b, i: (b, i, 0))
    feat = lambda c: pl.BlockSpec((1, c, tm), lambda b, i: (b, 0, i))
    tok_sds = lambda c: jax.ShapeDtypeStruct((batch, seq, c), BF16)
    feat_sds = lambda c: jax.ShapeDtypeStruct((batch, c, seq), BF16)
    dils = [dil for _, dil in DIL_PATTERNS]
    stream = lambda dil: pl.BlockSpec((1, dil, tm // dil, DIL_COLS), lambda b, i: (b, 0, i, 0))
    stream_sds = lambda dil: jax.ShapeDtypeStruct((batch, dil, seq // dil, DIL_COLS), BF16)
    return pl.pallas_call(
        _inproj_kernel,
        grid=(batch, seq // tm),
        in_specs=[
            tok(D_MODEL),
            _resident((1, D_MODEL)),
            _resident(lw["w1"].shape),
            _resident(lw["wt"].shape),
            _resident((1, MLA_Q_RANK)),
            _resident(lw["wuqt"].shape),
            _resident((1, MLA_KV_RANK)),
            _resident(lw["wk"].shape),
            _resident(lw["wvt"].shape),
            pl.BlockSpec((tm, LANES), lambda b, i: (i, 0)),
            pl.BlockSpec((tm, LANES), lambda b, i: (i, 0)),
            pl.BlockSpec((LANES, tm), lambda b, i: (0, i)),
            pl.BlockSpec((LANES, tm), lambda b, i: (0, i)),
            pl.BlockSpec((tm, MOBA_KPAD), lambda b, i: (i, 0)),
        ],
        out_specs=[feat(MLA_QPAD), tok(MLA_QPAD), feat(MLA_OUT),
                   feat(MOBA_OUT), tok(MOBA_KPAD), feat(MOBA_OUT)] + [stream(dil) for dil in dils],
        out_shape=[feat_sds(MLA_QPAD), tok_sds(MLA_QPAD), feat_sds(MLA_OUT),
                   feat_sds(MOBA_OUT), tok_sds(MOBA_KPAD), feat_sds(MOBA_OUT)]
        + [stream_sds(dil) for dil in dils],
        scratch_shapes=[pltpu.VMEM((DIL_COLS // LANES, tm, LANES), F32)],
        compiler_params=_cparams(("parallel", "parallel")),
        name="inproj",
    )(x3, gain, lw["w1"], lw["wt"], lw["qn"], lw["wuqt"], lw["kvn"], lw["wk"], lw["wvt"],
      cos_t, sin_t, cos_tt, sin_tt, kaux)


SUM_ROWS = 16


def _flash_heads(nh, i, t, dv, score_fn, value_fn, bias_fn, diag_fn, s_buf, p_buf):
    heads = range(nh)
    ones = jnp.ones((SUM_ROWS, t), BF16)

    def softmax(h, m, bias):
        s = bias(h, s_buf[h])
        m_new = jnp.maximum(m, jnp.max(s, axis=0, keepdims=True))
        p_buf[h] = jnp.exp2(s - m_new).astype(BF16)
        return m_new, jnp.exp2(m - m_new)

    def values(h, j, alpha, l, acc):
        vt = jnp.concatenate([value_fn(jnp.maximum(j, 0), h), ones], axis=0)
        pv = _dot(vt, p_buf[h])
        return alpha * l + pv[dv:dv + 1], alpha * acc + pv[:dv]

    def all_heads(fn, *state):
        return tuple(zip(*[fn(h, *[x[h] for x in state]) for h in heads]))

    @pl.when(i == 0)
    def _():
        p_buf[...] = jnp.zeros(p_buf.shape, BF16)

    for h in heads:
        s_buf[h] = score_fn(0, h)
    row = lambda v: tuple(jnp.full((1, t), v, F32) for _ in heads)
    carry = (row(NEG_INF), row(1.0), row(0.0), tuple(jnp.zeros((dv, t), F32) for _ in heads))

    def body(j, carry):
        def head_step(h, m, alpha, l, acc):
            sc = score_fn(j + 1, h)
            l, acc = values(h, j - 1, alpha, l, acc)
            m, alpha = softmax(h, m, lambda h, s: bias_fn(j, h, s))
            s_buf[h] = sc
            return m, alpha, l, acc
        return all_heads(head_step, *carry)

    m, alpha, l, acc = lax.fori_loop(0, i, body, carry)
    l, acc = all_heads(lambda h, a, l, c: values(h, i - 1, a, l, c), alpha, l, acc)
    m, alpha = all_heads(lambda h, m: softmax(h, m, diag_fn), m)
    l, acc = all_heads(lambda h, a, l, c: values(h, i, a, l, c), alpha, l, acc)
    return [acc[h] / l[h] for h in heads]


def _store_token_major(o_ref, outs):
    for p in range(len(outs) // 2):
        pair = jnp.concatenate([outs[2 * p], outs[2 * p + 1]], axis=0)
        o_ref[0, :, p * LANES:(p + 1) * LANES] = pair.T.astype(o_ref.dtype)


def _flash_scratch(nh, t):
    return [pltpu.VMEM((nh, t, t), F32), pltpu.VMEM((nh, t, t), BF16)]


def _causal_mask_t(t):
    key = lax.broadcasted_iota(jnp.int32, (t, t), 0)
    qry = lax.broadcasted_iota(jnp.int32, (t, t), 1)
    return key <= qry


def _mla_kernel(qt_ref, k_ref, vt_ref, o_ref, s_buf, p_buf):
    i = pl.program_id(2)
    t = MLA_T
    nh = MLA_NH
    qs = [qt_ref[0, h * LANES:(h + 1) * LANES, :] for h in range(nh)]

    def scores(j, h):
        start = pl.multiple_of(j * t, t)
        return _dot(k_ref[0, pl.ds(start, t), h * LANES:(h + 1) * LANES], qs[h])

    def values(j, h):
        return vt_ref[0, h * MLA_V:(h + 1) * MLA_V, pl.ds(pl.multiple_of(j * t, t), t)]

    causal = _causal_mask_t(t)
    outs = _flash_heads(nh, i, t, MLA_V, scores, values,
                        lambda j, h, s: s,
                        lambda h, s: jnp.where(causal, s, NEG_INF),
                        s_buf, p_buf)
    _store_token_major(o_ref, outs)


def _mla(mqt, mk, mvt):
    batch, seq, _ = mk.shape
    t = MLA_T
    nh = MLA_NH
    return pl.pallas_call(
        _mla_kernel,
        grid=(batch, MLA_HEADS // nh, seq // t),
        in_specs=[pl.BlockSpec((1, nh * LANES, t), lambda b, g, i: (b, g, i)),
                  pl.BlockSpec((1, seq, nh * LANES), lambda b, g, i: (b, 0, g)),
                  pl.BlockSpec((1, nh * MLA_V, seq), lambda b, g, i: (b, g, 0))],
        out_specs=pl.BlockSpec((1, t, nh * MLA_V), lambda b, g, i: (b, i, g)),
        out_shape=jax.ShapeDtypeStruct((batch, seq, MLA_OUT), BF16),
        scratch_shapes=_flash_scratch(nh, t),
        compiler_params=_cparams(("parallel", "parallel", "arbitrary")),
        name="mla_attn",
    )(mqt, mk, mvt)


def _moba_kernel(qt_ref, k_ref, vt_ref, o_ref, kmh_ref, kml_ref, s_buf, p_buf, *, nblk):
    i = pl.program_id(2)
    t = MOBA_BLOCK
    nh = MOBA_NH

    @pl.when(i == 0)
    def _():
        kf = k_ref[0].astype(F32).reshape(nblk, t, nh * LANES)
        km = jnp.sum(kf, axis=1) * (1.0 / t)
        hi = km.astype(BF16)
        kmh_ref[...] = hi
        kml_ref[...] = (km - hi.astype(F32)).astype(BF16)

    blk = lax.broadcasted_iota(jnp.int32, (nblk, 1), 0)
    blk_f = blk.astype(F32)
    pos_rows =lax.broadcasted_iota(jnp.int32, (AUX_SEL - AUX_POS, 1), 0) < AUX_POS_PARTS
    ones_rows = jnp.broadcast_to(jnp.where(pos_rows, 1.0, 0.0), (AUX_SEL - AUX_POS, t)).astype(BF16)
    pad_q = jnp.zeros((LANES - HEAD_DIM, t), BF16)
    pad_aug = jnp.zeros((LANES - AUX_SEL - nblk, t), BF16)
    qs = []
    for h in range(nh):
        q = qt_ref[0, h * HEAD_DIM:(h + 1) * HEAD_DIM, :]
        q_plain = jnp.concatenate([q, pad_q], axis=0)
        cols = slice(h * LANES, (h + 1) * LANES)
        gate = _dot(kmh_ref[:, cols], q_plain) + _dot(kml_ref[:, cols], q_plain)
        gate = jnp.where(blk < i, gate, NEG_INF)
        chosen = blk == i
        for _ in range(min(MOBA_TOPK, nblk)):
            best = jnp.max(gate, axis=0, keepdims=True)
            first = jnp.min(jnp.where(gate == best, blk_f, float(nblk)), axis=0, keepdims=True)
            pick = blk_f == first
            chosen = chosen | (pick & (blk < i))
            gate = jnp.where(pick, -jnp.inf, gate)
        sel = jnp.where(chosen, 0.0, NEG_INF).astype(BF16)
        qs.append(jnp.concatenate([q, ones_rows, sel, pad_aug], axis=0))

    def scores(j, h):
        start = pl.multiple_of(j * t, t)
        return _dot(k_ref[0, pl.ds(start, t), h * LANES:(h + 1) * LANES], qs[h])

    def values(j, h):
        return vt_ref[0, h * HEAD_DIM:(h + 1) * HEAD_DIM, pl.ds(pl.multiple_of(j * t, t), t)]

    causal = _causal_mask_t(t)
    outs = _flash_heads(nh, i, t, HEAD_DIM, scores, values,
                        lambda j, h, s: s,
                        lambda h, s: jnp.where(causal, s, NEG_INF),
                        s_buf, p_buf)
    _store_token_major(o_ref, outs)


def _moba(bqt, bk, bvt):
    batch, seq, _ = bk.shape
    t = MOBA_BLOCK
    nh = MOBA_NH
    nblk = seq // t
    assert AUX_SEL + nblk <= LANES
    cols = nh * HEAD_DIM
    return pl.pallas_call(
        functools.partial(_moba_kernel, nblk=nblk),
        grid=(batch, MOBA_HEADS // nh, nblk),
        in_specs=[
            pl.BlockSpec((1, cols, t), lambda b, g, i: (b, g, i)),
            pl.BlockSpec((1, seq, nh * LANES), lambda b, g, i: (b, 0, g)),
            pl.BlockSpec((1, cols, seq), lambda b, g, i: (b, g, 0)),
        ],
        out_specs=pl.BlockSpec((1, t, cols), lambda b, g, i: (b, i, g)),
        out_shape=jax.ShapeDtypeStruct((batch, seq, MOBA_OUT), BF16),
        scratch_shapes=[
            pltpu.VMEM((nblk, nh * LANES), BF16),
            pltpu.VMEM((nblk, nh * LANES), BF16),
        ] + _flash_scratch(nh, t),
        compiler_params=_cparams(("parallel", "parallel", "arbitrary")),
        name="moba_attn",
    )(bqt, bk, bvt)


def _dil_kernel(own_ref, prev_ref, bias_ref, o_ref, lse_ref, *, ns, qb):
    n = pl.program_id(2)
    L = DIL_L
    low = lax.broadcasted_iota(jnp.int32, (1, LANES), 1) < HEAD_DIM
    first_cols = lax.broadcasted_iota(jnp.int32, (1, 2 * L), 1) < L
    problems = [(si, pair, blk) for si in range(ns) for pair in range(DIL_OUT // LANES)
                for blk in range(qb // L)]

    def window(si, blk, cols):
        if blk == 0:
            return jnp.concatenate([prev_ref[0, si, :, cols], own_ref[0, si, :L, cols]], axis=0)
        return own_ref[0, si, (blk - 1) * L:(blk + 1) * L, cols]

    scores = []
    for si, pair, blk in problems:
        qp = own_ref[0, si, blk * L:(blk + 1) * L, pair * LANES:(pair + 1) * LANES]
        zero = jnp.zeros_like(qp)
        keys = window(si, blk, slice(DIL_OUT + pair * LANES, DIL_OUT + (pair + 1) * LANES))
        scores.append([_dot_nt(jnp.where(low, qp, zero), keys),
                       _dot_nt(jnp.where(low, zero, qp), keys)])

    probs = []
    for (si, pair, blk), sc in zip(problems, scores):
        per_head = []
        for hh in range(2):
            s = sc[hh] + bias_ref[2 * pair + hh]
            if blk == 0:
                s = jnp.where(first_cols & (n == 0), NEG_INF, s)
            m = jnp.max(s, axis=-1, keepdims=True)
            e = jnp.exp(s - m)
            den = jnp.sum(e, axis=-1, keepdims=True)
            per_head.append((e.astype(BF16), den, m + jnp.log(den)))
        probs.append(per_head)

    for (si, pair, blk), per_head in zip(problems, probs):
        vals = window(si, blk, slice(2 * DIL_OUT + pair * LANES, 2 * DIL_OUT + (pair + 1) * LANES))
        o_pair = [_dot(e, vals) / den for e, den, _ in per_head]
        lse_pair = [jnp.broadcast_to(lse, (L, LANES)) for _, _, lse in per_head]
        dst = (0, si, slice(blk * L, (blk + 1) * L), slice(pair * LANES, (pair + 1) * LANES))
        o_ref[dst] = jnp.where(low, o_pair[0], o_pair[1])
        lse_ref[dst] = jnp.where(low, lse_pair[0], lse_pair[1])


def _dil(dg, bias):
    batch, dilation, rows, _ = dg.shape
    L = DIL_L
    qb = min(DIL_STEP_ROWS, rows)
    ns = min(DIL_STEP_ROWS // qb, dilation)
    per = qb // L
    out_sds = jax.ShapeDtypeStruct((batch, dilation, rows, DIL_OUT), F32)
    return pl.pallas_call(
        functools.partial(_dil_kernel, ns=ns, qb=qb),
        grid=(batch, dilation // ns, rows // qb),
        in_specs=[
            pl.BlockSpec((1, ns, qb, DIL_COLS), lambda b, r, n: (b, r, n, 0)),
            pl.BlockSpec((1, ns, L, DIL_COLS),
                         lambda b, r, n: (b, r, jnp.maximum(n * per - 1, 0), 0)),
            _resident(bias.shape),
        ],
        out_specs=[pl.BlockSpec((1, ns, qb, DIL_OUT), lambda b, r, n: (b, r, n, 0))] * 2,
        out_shape=[out_sds, out_sds],
        compiler_params=_cparams(("parallel", "parallel", "arbitrary")),
        name="dil_attn_d%d" % dilation,
    )(dg, dg, bias)


def _merge_kernel(x_ref, gn_ref, wgate_ref, a_ref, b_ref, o0_ref, o1_ref, o2_ref,
                  l0_ref, l1_ref, l2_ref, wa_ref, wb_ref, wc_ref, wo_ref, out_ref, *scratch):
    x = x_ref[0]
    h = _rms(x, gn_ref[...]).astype(BF16)
    tm = x.shape[0]
    scratch = list(scratch)

    def token_order(ref):
        dil = ref.shape[1]
        if dil == 1:
            return ref[0, 0]
        buf = scratch.pop()
        for c in range(buf.shape[0]):
            for r in range(dil):
                buf[c, pl.ds(r, tm // dil, stride=dil), :] = ref[0, r, :, c * LANES:(c + 1) * LANES]
        return jnp.concatenate([buf[c] for c in range(buf.shape[0])], axis=1)

    lses = [token_order(r) for r in (l0_ref, l1_ref, l2_ref)]
    outs = [token_order(r) for r in (o0_ref, o1_ref, o2_ref)]
    mx = jnp.maximum(jnp.maximum(lses[0], lses[1]), lses[2])
    es = [jnp.exp(l - mx) for l in lses]
    den = es[0] + es[1] + es[2]
    c = (es[0] / den) * outs[0] + (es[1] / den) * outs[1] + (es[2] / den) * outs[2]

    branches = ((a_ref[0], wa_ref), (b_ref[0], wb_ref), (c.astype(BF16), wc_ref))
    merged = jnp.zeros(x.shape, F32)
    for k, (act, w_ref) in enumerate(branches):
        gate = jax.nn.sigmoid(_dot(h, wgate_ref[:, k * D_MODEL:(k + 1) * D_MODEL]))
        merged = merged + gate * _dot(act, w_ref[...])
    out_ref[0] = x + _dot(merged.astype(BF16), wo_ref[...])


def _merge(x3, gain, wgate, a, b, dil_outs, wa, wb, wc, wo):
    batch, seq, _ = x3.shape
    tm = TM_MERGE
    tok = lambda c: pl.BlockSpec((1, tm, c), lambda b, i: (b, i, 0))
    stream = lambda arr: pl.BlockSpec((1, arr.shape[1], tm // arr.shape[1], DIL_OUT),
                                      lambda b, i: (b, 0, i, 0))
    (o0, l0), (o1, l1), (o2, l2) = dil_outs
    streams = (o0, o1, o2, l0, l1, l2)
    n_reordered = sum(arr.shape[1] > 1 for arr in streams)
    return pl.pallas_call(
        _merge_kernel,
        grid=(batch, seq // tm),
        in_specs=[tok(D_MODEL), _resident((1, D_MODEL)), _resident(wgate.shape),
                  tok(MLA_OUT), tok(MOBA_OUT)]
        + [stream(arr) for arr in streams]
        + [_resident(wa.shape), _resident(wb.shape), _resident(wc.shape), _resident(wo.shape)],
        out_specs=tok(D_MODEL),
        out_shape=jax.ShapeDtypeStruct((batch, seq, D_MODEL), F32),
        scratch_shapes=[pltpu.VMEM((DIL_OUT // LANES, tm, LANES), F32)] * n_reordered,
        compiler_params=_cparams(("parallel", "parallel")),
        name="merge",
    )(x3, gain, wgate, a, b, *streams, wa, wb, wc, wo)


def _alibi_slopes():
    return (2.0 ** (-8.0 * np.arange(1, N_ALIBI + 1, dtype=np.float32) / N_ALIBI)).astype(np.float32)


def _rope_tables(seq):
    pos = jnp.arange(seq, dtype=F32)
    inv = ROPE_THETA ** (-jnp.arange(0, MLA_ROPE, 2, dtype=F32) / MLA_ROPE)
    ang = pos[:, None] * inv[None, :]
    cos, sin = jnp.cos(ang), jnp.sin(ang)
    pad = jnp.zeros((seq, LANES - MLA_QK), F32)
    cos_t = jnp.concatenate([jnp.ones((seq, MLA_NOPE), F32), cos, cos, pad], axis=1)
    sin_t = jnp.concatenate([jnp.zeros((seq, MLA_NOPE), F32), -sin, sin, pad], axis=1)
    return cos_t, sin_t, cos_t.T, sin_t.T


def _dil_bias(group, dilation):
    L = DIL_L
    slopes = _alibi_slopes()[group * DIL_GROUP_HEADS:(group + 1) * DIL_GROUP_HEADS]
    steps = L + np.arange(L)[:, None] - np.arange(2 * L)[None, :]
    valid = (steps >= 0) & (steps <= L)
    dist = (steps * dilation).astype(np.float32)
    bias = np.where(valid[None], -slopes[:, None, None] * dist[None], np.float32(NEG_INF))
    return jnp.asarray(bias.astype(np.float32))


def _moba_key_aux(seq):
    slopes = _alibi_slopes()[DIL_HEADS:].astype(np.float64) * LOG2E
    pos = jnp.asarray((slopes[None, :] * np.arange(seq)[:, None]).astype(np.float32))
    aux = jnp.zeros((seq, MOBA_HEADS, LANES), F32)
    rest = pos
    for part in range(AUX_POS_PARTS):
        piece = rest.astype(BF16).astype(F32)
        aux = aux.at[:, :, AUX_POS + part].set(piece)
        rest = rest - piece
    onehot = jax.nn.one_hot(jnp.arange(seq) // MOBA_BLOCK, seq // MOBA_BLOCK, dtype=F32)
    aux = aux.at[:, :, AUX_SEL:AUX_SEL + seq // MOBA_BLOCK].set(onehot[:, None, :])
    return aux.reshape(seq, MOBA_KPAD).astype(BF16)


def _layer_weights(w_in, w_uq, w_ukv, q_norm, kv_norm):
    d = w_in.shape[0]
    z = lambda c: jnp.zeros((d, c), w_in.dtype)
    i0 = MLA_Q_RANK + MLA_KV_RANK
    kr = w_in[:, i0:i0 + MLA_ROPE]
    hr = MLA_ROPE // 2
    kr_pad = jnp.concatenate([z(MLA_NOPE), kr, z(LANES - MLA_QK)], axis=1)
    kr_swap = jnp.concatenate([z(MLA_NOPE), kr[:, hr:], kr[:, :hr], z(LANES - MLA_QK)], axis=1)
    moba = w_in[:, MLA_IN:MLA_IN + MOBA_IN]
    bq, bk, bv = (moba[:, c * MOBA_OUT:(c + 1) * MOBA_OUT] for c in range(3))
    bk = jnp.concatenate([bk.reshape(d, MOBA_HEADS, HEAD_DIM),
                          jnp.zeros((d, MOBA_HEADS, LANES - HEAD_DIM), w_in.dtype)],
                         axis=2).reshape(d, MOBA_KPAD)
    dil = w_in[:, MLA_IN + MOBA_IN:MLA_IN + MOBA_IN + DIL_IN].reshape(d, 3, DIL_HEADS, HEAD_DIM)
    groups = [dil[:, :, g * DIL_GROUP_HEADS:(g + 1) * DIL_GROUP_HEADS].reshape(d, DIL_COLS)
              for g in range(len(DIL_PATTERNS))]
    w1 = jnp.concatenate([w_in[:, :i0], kr_pad, kr_swap, bk] + groups, axis=1).astype(BF16)
    wt = jnp.concatenate([bq, bv], axis=1).T.astype(BF16)
    wgate = w_in[:, MLA_IN + MOBA_IN + DIL_IN:].astype(BF16)

    r = w_uq.shape[0]
    uq = w_uq.reshape(r, MLA_HEADS, MLA_QK)
    zq = jnp.zeros((r, MLA_HEADS, LANES - MLA_QK), w_uq.dtype)
    zn = jnp.zeros((r, MLA_HEADS, MLA_NOPE), w_uq.dtype)
    uq_pad = jnp.concatenate([uq, zq], axis=2).reshape(r, MLA_QPAD)
    uq_swap = jnp.concatenate(
        [zn, uq[:, :, MLA_NOPE + hr:], uq[:, :, MLA_NOPE:MLA_NOPE + hr], zq], axis=2
    ).reshape(r, MLA_QPAD)
    wuqt = jnp.concatenate([uq_pad, uq_swap], axis=1).T.astype(BF16)

    rk = w_ukv.shape[0]
    ukv = w_ukv.reshape(rk, MLA_HEADS, MLA_NOPE + MLA_V)
    zk = jnp.zeros((rk, MLA_HEADS, LANES - MLA_NOPE), w_ukv.dtype)
    wk = jnp.concatenate([ukv[:, :, :MLA_NOPE], zk], axis=2).reshape(rk, MLA_QPAD).astype(BF16)
    wvt = ukv[:, :, MLA_NOPE:].reshape(rk, MLA_OUT).T.astype(BF16)
    return dict(w1=w1, wt=wt, wgate=wgate, wuqt=wuqt, wk=wk, wvt=wvt,
                qn=q_norm.reshape(1, -1), kvn=kv_norm.reshape(1, -1))


def kernel(x, ffn1_norm, ffn1_w_gate, ffn1_w_up, ffn1_w_down, mix_norm, w_in, q_norm, w_uq, kv_norm, w_ukv, w_br_mla, w_br_moba, w_br_dil, w_out, ffn2_norm, ffn2_w_gate, ffn2_w_up, ffn2_w_down, final_norm):
    batch, seq, d = x.shape
    assert d == D_MODEL and seq % (DIL_PATTERNS[-1][1] * DIL_L) == 0 and seq % TM_PROJ == 0
    n = batch * seq
    x2 = x.reshape(n, d)
    tables = _rope_tables(seq) + (_moba_key_aux(seq),)
    dil_bias = [_dil_bias(g, dil) for g, (_, dil) in enumerate(DIL_PATTERNS)]
    final_gain = final_norm.reshape(1, d)
    bf = lambda w: w.astype(BF16)
    row = lambda v: v.reshape(1, -1)

    for l in range(DEPTH):
        x2 = _ffn(x2, row(ffn1_norm[l]), bf(ffn1_w_gate[l]), bf(ffn1_w_up[l]), bf(ffn1_w_down[l]),
                  final_gain, False)
        lw = _layer_weights(w_in[l], w_uq[l], w_ukv[l], q_norm[l], kv_norm[l])
        mqt, mk, mvt, bqt, bk, bvt, d0, d1, d2 = _inproj(
            x2.reshape(batch, seq, d), row(mix_norm[l]), lw, tables)
        a = _mla(mqt, mk, mvt)
        b = _moba(bqt, bk, bvt)
        dil_outs = [_dil(dg, dil_bias[g]) for g, dg in enumerate((d0, d1, d2))]
        x2 = _merge(x2.reshape(batch, seq, d), row(mix_norm[l]), lw["wgate"], a, b, dil_outs,
                    bf(w_br_mla[l]), bf(w_br_moba[l]), bf(w_br_dil[l]), bf(w_out[l])).reshape(n, d)
        x2 = _ffn(x2, row(ffn2_norm[l]), bf(ffn2_w_gate[l]), bf(ffn2_w_up[l]), bf(ffn2_w_down[l]),
                  final_gain, l == DEPTH - 1)
    return x2.reshape(batch, seq, d)
```

```python
import functools

import numpy as np
import jax
import jax.numpy as jnp
from jax import lax
from jax.experimental import pallas as pl
from jax.experimental.pallas import tpu as pltpu

F32 = jnp.float32
BF16 = jnp.bfloat16

D_MODEL = 1024
DEPTH = 4
HEAD_DIM = 64
MLA_HEADS = 6
MLA_NOPE = 64
MLA_ROPE = 32
MLA_V = 64
MLA_Q_RANK = 256
MLA_KV_RANK = 128
ROPE_THETA = 10000.0
MOBA_HEADS = 6
MOBA_BLOCK = 256
MOBA_TOPK = 3
DIL_PATTERNS = ((128, 1), (512, 4), (2048, 16))
DIL_GROUP_HEADS = 4
DIL_HEADS = DIL_GROUP_HEADS * len(DIL_PATTERNS)
N_ALIBI = DIL_HEADS + MOBA_HEADS
N_BRANCHES = 3
D_FF = 2816
NORM_EPS = 1e-6
NEG_INF = -1e30
MLA_IN = MLA_Q_RANK + MLA_KV_RANK + MLA_ROPE
MOBA_IN = 3 * MOBA_HEADS * HEAD_DIM
DIL_IN = 3 * DIL_HEADS * HEAD_DIM
MLA_OUT = MLA_HEADS * MLA_V
MOBA_OUT = MOBA_HEADS * HEAD_DIM
DIL_OUT = DIL_GROUP_HEADS * HEAD_DIM

LANES = 128
MLA_QK = MLA_NOPE + MLA_ROPE
MLA_QPAD = MLA_HEADS * LANES
LOG2E = 1.4426950408889634
MLA_SCALE = MLA_QK ** -0.5
HEAD_SCALE = HEAD_DIM ** -0.5
DIL_L = DIL_PATTERNS[0][0] // DIL_PATTERNS[0][1]
DIL_COLS = 3 * DIL_GROUP_HEADS * HEAD_DIM

C_CQ = 0
C_CKV = C_CQ + MLA_Q_RANK
C_KR = C_CKV + MLA_KV_RANK
C_KR2 = C_KR + LANES
C_MOBA_K = C_KR2 + LANES
MOBA_KPAD = MOBA_HEADS * LANES
C_DIL = C_MOBA_K + MOBA_KPAD
C_END = C_DIL + DIL_IN

AUX_POS = HEAD_DIM
AUX_POS_PARTS = 3
AUX_SEL = HEAD_DIM + 16

VMEM_LIMIT = 56 * 1024 * 1024

TM_FFN = 512
TM_PROJ = 512
TM_MERGE = 512
FF_CHUNK = 256
MLA_T = 256
MLA_NH = 6
MOBA_NH = 6
DIL_STEP_ROWS = 512


def _cparams(sem):
    return pltpu.CompilerParams(dimension_semantics=sem, vmem_limit_bytes=VMEM_LIMIT)


def _resident(shape):
    nd = len(shape)
    return pl.BlockSpec(shape, lambda *_: (0,) * nd, pipeline_mode=pl.Buffered(1))


def _rms(x, gain):
    ms = jnp.mean(x * x, axis=-1, keepdims=True)
    return x * lax.rsqrt(ms + NORM_EPS) * gain


def _dot(a, b):
    return jnp.dot(a, b, preferred_element_type=F32)


def _dot_nt(a, b):
    return lax.dot_general(a, b, (((1,), (1,)), ((), ())), preferred_element_type=F32)


def _ffn_kernel(x_ref, g_ref, wg_ref, wu_ref, wd_ref, fg_ref, o_ref, *, final):
    x = x_ref[...]
    h = _rms(x, g_ref[...]).astype(BF16)
    acc = jnp.zeros(x.shape, F32)
    for c in range(D_FF // FF_CHUNK):
        sl = slice(c * FF_CHUNK, (c + 1) * FF_CHUNK)
        a = _dot(h, wg_ref[:, sl])
        u = _dot(h, wu_ref[:, sl])
        g = (a * jax.nn.sigmoid(a) * u).astype(BF16)
        acc = acc + _dot(g, wd_ref[sl, :])
    y = x + 0.5 * acc
    if final:
        y = _rms(y, fg_ref[...])
    o_ref[...] = y


def _ffn(x2, gain, wg, wu, wd, final_gain, final):
    n = x2.shape[0]
    return pl.pallas_call(
        functools.partial(_ffn_kernel, final=final),
        grid=(n // TM_FFN,),
        in_specs=[
            pl.BlockSpec((TM_FFN, D_MODEL), lambda i: (i, 0)),
            _resident((1, D_MODEL)),
            _resident((D_MODEL, D_FF)),
            _resident((D_MODEL, D_FF)),
            _resident((D_FF, D_MODEL)),
            _resident((1, D_MODEL)),
        ],
        out_specs=pl.BlockSpec((TM_FFN, D_MODEL), lambda i: (i, 0)),
        out_shape=jax.ShapeDtypeStruct((n, D_MODEL), F32),
        compiler_params=_cparams(("parallel",)),
        name="ffn_final" if final else "ffn",
    )(x2, gain, wg, wu, wd, final_gain)


def _inproj_kernel(x_ref, gn_ref, w1_ref, wt_ref, qn_ref, wuqt_ref, kvn_ref, wk_ref, wvt_ref,
                   cos_ref, sin_ref, cost_ref, sint_ref, kaux_ref,
                   mqt_ref, mk_ref, mvt_ref, bqt_ref, bk_ref, bvt_ref, d0_ref, d1_ref, d2_ref,
                   stage_ref):
    x = x_ref[0]
    h = _rms(x, gn_ref[...]).astype(BF16)

    pm = _dot(h, w1_ref[:, C_CQ:C_MOBA_K])
    cq = _rms(pm[:, C_CQ:C_CKV], qn_ref[...]).astype(BF16)
    ckv = _rms(pm[:, C_CKV:C_KR], kvn_ref[...]).astype(BF16)
    k_rope = pm[:, C_KR:C_KR2] * cos_ref[...] + pm[:, C_KR2:C_MOBA_K] * sin_ref[...]
    kk = _dot(ckv, wk_ref[...])
    qqt = _dot_nt(wuqt_ref[...], cq)
    cost = cost_ref[...]
    sint = sint_ref[...]
    for hh in range(MLA_HEADS):
        sl = slice(hh * LANES, (hh + 1) * LANES)
        sl2 = slice(MLA_QPAD + hh * LANES, MLA_QPAD + (hh + 1) * LANES)
        mqt_ref[0, sl, :] = ((qqt[sl] * cost + qqt[sl2] * sint) * (MLA_SCALE * LOG2E)).astype(BF16)
        mk_ref[0, :, sl] = (kk[:, sl] + k_rope).astype(BF16)
    mvt_ref[0] = _dot_nt(wvt_ref[...], ckv).astype(BF16)

    bt = _dot_nt(wt_ref[...], h)
    bqt_ref[0] = (bt[:MOBA_OUT] * (HEAD_SCALE * LOG2E)).astype(BF16)
    bvt_ref[0] = bt[MOBA_OUT:].astype(BF16)
    bk_ref[0] = (_dot(h, w1_ref[:, C_MOBA_K:C_DIL]) + kaux_ref[...].astype(F32)).astype(BF16)

    tm = x.shape[0]
    for g, d_ref in enumerate((d0_ref, d1_ref, d2_ref)):
        dil = DIL_PATTERNS[g][1]
        dd = _dot(h, w1_ref[:, C_DIL + g * DIL_COLS:C_DIL + (g + 1) * DIL_COLS])
        for c in range(DIL_COLS // LANES):
            cols = slice(c * LANES, (c + 1) * LANES)
            blk = dd[:, cols]
            stage_ref[c] = blk * HEAD_SCALE if c < DIL_OUT // LANES else blk
            for r in range(dil):
                d_ref[0, r, :, cols] = stage_ref[c, pl.ds(r, tm // dil, stride=dil), :].astype(BF16)


def _inproj(x3, gain, lw, tables):
    batch, seq, _ = x3.shape
    tm = TM_PROJ
    cos_t, sin_t, cos_tt, sin_tt, kaux = tables
    tok = lambda c: pl.BlockSpec((1, tm, c), lambda b, i: (b, i, 0))
    feat = lambda c: pl.BlockSpec((1, c, tm), lambda b, i: (b, 0, i))
    tok_sds = lambda c: jax.ShapeDtypeStruct((batch, seq, c), BF16)
    feat_sds = lambda c: jax.ShapeDtypeStruct((batch, c, seq), BF16)
    dils = [dil for _, dil in DIL_PATTERNS]
    stream = lambda dil: pl.BlockSpec((1, dil, tm // dil, DIL_COLS), lambda b, i: (b, 0, i, 0))
    stream_sds = lambda dil: jax.ShapeDtypeStruct((batch, dil, seq // dil, DIL_COLS), BF16)
    return pl.pallas_call(
        _inproj_kernel,
        grid=(batch, seq // tm),
        in_specs=[
            tok(D_MODEL),
            _resident((1, D_MODEL)),
            _resident(lw["w1"].shape),
            _resident(lw["wt"].shape),
            _resident((1, MLA_Q_RANK)),
            _resident(lw["wuqt"].shape),
            _resident((1, MLA_KV_RANK)),
            _resident(lw["wk"].shape),
            _resident(lw["wvt"].shape),
            pl.BlockSpec((tm, LANES), lambda b, i: (i, 0)),
            pl.BlockSpec((tm, LANES), lambda b, i: (i, 0)),
            pl.BlockSpec((LANES, tm), lambda b, i: (0, i)),
            pl.BlockSpec((LANES, tm), lambda b, i: (0, i)),
            pl.BlockSpec((tm, MOBA_KPAD), lambda b, i: (i, 0)),
        ],
        out_specs=[feat(MLA_QPAD), tok(MLA_QPAD), feat(MLA_OUT),
                   feat(MOBA_OUT), tok(MOBA_KPAD), feat(MOBA_OUT)] + [stream(dil) for dil in dils],
        out_shape=[feat_sds(MLA_QPAD), tok_sds(MLA_QPAD), feat_sds(MLA_OUT),
                   feat_sds(MOBA_OUT), tok_sds(MOBA_KPAD), feat_sds(MOBA_OUT)]
        + [stream_sds(dil) for dil in dils],
        scratch_shapes=[pltpu.VMEM((DIL_COLS // LANES, tm, LANES), F32)],
        compiler_params=_cparams(("parallel", "parallel")),
        name="inproj",
    )(x3, gain, lw["w1"], lw["wt"], lw["qn"], lw["wuqt"], lw["kvn"], lw["wk"], lw["wvt"],
      cos_t, sin_t, cos_tt, sin_tt, kaux)


SUM_ROWS = 16


ROW_M, ROW_ALPHA, ROW_L, ROW_SMAX = range(4)


def _flash_heads(nh, i, t, dv, score_fn, value_fn, diag_fn, s_buf, p_buf, acc_buf, stat_buf):
    heads = range(nh)
    ones = jnp.ones((SUM_ROWS, t), BF16)
    col_max = lambda s: jnp.max(s, axis=0, keepdims=True)
    stat = lambda h, r: stat_buf[h, r:r + 1, :]

    def set_stat(h, r, v):
        stat_buf[h, r:r + 1, :] = v

    def values(h, j):
        vt = jnp.concatenate([value_fn(jnp.maximum(j, 0), h), ones], axis=0)
        pv = _dot(vt, p_buf[h])
        alpha = stat(h, ROW_ALPHA)
        acc_buf[h] = alpha * acc_buf[h] + pv[:dv]
        set_stat(h, ROW_L, alpha * stat(h, ROW_L) + pv[dv:dv + 1])

    def softmax(h, s, s_max):
        m = stat(h, ROW_M)
        m_new = jnp.maximum(m, s_max)
        p_buf[h] = jnp.exp2(s - m_new).astype(BF16)
        set_stat(h, ROW_M, m_new)
        set_stat(h, ROW_ALPHA, jnp.exp2(m - m_new))

    @pl.when(i == 0)
    def _():
        p_buf[...] = jnp.zeros(p_buf.shape, BF16)

    for h in heads:
        sc = score_fn(0, h)
        s_buf[h] = sc
        set_stat(h, ROW_SMAX, col_max(sc))
        set_stat(h, ROW_M, jnp.full((1, t), NEG_INF, F32))
        set_stat(h, ROW_ALPHA, jnp.ones((1, t), F32))
        set_stat(h, ROW_L, jnp.zeros((1, t), F32))
        acc_buf[h] = jnp.zeros((dv, t), F32)

    @pl.loop(0, i)
    def _(j):
        for h in heads:
            sc = score_fn(j + 1, h)
            values(h, j - 1)
            softmax(h, s_buf[h], stat(h, ROW_SMAX))
            s_buf[h] = sc
            set_stat(h, ROW_SMAX, col_max(sc))

    for h in heads:
        values(h, i - 1)
    for h in heads:
        s = diag_fn(h, s_buf[h])
        softmax(h, s, col_max(s))
    for h in heads:
        values(h, i)
    return [acc_buf[h] / stat(h, ROW_L) for h in heads]


def _store_token_major(o_ref, outs):
    for p in range(len(outs) // 2):
        pair = jnp.concatenate([outs[2 * p], outs[2 * p + 1]], axis=0)
        o_ref[0, :, p * LANES:(p + 1) * LANES] = pair.T.astype(o_ref.dtype)


def _flash_scratch(nh, t, dv):
    return [pltpu.VMEM((nh, t, t), F32), pltpu.VMEM((nh, t, t), BF16),
            pltpu.VMEM((nh, dv, t), F32), pltpu.VMEM((nh, 8, t), F32)]


def _causal_mask_t(t):
    key = lax.broadcasted_iota(jnp.int32, (t, t), 0)
    qry = lax.broadcasted_iota(jnp.int32, (t, t), 1)
    return key <= qry


def _mla_kernel(qt_ref, k_ref, vt_ref, o_ref, *flash_scratch):
    i = pl.program_id(2)
    t = MLA_T
    nh = MLA_NH
    qs = [qt_ref[0, h * LANES:(h + 1) * LANES, :] for h in range(nh)]

    def scores(j, h):
        start = pl.multiple_of(j * t, t)
        return _dot(k_ref[0, pl.ds(start, t), h * LANES:(h + 1) * LANES], qs[h])

    def values(j, h):
        return vt_ref[0, h * MLA_V:(h + 1) * MLA_V, pl.ds(pl.multiple_of(j * t, t), t)]

    causal = _causal_mask_t(t)
    outs = _flash_heads(nh, i, t, MLA_V, scores, values,
                        lambda h, s: jnp.where(causal, s, NEG_INF),
                        *flash_scratch)
    _store_token_major(o_ref, outs)


def _mla(mqt, mk, mvt):
    batch, seq, _ = mk.shape
    t = MLA_T
    nh = MLA_NH
    return pl.pallas_call(
        _mla_kernel,
        grid=(batch, MLA_HEADS // nh, seq // t),
        in_specs=[pl.BlockSpec((1, nh * LANES, t), lambda b, g, i: (b, g, i)),
                  pl.BlockSpec((1, seq, nh * LANES), lambda b, g, i: (b, 0, g)),
                  pl.BlockSpec((1, nh * MLA_V, seq), lambda b, g, i: (b, g, 0))],
        out_specs=pl.BlockSpec((1, t, nh * MLA_V), lambda b, g, i: (b, i, g)),
        out_shape=jax.ShapeDtypeStruct((batch, seq, MLA_OUT), BF16),
        scratch_shapes=_flash_scratch(nh, t, MLA_V),
        compiler_params=_cparams(("parallel", "parallel", "arbitrary")),
        name="mla_attn",
    )(mqt, mk, mvt)


def _moba_kernel(qt_ref, k_ref, vt_ref, o_ref, kmh_ref, kml_ref, *flash_scratch, nblk):
    i = pl.program_id(2)
    t = MOBA_BLOCK
    nh = MOBA_NH

    @pl.when(i == 0)
    def _():
        kf = k_ref[0].astype(F32).reshape(nblk, t, nh * LANES)
        km = jnp.sum(kf, axis=1) * (1.0 / t)
        hi = km.astype(BF16)
        kmh_ref[...] = hi
        kml_ref[...] = (km - hi.astype(F32)).astype(BF16)

    blk = lax.broadcasted_iota(jnp.int32, (nblk, 1), 0)
    blk_f = blk.astype(F32)
    pos_rows =lax.broadcasted_iota(jnp.int32, (AUX_SEL - AUX_POS, 1), 0) < AUX_POS_PARTS
    ones_rows = jnp.broadcast_to(jnp.where(pos_rows, 1.0, 0.0), (AUX_SEL - AUX_POS, t)).astype(BF16)
    pad_q = jnp.zeros((LANES - HEAD_DIM, t), BF16)
    pad_aug = jnp.zeros((LANES - AUX_SEL - nblk, t), BF16)
    qs = []
    for h in range(nh):
        q = qt_ref[0, h * HEAD_DIM:(h + 1) * HEAD_DIM, :]
        q_plain = jnp.concatenate([q, pad_q], axis=0)
        cols = slice(h * LANES, (h + 1) * LANES)
        gate = _dot(kmh_ref[:, cols], q_plain) + _dot(kml_ref[:, cols], q_plain)
        gate = jnp.where(blk < i, gate, NEG_INF)
        chosen = blk == i
        for _ in range(min(MOBA_TOPK, nblk)):
            best = jnp.max(gate, axis=0, keepdims=True)
            first = jnp.min(jnp.where(gate == best, blk_f, float(nblk)), axis=0, keepdims=True)
            pick = blk_f == first
            chosen = chosen | (pick & (blk < i))
            gate = jnp.where(pick, -jnp.inf, gate)
        sel = jnp.where(chosen, 0.0, NEG_INF).astype(BF16)
        qs.append(jnp.concatenate([q, ones_rows, sel, pad_aug], axis=0))

    def scores(j, h):
        start = pl.multiple_of(j * t, t)
        return _dot(k_ref[0, pl.ds(start, t), h * LANES:(h + 1) * LANES], qs[h])

    def values(j, h):
        return vt_ref[0, h * HEAD_DIM:(h + 1) * HEAD_DIM, pl.ds(pl.multiple_of(j * t, t), t)]

    causal = _causal_mask_t(t)
    outs = _flash_heads(nh, i, t, HEAD_DIM, scores, values,
                        lambda h, s: jnp.where(causal, s, NEG_INF),
                        *flash_scratch)
    _store_token_major(o_ref, outs)


def _moba(bqt, bk, bvt):
    batch, seq, _ = bk.shape
    t = MOBA_BLOCK
    nh = MOBA_NH
    nblk = seq // t
    assert AUX_SEL + nblk <= LANES
    cols = nh * HEAD_DIM
    return pl.pallas_call(
        functools.partial(_moba_kernel, nblk=nblk),
        grid=(batch, MOBA_HEADS // nh, nblk),
        in_specs=[
            pl.BlockSpec((1, cols, t), lambda b, g, i: (b, g, i)),
            pl.BlockSpec((1, seq, nh * LANES), lambda b, g, i: (b, 0, g)),
            pl.BlockSpec((1, cols, seq), lambda b, g, i: (b, g, 0)),
        ],
        out_specs=pl.BlockSpec((1, t, cols), lambda b, g, i: (b, i, g)),
        out_shape=jax.ShapeDtypeStruct((batch, seq, MOBA_OUT), BF16),
        scratch_shapes=[
            pltpu.VMEM((nblk, nh * LANES), BF16),
            pltpu.VMEM((nblk, nh * LANES), BF16),
        ] + _flash_scratch(nh, t, HEAD_DIM),
        compiler_params=_cparams(("parallel", "parallel", "arbitrary")),
        name="moba_attn",
    )(bqt, bk, bvt)


def _dil_kernel(own_ref, prev_ref, bias_ref, o_ref, lse_ref, *, ns, qb):
    n = pl.program_id(2)
    L = DIL_L
    low = lax.broadcasted_iota(jnp.int32, (1, LANES), 1) < HEAD_DIM
    first_cols = lax.broadcasted_iota(jnp.int32, (1, 2 * L), 1) < L
    problems = [(si, pair, blk) for si in range(ns) for pair in range(DIL_OUT // LANES)
                for blk in range(qb // L)]

    def window(si, blk, cols):
        if blk == 0:
            return jnp.concatenate([prev_ref[0, si, :, cols], own_ref[0, si, :L, cols]], axis=0)
        return own_ref[0, si, (blk - 1) * L:(blk + 1) * L, cols]

    scores = []
    for si, pair, blk in problems:
        qp = own_ref[0, si, blk * L:(blk + 1) * L, pair * LANES:(pair + 1) * LANES]
        zero = jnp.zeros_like(qp)
        keys = window(si, blk, slice(DIL_OUT + pair * LANES, DIL_OUT + (pair + 1) * LANES))
        scores.append([_dot_nt(jnp.where(low, qp, zero), keys),
                       _dot_nt(jnp.where(low, zero, qp), keys)])

    probs = []
    for (si, pair, blk), sc in zip(problems, scores):
        per_head = []
        for hh in range(2):
            s = sc[hh] + bias_ref[2 * pair + hh]
            if blk == 0:
                s = jnp.where(first_cols & (n == 0), NEG_INF, s)
            m = jnp.max(s, axis=-1, keepdims=True)
            e = jnp.exp(s - m)
            den = jnp.sum(e, axis=-1, keepdims=True)
            per_head.append((e.astype(BF16), den, m + jnp.log(den)))
        probs.append(per_head)

    for (si, pair, blk), per_head in zip(problems, probs):
        vals = window(si, blk, slice(2 * DIL_OUT + pair * LANES, 2 * DIL_OUT + (pair + 1) * LANES))
        o_pair = [_dot(e, vals) / den for e, den, _ in per_head]
        lse_pair = [jnp.broadcast_to(lse, (L, LANES)) for _, _, lse in per_head]
        dst = (0, si, slice(blk * L, (blk + 1) * L), slice(pair * LANES, (pair + 1) * LANES))
        o_ref[dst] = jnp.where(low, o_pair[0], o_pair[1])
        lse_ref[dst] = jnp.where(low, lse_pair[0], lse_pair[1])


def _dil(dg, bias):
    batch, dilation, rows, _ = dg.shape
    L = DIL_L
    qb = min(DIL_STEP_ROWS, rows)
    ns = min(DIL_STEP_ROWS // qb, dilation)
    per = qb // L
    out_sds = jax.ShapeDtypeStruct((batch, dilation, rows, DIL_OUT), F32)
    return pl.pallas_call(
        functools.partial(_dil_kernel, ns=ns, qb=qb),
        grid=(batch, dilation // ns, rows // qb),
        in_specs=[
            pl.BlockSpec((1, ns, qb, DIL_COLS), lambda b, r, n: (b, r, n, 0)),
            pl.BlockSpec((1, ns, L, DIL_COLS),
                         lambda b, r, n: (b, r, jnp.maximum(n * per - 1, 0), 0)),
            _resident(bias.shape),
        ],
        out_specs=[pl.BlockSpec((1, ns, qb, DIL_OUT), lambda b, r, n: (b, r, n, 0))] * 2,
        out_shape=[out_sds, out_sds],
        compiler_params=_cparams(("parallel", "parallel", "arbitrary")),
        name="dil_attn_d%d" % dilation,
    )(dg, dg, bias)


def _merge_kernel(x_ref, gn_ref, wgate_ref, a_ref, b_ref, o0_ref, o1_ref, o2_ref,
                  l0_ref, l1_ref, l2_ref, wa_ref, wb_ref, wc_ref, wo_ref, out_ref, *scratch):
    x = x_ref[0]
    h = _rms(x, gn_ref[...]).astype(BF16)
    tm = x.shape[0]
    scratch = list(scratch)

    def token_order(ref):
        dil = ref.shape[1]
        if dil == 1:
            return ref[0, 0]
        buf = scratch.pop()
        for c in range(buf.shape[0]):
            for r in range(dil):
                buf[c, pl.ds(r, tm // dil, stride=dil), :] = ref[0, r, :, c * LANES:(c + 1) * LANES]
        return jnp.concatenate([buf[c] for c in range(buf.shape[0])], axis=1)

    lses = [token_order(r) for r in (l0_ref, l1_ref, l2_ref)]
    outs = [token_order(r) for r in (o0_ref, o1_ref, o2_ref)]
    mx = jnp.maximum(jnp.maximum(lses[0], lses[1]), lses[2])
    es = [jnp.exp(l - mx) for l in lses]
    den = es[0] + es[1] + es[2]
    c = (es[0] / den) * outs[0] + (es[1] / den) * outs[1] + (es[2] / den) * outs[2]

    branches = ((a_ref[0], wa_ref), (b_ref[0], wb_ref), (c.astype(BF16), wc_ref))
    merged = jnp.zeros(x.shape, F32)
    for k, (act, w_ref) in enumerate(branches):
        gate = jax.nn.sigmoid(_dot(h, wgate_ref[:, k * D_MODEL:(k + 1) * D_MODEL]))
        merged = merged + gate * _dot(act, w_ref[...])
    out_ref[0] = x + _dot(merged.astype(BF16), wo_ref[...])


def _merge(x3, gain, wgate, a, b, dil_outs, wa, wb, wc, wo):
    batch, seq, _ = x3.shape
    tm = TM_MERGE
    tok = lambda c: pl.BlockSpec((1, tm, c), lambda b, i: (b, i, 0))
    stream = lambda arr: pl.BlockSpec((1, arr.shape[1], tm // arr.shape[1], DIL_OUT),
                                      lambda b, i: (b, 0, i, 0))
    (o0, l0), (o1, l1), (o2, l2) = dil_outs
    streams = (o0, o1, o2, l0, l1, l2)
    n_reordered = sum(arr.shape[1] > 1 for arr in streams)
    return pl.pallas_call(
        _merge_kernel,
        grid=(batch, seq // tm),
        in_specs=[tok(D_MODEL), _resident((1, D_MODEL)), _resident(wgate.shape),
                  tok(MLA_OUT), tok(MOBA_OUT)]
        + [stream(arr) for arr in streams]
        + [_resident(wa.shape), _resident(wb.shape), _resident(wc.shape), _resident(wo.shape)],
        out_specs=tok(D_MODEL),
        out_shape=jax.ShapeDtypeStruct((batch, seq, D_MODEL), F32),
        scratch_shapes=[pltpu.VMEM((DIL_OUT // LANES, tm, LANES), F32)] * n_reordered,
        compiler_params=_cparams(("parallel", "parallel")),
        name="merge",
    )(x3, gain, wgate, a, b, *streams, wa, wb, wc, wo)


def _alibi_slopes():
    return (2.0 ** (-8.0 * np.arange(1, N_ALIBI + 1, dtype=np.float32) / N_ALIBI)).astype(np.float32)


def _rope_tables(seq):
    pos = jnp.arange(seq, dtype=F32)
    inv = ROPE_THETA ** (-jnp.arange(0, MLA_ROPE, 2, dtype=F32) / MLA_ROPE)
    ang = pos[:, None] * inv[None, :]
    cos, sin = jnp.cos(ang), jnp.sin(ang)
    pad = jnp.zeros((seq, LANES - MLA_QK), F32)
    cos_t = jnp.concatenate([jnp.ones((seq, MLA_NOPE), F32), cos, cos, pad], axis=1)
    sin_t = jnp.concatenate([jnp.zeros((seq, MLA_NOPE), F32), -sin, sin, pad], axis=1)
    return cos_t, sin_t, cos_t.T, sin_t.T


def _dil_bias(group, dilation):
    L = DIL_L
    slopes = _alibi_slopes()[group * DIL_GROUP_HEADS:(group + 1) * DIL_GROUP_HEADS]
    steps = L + np.arange(L)[:, None] - np.arange(2 * L)[None, :]
    valid = (steps >= 0) & (steps <= L)
    dist = (steps * dilation).astype(np.float32)
    bias = np.where(valid[None], -slopes[:, None, None] * dist[None], np.float32(NEG_INF))
    return jnp.asarray(bias.astype(np.float32))


def _moba_key_aux(seq):
    slopes = _alibi_slopes()[DIL_HEADS:].astype(np.float64) * LOG2E
    pos = jnp.asarray((slopes[None, :] * np.arange(seq)[:, None]).astype(np.float32))
    aux = jnp.zeros((seq, MOBA_HEADS, LANES), F32)
    rest = pos
    for part in range(AUX_POS_PARTS):
        piece = rest.astype(BF16).astype(F32)
        aux = aux.at[:, :, AUX_POS + part].set(piece)
        rest = rest - piece
    onehot = jax.nn.one_hot(jnp.arange(seq) // MOBA_BLOCK, seq // MOBA_BLOCK, dtype=F32)
    aux = aux.at[:, :, AUX_SEL:AUX_SEL + seq // MOBA_BLOCK].set(onehot[:, None, :])
    return aux.reshape(seq, MOBA_KPAD).astype(BF16)


def _layer_weights(w_in, w_uq, w_ukv, q_norm, kv_norm):
    d = w_in.shape[0]
    z = lambda c: jnp.zeros((d, c), w_in.dtype)
    i0 = MLA_Q_RANK + MLA_KV_RANK
    kr = w_in[:, i0:i0 + MLA_ROPE]
    hr = MLA_ROPE // 2
    kr_pad = jnp.concatenate([z(MLA_NOPE), kr, z(LANES - MLA_QK)], axis=1)
    kr_swap = jnp.concatenate([z(MLA_NOPE), kr[:, hr:], kr[:, :hr], z(LANES - MLA_QK)], axis=1)
    moba = w_in[:, MLA_IN:MLA_IN + MOBA_IN]
    bq, bk, bv = (moba[:, c * MOBA_OUT:(c + 1) * MOBA_OUT] for c in range(3))
    bk = jnp.concatenate([bk.reshape(d, MOBA_HEADS, HEAD_DIM),
                          jnp.zeros((d, MOBA_HEADS, LANES - HEAD_DIM), w_in.dtype)],
                         axis=2).reshape(d, MOBA_KPAD)
    dil = w_in[:, MLA_IN + MOBA_IN:MLA_IN + MOBA_IN + DIL_IN].reshape(d, 3, DIL_HEADS, HEAD_DIM)
    groups = [dil[:, :, g * DIL_GROUP_HEADS:(g + 1) * DIL_GROUP_HEADS].reshape(d, DIL_COLS)
              for g in range(len(DIL_PATTERNS))]
    w1 = jnp.concatenate([w_in[:, :i0], kr_pad, kr_swap, bk] + groups, axis=1).astype(BF16)
    wt = jnp.concatenate([bq, bv], axis=1).T.astype(BF16)
    wgate = w_in[:, MLA_IN + MOBA_IN + DIL_IN:].astype(BF16)

    r = w_uq.shape[0]
    uq = w_uq.reshape(r, MLA_HEADS, MLA_QK)
    zq = jnp.zeros((r, MLA_HEADS, LANES - MLA_QK), w_uq.dtype)
    zn = jnp.zeros((r, MLA_HEADS, MLA_NOPE), w_uq.dtype)
    uq_pad = jnp.concatenate([uq, zq], axis=2).reshape(r, MLA_QPAD)
    uq_swap = jnp.concatenate(
        [zn, uq[:, :, MLA_NOPE + hr:], uq[:, :, MLA_NOPE:MLA_NOPE + hr], zq], axis=2
    ).reshape(r, MLA_QPAD)
    wuqt = jnp.concatenate([uq_pad, uq_swap], axis=1).T.astype(BF16)

    rk = w_ukv.shape[0]
    ukv = w_ukv.reshape(rk, MLA_HEADS, MLA_NOPE + MLA_V)
    zk = jnp.zeros((rk, MLA_HEADS, LANES - MLA_NOPE), w_ukv.dtype)
    wk = jnp.concatenate([ukv[:, :, :MLA_NOPE], zk], axis=2).reshape(rk, MLA_QPAD).astype(BF16)
    wvt = ukv[:, :, MLA_NOPE:].reshape(rk, MLA_OUT).T.astype(BF16)
    return dict(w1=w1, wt=wt, wgate=wgate, wuqt=wuqt, wk=wk, wvt=wvt,
                qn=q_norm.reshape(1, -1), kvn=kv_norm.reshape(1, -1))


def kernel(x, ffn1_norm, ffn1_w_gate, ffn1_w_up, ffn1_w_down, mix_norm, w_in, q_norm, w_uq, kv_norm, w_ukv, w_br_mla, w_br_moba, w_br_dil, w_out, ffn2_norm, ffn2_w_gate, ffn2_w_up, ffn2_w_down, final_norm):
    batch, seq, d = x.shape
    assert d == D_MODEL and seq % (DIL_PATTERNS[-1][1] * DIL_L) == 0 and seq % TM_PROJ == 0
    n = batch * seq
    x2 = x.reshape(n, d)
    tables = _rope_tables(seq) + (_moba_key_aux(seq),)
    dil_bias = [_dil_bias(g, dil) for g, (_, dil) in enumerate(DIL_PATTERNS)]
    final_gain = final_norm.reshape(1, d)
    bf = lambda w: w.astype(BF16)
    row = lambda v: v.reshape(1, -1)

    for l in range(DEPTH):
        x2 = _ffn(x2, row(ffn1_norm[l]), bf(ffn1_w_gate[l]), bf(ffn1_w_up[l]), bf(ffn1_w_down[l]),
                  final_gain, False)
        lw = _layer_weights(w_in[l], w_uq[l], w_ukv[l], q_norm[l], kv_norm[l])
        mqt, mk, mvt, bqt, bk, bvt, d0, d1, d2 = _inproj(
            x2.reshape(batch, seq, d), row(mix_norm[l]), lw, tables)
        a = _mla(mqt, mk, mvt)
        b = _moba(bqt, bk, bvt)
        dil_outs = [_dil(dg, dil_bias[g]) for g, dg in enumerate((d0, d1, d2))]
        x2 = _merge(x2.reshape(batch, seq, d), row(mix_norm[l]), lw["wgate"], a, b, dil_outs,
                    bf(w_br_mla[l]), bf(w_br_moba[l]), bf(w_br_dil[l]), bf(w_out[l])).reshape(n, d)
        x2 = _ffn(x2, row(ffn2_norm[l]), bf(ffn2_w_gate[l]), bf(ffn2_w_up[l]), bf(ffn2_w_down[l]),
                  final_gain, l == DEPTH - 1)
    return x2.reshape(batch, seq, d)
```

```python
import functools

import numpy as np
import jax
import jax.numpy as jnp
from jax import lax
from jax.experimental import pallas as pl
from jax.experimental.pallas import tpu as pltpu

F32 = jnp.float32
BF16 = jnp.bfloat16

D_MODEL = 1024
DEPTH = 4
HEAD_DIM = 64
MLA_HEADS = 6
MLA_NOPE = 64
MLA_ROPE = 32
MLA_V = 64
MLA_Q_RANK = 256
MLA_KV_RANK = 128
ROPE_THETA = 10000.0
MOBA_HEADS = 6
MOBA_BLOCK = 256
MOBA_TOPK = 3
DIL_PATTERNS = ((128, 1), (512, 4), (2048, 16))
DIL_GROUP_HEADS = 4
DIL_HEADS = DIL_GROUP_HEADS * len(DIL_PATTERNS)
N_ALIBI = DIL_HEADS + MOBA_HEADS
N_BRANCHES = 3
D_FF = 2816
NORM_EPS = 1e-6
NEG_INF = -1e30
MLA_IN = MLA_Q_RANK + MLA_KV_RANK + MLA_ROPE
MOBA_IN = 3 * MOBA_HEADS * HEAD_DIM
DIL_IN = 3 * DIL_HEADS * HEAD_DIM
MLA_OUT = MLA_HEADS * MLA_V
MOBA_OUT = MOBA_HEADS * HEAD_DIM
DIL_OUT = DIL_GROUP_HEADS * HEAD_DIM

LANES = 128
MLA_QK = MLA_NOPE + MLA_ROPE
MLA_QPAD = MLA_HEADS * LANES
LOG2E = 1.4426950408889634
MLA_SCALE = MLA_QK ** -0.5
HEAD_SCALE = HEAD_DIM ** -0.5
DIL_L = DIL_PATTERNS[0][0] // DIL_PATTERNS[0][1]
DIL_COLS = 3 * DIL_GROUP_HEADS * HEAD_DIM

C_CQ = 0
C_CKV = C_CQ + MLA_Q_RANK
C_KR = C_CKV + MLA_KV_RANK
C_KR2 = C_KR + LANES
C_MOBA_K = C_KR2 + LANES
MOBA_KPAD = MOBA_HEADS * LANES
C_DIL = C_MOBA_K + MOBA_KPAD
C_END = C_DIL + DIL_IN

AUX_POS = HEAD_DIM
AUX_POS_PARTS = 3
AUX_SEL = HEAD_DIM + 16

VMEM_LIMIT = 56 * 1024 * 1024

TM_FFN = 512
TM_PROJ = 512
TM_MERGE = 512
FF_CHUNK = 256
MLA_T = 256
MLA_NH = 6
MOBA_NH = 6
DIL_STEP_ROWS = 512


def _cparams(sem):
    return pltpu.CompilerParams(dimension_semantics=sem, vmem_limit_bytes=VMEM_LIMIT)


def _resident(shape):
    nd = len(shape)
    return pl.BlockSpec(shape, lambda *_: (0,) * nd, pipeline_mode=pl.Buffered(1))


def _rms(x, gain):
    ms = jnp.mean(x * x, axis=-1, keepdims=True)
    return x * lax.rsqrt(ms + NORM_EPS) * gain


def _dot(a, b):
    return jnp.dot(a, b, preferred_element_type=F32)


def _dot_nt(a, b):
    return lax.dot_general(a, b, (((1,), (1,)), ((), ())), preferred_element_type=F32)


def _ffn_kernel(x_ref, g_ref, wg_ref, wu_ref, wd_ref, fg_ref, o_ref, *, final):
    x = x_ref[...]
    h = _rms(x, g_ref[...]).astype(BF16)
    acc = jnp.zeros(x.shape, F32)
    for c in range(D_FF // FF_CHUNK):
        sl = slice(c * FF_CHUNK, (c + 1) * FF_CHUNK)
        a = _dot(h, wg_ref[:, sl])
        u = _dot(h, wu_ref[:, sl])
        g = (a * jax.nn.sigmoid(a) * u).astype(BF16)
        acc = acc + _dot(g, wd_ref[sl, :])
    y = x + 0.5 * acc
    if final:
        y = _rms(y, fg_ref[...])
    o_ref[...] = y


def _ffn(x2, gain, wg, wu, wd, final_gain, final):
    n = x2.shape[0]
    return pl.pallas_call(
        functools.partial(_ffn_kernel, final=final),
        grid=(n // TM_FFN,),
        in_specs=[
            pl.BlockSpec((TM_FFN, D_MODEL), lambda i: (i, 0)),
            _resident((1, D_MODEL)),
            _resident((D_MODEL, D_FF)),
            _resident((D_MODEL, D_FF)),
            _resident((D_FF, D_MODEL)),
            _resident((1, D_MODEL)),
        ],
        out_specs=pl.BlockSpec((TM_FFN, D_MODEL), lambda i: (i, 0)),
        out_shape=jax.ShapeDtypeStruct((n, D_MODEL), F32),
        compiler_params=_cparams(("parallel",)),
        name="ffn_final" if final else "ffn",
    )(x2, gain, wg, wu, wd, final_gain)


def _inproj_kernel(x_ref, gn_ref, w1_ref, wt_ref, qn_ref, wuqt_ref, kvn_ref, wk_ref, wvt_ref,
                   cos_ref, sin_ref, cost_ref, sint_ref, kaux_ref,
                   mqt_ref, mk_ref, mvt_ref, bqt_ref, bk_ref, bvt_ref, d0_ref, d1_ref, d2_ref,
                   stage_ref):
    x = x_ref[0]
    h = _rms(x, gn_ref[...]).astype(BF16)

    pm = _dot(h, w1_ref[:, C_CQ:C_MOBA_K])
    cq = _rms(pm[:, C_CQ:C_CKV], qn_ref[...]).astype(BF16)
    ckv = _rms(pm[:, C_CKV:C_KR], kvn_ref[...]).astype(BF16)
    k_rope = pm[:, C_KR:C_KR2] * cos_ref[...] + pm[:, C_KR2:C_MOBA_K] * sin_ref[...]
    kk = _dot(ckv, wk_ref[...])
    qqt = _dot_nt(wuqt_ref[...], cq)
    cost = cost_ref[...]
    sint = sint_ref[...]
    for hh in range(MLA_HEADS):
        sl = slice(hh * LANES, (hh + 1) * LANES)
        sl2 = slice(MLA_QPAD + hh * LANES, MLA_QPAD + (hh + 1) * LANES)
        mqt_ref[0, sl, :] = ((qqt[sl] * cost + qqt[sl2] * sint) * (MLA_SCALE * LOG2E)).astype(BF16)
        mk_ref[0, :, sl] = (kk[:, sl] + k_rope).astype(BF16)
    mvt_ref[0] = _dot_nt(wvt_ref[...], ckv).astype(BF16)

    bt = _dot_nt(wt_ref[...], h)
    bqt_ref[0] = (bt[:MOBA_OUT] * (HEAD_SCALE * LOG2E)).astype(BF16)
    bvt_ref[0] = bt[MOBA_OUT:].astype(BF16)
    bk_ref[0] = (_dot(h, w1_ref[:, C_MOBA_K:C_DIL]) + kaux_ref[...].astype(F32)).astype(BF16)

    tm = x.shape[0]
    for g, d_ref in enumerate((d0_ref, d1_ref, d2_ref)):
        dil = DIL_PATTERNS[g][1]
        dd = _dot(h, w1_ref[:, C_DIL + g * DIL_COLS:C_DIL + (g + 1) * DIL_COLS])
        for c in range(DIL_COLS // LANES):
            cols = slice(c * LANES, (c + 1) * LANES)
            blk = dd[:, cols]
            stage_ref[c] = blk * HEAD_SCALE if c < DIL_OUT // LANES else blk
            for r in range(dil):
                d_ref[0, r, :, cols] = stage_ref[c, pl.ds(r, tm // dil, stride=dil), :].astype(BF16)


def _inproj(x3, gain, lw, tables):
    batch, seq, _ = x3.shape
    tm = TM_PROJ
    cos_t, sin_t, cos_tt, sin_tt, kaux = tables
    tok = lambda c: pl.BlockSpec((1, tm, c), lambda b, i: (b, i, 0))
    feat = lambda c: pl.BlockSpec((1, c, tm), lambda b, i: (b, 0, i))
    tok_sds = lambda c: jax.ShapeDtypeStruct((batch, seq, c), BF16)
    feat_sds = lambda c: jax.ShapeDtypeStruct((batch, c, seq), BF16)
    dils = [dil for _, dil in DIL_PATTERNS]
    stream = lambda dil: pl.BlockSpec((1, dil, tm // dil, DIL_COLS), lambda b, i: (b, 0, i, 0))
    stream_sds = lambda dil: jax.ShapeDtypeStruct((batch, dil, seq // dil, DIL_COLS), BF16)
    return pl.pallas_call(
        _inproj_kernel,
        grid=(batch, seq // tm),
        in_specs=[
            tok(D_MODEL),
            _resident((1, D_MODEL)),
            _resident(lw["w1"].shape),
            _resident(lw["wt"].shape),
            _resident((1, MLA_Q_RANK)),
            _resident(lw["wuqt"].shape),
            _resident((1, MLA_KV_RANK)),
            _resident(lw["wk"].shape),
            _resident(lw["wvt"].shape),
            pl.BlockSpec((tm, LANES), lambda b, i: (i, 0)),
            pl.BlockSpec((tm, LANES), lambda b, i: (i, 0)),
            pl.BlockSpec((LANES, tm), lambda b, i: (0, i)),
            pl.BlockSpec((LANES, tm), lambda b, i: (0, i)),
            pl.BlockSpec((tm, MOBA_KPAD), lambda b, i: (i, 0)),
        ],
        out_specs=[feat(MLA_QPAD), tok(MLA_QPAD), feat(MLA_OUT),
                   feat(MOBA_OUT), tok(MOBA_KPAD), feat(MOBA_OUT)] + [stream(dil) for dil in dils],
        out_shape=[feat_sds(MLA_QPAD), tok_sds(MLA_QPAD), feat_sds(MLA_OUT),
                   feat_sds(MOBA_OUT), tok_sds(MOBA_KPAD), feat_sds(MOBA_OUT)]
        + [stream_sds(dil) for dil in dils],
        scratch_shapes=[pltpu.VMEM((DIL_COLS // LANES, tm, LANES), F32)],
        compiler_params=_cparams(("parallel", "parallel")),
        name="inproj",
    )(x3, gain, lw["w1"], lw["wt"], lw["qn"], lw["wuqt"], lw["kvn"], lw["wk"], lw["wvt"],
      cos_t, sin_t, cos_tt, sin_tt, kaux)


ROW_M, ROW_ALPHA, ROW_L, ROW_SMAX = range(4)


def _flash_heads(nh, i, t, dv, score_fn, value_fn, diag_fn, s_buf, p_buf, acc_buf, stat_buf):
    heads = range(nh)
    col_max = lambda s: jnp.max(s, axis=0, keepdims=True)
    stat = lambda h, r: stat_buf[h, r:r + 1, :]

    def set_stat(h, r, v):
        stat_buf[h, r:r + 1, :] = v

    def values(h, j):
        pv = _dot(value_fn(jnp.maximum(j, 0), h), p_buf[h])
        acc_buf[h] = stat(h, ROW_ALPHA) * acc_buf[h] + pv

    def softmax(h, s, s_max):
        m = stat(h, ROW_M)
        m_new = jnp.maximum(m, s_max)
        alpha = jnp.exp2(m - m_new)
        p = jnp.exp2(s - m_new)
        p_buf[h] = p.astype(BF16)
        set_stat(h, ROW_M, m_new)
        set_stat(h, ROW_ALPHA, alpha)
        set_stat(h, ROW_L, alpha * stat(h, ROW_L) + jnp.sum(p, axis=0, keepdims=True))

    @pl.when(i == 0)
    def _():
        p_buf[...] = jnp.zeros(p_buf.shape, BF16)

    for h in heads:
        sc = score_fn(0, h)
        s_buf[h] = sc
        set_stat(h, ROW_SMAX, col_max(sc))
        set_stat(h, ROW_M, jnp.full((1, t), NEG_INF, F32))
        set_stat(h, ROW_ALPHA, jnp.ones((1, t), F32))
        set_stat(h, ROW_L, jnp.zeros((1, t), F32))
        acc_buf[h] = jnp.zeros((dv, t), F32)

    @pl.loop(0, i)
    def _(j):
        for h in heads:
            sc = score_fn(j + 1, h)
            values(h, j - 1)
            softmax(h, s_buf[h], stat(h, ROW_SMAX))
            s_buf[h] = sc
            set_stat(h, ROW_SMAX, col_max(sc))

    for h in heads:
        values(h, i - 1)
    for h in heads:
        s = diag_fn(h, s_buf[h])
        softmax(h, s, col_max(s))
    for h in heads:
        values(h, i)
    return [acc_buf[h] / stat(h, ROW_L) for h in heads]


def _store_token_major(o_ref, outs):
    for p in range(len(outs) // 2):
        pair = jnp.concatenate([outs[2 * p], outs[2 * p + 1]], axis=0)
        o_ref[0, :, p * LANES:(p + 1) * LANES] = pair.T.astype(o_ref.dtype)


def _flash_scratch(nh, t, dv):
    return [pltpu.VMEM((nh, t, t), F32), pltpu.VMEM((nh, t, t), BF16),
            pltpu.VMEM((nh, dv, t), F32), pltpu.VMEM((nh, 8, t), F32)]


def _causal_mask_t(t):
    key = lax.broadcasted_iota(jnp.int32, (t, t), 0)
    qry = lax.broadcasted_iota(jnp.int32, (t, t), 1)
    return key <= qry


def _mla_kernel(qt_ref, k_ref, vt_ref, o_ref, *flash_scratch):
    i = pl.program_id(2)
    t = MLA_T
    nh = MLA_NH
    qs = [qt_ref[0, h * LANES:h * LANES + MLA_QK, :] for h in range(nh)]

    def scores(j, h):
        start = pl.multiple_of(j * t, t)
        return _dot(k_ref[0, pl.ds(start, t), h * LANES:h * LANES + MLA_QK], qs[h])

    def values(j, h):
        return vt_ref[0, h * MLA_V:(h + 1) * MLA_V, pl.ds(pl.multiple_of(j * t, t), t)]

    causal = _causal_mask_t(t)
    outs = _flash_heads(nh, i, t, MLA_V, scores, values,
                        lambda h, s: jnp.where(causal, s, NEG_INF),
                        *flash_scratch)
    _store_token_major(o_ref, outs)


def _mla(mqt, mk, mvt):
    batch, seq, _ = mk.shape
    t = MLA_T
    nh = MLA_NH
    return pl.pallas_call(
        _mla_kernel,
        grid=(batch, MLA_HEADS // nh, seq // t),
        in_specs=[pl.BlockSpec((1, nh * LANES, t), lambda b, g, i: (b, g, i)),
                  pl.BlockSpec((1, seq, nh * LANES), lambda b, g, i: (b, 0, g)),
                  pl.BlockSpec((1, nh * MLA_V, seq), lambda b, g, i: (b, g, 0))],
        out_specs=pl.BlockSpec((1, t, nh * MLA_V), lambda b, g, i: (b, i, g)),
        out_shape=jax.ShapeDtypeStruct((batch, seq, MLA_OUT), BF16),
        scratch_shapes=_flash_scratch(nh, t, MLA_V),
        compiler_params=_cparams(("parallel", "parallel", "arbitrary")),
        name="mla_attn",
    )(mqt, mk, mvt)


def _moba_kernel(qt_ref, k_ref, vt_ref, o_ref, kmh_ref, kml_ref, *flash_scratch, nblk):
    i = pl.program_id(2)
    t = MOBA_BLOCK
    nh = MOBA_NH

    @pl.when(i == 0)
    def _():
        kf = k_ref[0].astype(F32).reshape(nblk, t, nh * LANES)
        km = jnp.sum(kf, axis=1) * (1.0 / t)
        hi = km.astype(BF16)
        kmh_ref[...] = hi
        kml_ref[...] = (km - hi.astype(F32)).astype(BF16)

    blk = lax.broadcasted_iota(jnp.int32, (nblk, 1), 0)
    blk_f = blk.astype(F32)
    pos_rows =lax.broadcasted_iota(jnp.int32, (AUX_SEL - AUX_POS, 1), 0) < AUX_POS_PARTS
    ones_rows = jnp.broadcast_to(jnp.where(pos_rows, 1.0, 0.0), (AUX_SEL - AUX_POS, t)).astype(BF16)
    pad_q = jnp.zeros((LANES - HEAD_DIM, t), BF16)
    qs = []
    for h in range(nh):
        q = qt_ref[0, h * HEAD_DIM:(h + 1) * HEAD_DIM, :]
        q_plain = jnp.concatenate([q, pad_q], axis=0)
        cols = slice(h * LANES, (h + 1) * LANES)
        gate = _dot(kmh_ref[:, cols], q_plain) + _dot(kml_ref[:, cols], q_plain)
        gate = jnp.where(blk < i, gate, NEG_INF)
        chosen = blk == i
        for _ in range(min(MOBA_TOPK, nblk)):
            best = jnp.max(gate, axis=0, keepdims=True)
            first = jnp.min(jnp.where(gate == best, blk_f, float(nblk)), axis=0, keepdims=True)
            pick = blk_f == first
            chosen = chosen | (pick & (blk < i))
            gate = jnp.where(pick, -jnp.inf, gate)
        sel = jnp.where(chosen, 0.0, NEG_INF).astype(BF16)
        qs.append(jnp.concatenate([q, ones_rows, sel], axis=0))

    def scores(j, h):
        start = pl.multiple_of(j * t, t)
        return _dot(k_ref[0, pl.ds(start, t), h * LANES:h * LANES + AUX_SEL + nblk], qs[h])

    def values(j, h):
        return vt_ref[0, h * HEAD_DIM:(h + 1) * HEAD_DIM, pl.ds(pl.multiple_of(j * t, t), t)]

    causal = _causal_mask_t(t)
    outs = _flash_heads(nh, i, t, HEAD_DIM, scores, values,
                        lambda h, s: jnp.where(causal, s, NEG_INF),
                        *flash_scratch)
    _store_token_major(o_ref, outs)


def _moba(bqt, bk, bvt):
    batch, seq, _ = bk.shape
    t = MOBA_BLOCK
    nh = MOBA_NH
    nblk = seq // t
    assert AUX_SEL + nblk <= LANES
    cols = nh * HEAD_DIM
    return pl.pallas_call(
        functools.partial(_moba_kernel, nblk=nblk),
        grid=(batch, MOBA_HEADS // nh, nblk),
        in_specs=[
            pl.BlockSpec((1, cols, t), lambda b, g, i: (b, g, i)),
            pl.BlockSpec((1, seq, nh * LANES), lambda b, g, i: (b, 0, g)),
            pl.BlockSpec((1, cols, seq), lambda b, g, i: (b, g, 0)),
        ],
        out_specs=pl.BlockSpec((1, t, cols), lambda b, g, i: (b, i, g)),
        out_shape=jax.ShapeDtypeStruct((batch, seq, MOBA_OUT), BF16),
        scratch_shapes=[
            pltpu.VMEM((nblk, nh * LANES), BF16),
            pltpu.VMEM((nblk, nh * LANES), BF16),
        ] + _flash_scratch(nh, t, HEAD_DIM),
        compiler_params=_cparams(("parallel", "parallel", "arbitrary")),
        name="moba_attn",
    )(bqt, bk, bvt)


def _dil_kernel(own_ref, prev_ref, bias_ref, o_ref, lse_ref, *, ns, qb):
    n = pl.program_id(2)
    L = DIL_L
    low = lax.broadcasted_iota(jnp.int32, (1, LANES), 1) < HEAD_DIM
    first_cols = lax.broadcasted_iota(jnp.int32, (1, 2 * L), 1) < L
    problems = [(si, pair, blk) for si in range(ns) for pair in range(DIL_OUT // LANES)
                for blk in range(qb // L)]

    def window(si, blk, cols):
        if blk == 0:
            return jnp.concatenate([prev_ref[0, si, :, cols], own_ref[0, si, :L, cols]], axis=0)
        return own_ref[0, si, (blk - 1) * L:(blk + 1) * L, cols]

    scores = []
    for si, pair, blk in problems:
        qp = own_ref[0, si, blk * L:(blk + 1) * L, pair * LANES:(pair + 1) * LANES]
        zero = jnp.zeros_like(qp)
        keys = window(si, blk, slice(DIL_OUT + pair * LANES, DIL_OUT + (pair + 1) * LANES))
        scores.append([_dot_nt(jnp.where(low, qp, zero), keys),
                       _dot_nt(jnp.where(low, zero, qp), keys)])

    probs = []
    for (si, pair, blk), sc in zip(problems, scores):
        per_head = []
        for hh in range(2):
            s = sc[hh] + bias_ref[2 * pair + hh]
            if blk == 0:
                s = jnp.where(first_cols & (n == 0), NEG_INF, s)
            m = jnp.max(s, axis=-1, keepdims=True)
            e = jnp.exp(s - m)
            den = jnp.sum(e, axis=-1, keepdims=True)
            per_head.append((e.astype(BF16), den, m + jnp.log(den)))
        probs.append(per_head)

    for (si, pair, blk), per_head in zip(problems, probs):
        vals = window(si, blk, slice(2 * DIL_OUT + pair * LANES, 2 * DIL_OUT + (pair + 1) * LANES))
        o_pair = [_dot(e, vals) / den for e, den, _ in per_head]
        lse_pair = [jnp.broadcast_to(lse, (L, LANES)) for _, _, lse in per_head]
        dst = (0, si, slice(blk * L, (blk + 1) * L), slice(pair * LANES, (pair + 1) * LANES))
        o_ref[dst] = jnp.where(low, o_pair[0], o_pair[1])
        lse_ref[dst] = jnp.where(low, lse_pair[0], lse_pair[1])


def _dil(dg, bias):
    batch, dilation, rows, _ = dg.shape
    L = DIL_L
    qb = min(DIL_STEP_ROWS, rows)
    ns = min(DIL_STEP_ROWS // qb, dilation)
    per = qb // L
    out_sds = jax.ShapeDtypeStruct((batch, dilation, rows, DIL_OUT), F32)
    return pl.pallas_call(
        functools.partial(_dil_kernel, ns=ns, qb=qb),
        grid=(batch, dilation // ns, rows // qb),
        in_specs=[
            pl.BlockSpec((1, ns, qb, DIL_COLS), lambda b, r, n: (b, r, n, 0)),
            pl.BlockSpec((1, ns, L, DIL_COLS),
                         lambda b, r, n: (b, r, jnp.maximum(n * per - 1, 0), 0)),
            _resident(bias.shape),
        ],
        out_specs=[pl.BlockSpec((1, ns, qb, DIL_OUT), lambda b, r, n: (b, r, n, 0))] * 2,
        out_shape=[out_sds, out_sds],
        compiler_params=_cparams(("parallel", "parallel", "arbitrary")),
        name="dil_attn_d%d" % dilation,
    )(dg, dg, bias)


def _merge_kernel(x_ref, gn_ref, wgate_ref, a_ref, b_ref, o0_ref, o1_ref, o2_ref,
                  l0_ref, l1_ref, l2_ref, wa_ref, wb_ref, wc_ref, wo_ref, out_ref, *scratch):
    x = x_ref[0]
    h = _rms(x, gn_ref[...]).astype(BF16)
    tm = x.shape[0]
    scratch = list(scratch)

    def token_order(ref):
        dil = ref.shape[1]
        if dil == 1:
            return ref[0, 0]
        buf = scratch.pop()
        for c in range(buf.shape[0]):
            for r in range(dil):
                buf[c, pl.ds(r, tm // dil, stride=dil), :] = ref[0, r, :, c * LANES:(c + 1) * LANES]
        return jnp.concatenate([buf[c] for c in range(buf.shape[0])], axis=1)

    lses = [token_order(r) for r in (l0_ref, l1_ref, l2_ref)]
    outs = [token_order(r) for r in (o0_ref, o1_ref, o2_ref)]
    mx = jnp.maximum(jnp.maximum(lses[0], lses[1]), lses[2])
    es = [jnp.exp(l - mx) for l in lses]
    den = es[0] + es[1] + es[2]
    c = (es[0] / den) * outs[0] + (es[1] / den) * outs[1] + (es[2] / den) * outs[2]

    branches = ((a_ref[0], wa_ref), (b_ref[0], wb_ref), (c.astype(BF16), wc_ref))
    merged = jnp.zeros(x.shape, F32)
    for k, (act, w_ref) in enumerate(branches):
        gate = jax.nn.sigmoid(_dot(h, wgate_ref[:, k * D_MODEL:(k + 1) * D_MODEL]))
        merged = merged + gate * _dot(act, w_ref[...])
    out_ref[0] = x + _dot(merged.astype(BF16), wo_ref[...])


def _merge(x3, gain, wgate, a, b, dil_outs, wa, wb, wc, wo):
    batch, seq, _ = x3.shape
    tm = TM_MERGE
    tok = lambda c: pl.BlockSpec((1, tm, c), lambda b, i: (b, i, 0))
    stream = lambda arr: pl.BlockSpec((1, arr.shape[1], tm // arr.shape[1], DIL_OUT),
                                      lambda b, i: (b, 0, i, 0))
    (o0, l0), (o1, l1), (o2, l2) = dil_outs
    streams = (o0, o1, o2, l0, l1, l2)
    n_reordered = sum(arr.shape[1] > 1 for arr in streams)
    return pl.pallas_call(
        _merge_kernel,
        grid=(batch, seq // tm),
        in_specs=[tok(D_MODEL), _resident((1, D_MODEL)), _resident(wgate.shape),
                  tok(MLA_OUT), tok(MOBA_OUT)]
        + [stream(arr) for arr in streams]
        + [_resident(wa.shape), _resident(wb.shape), _resident(wc.shape), _resident(wo.shape)],
        out_specs=tok(D_MODEL),
        out_shape=jax.ShapeDtypeStruct((batch, seq, D_MODEL), F32),
        scratch_shapes=[pltpu.VMEM((DIL_OUT // LANES, tm, LANES), F32)] * n_reordered,
        compiler_params=_cparams(("parallel", "parallel")),
        name="merge",
    )(x3, gain, wgate, a, b, *streams, wa, wb, wc, wo)


def _alibi_slopes():
    return (2.0 ** (-8.0 * np.arange(1, N_ALIBI + 1, dtype=np.float32) / N_ALIBI)).astype(np.float32)


def _rope_tables(seq):
    pos = jnp.arange(seq, dtype=F32)
    inv = ROPE_THETA ** (-jnp.arange(0, MLA_ROPE, 2, dtype=F32) / MLA_ROPE)
    ang = pos[:, None] * inv[None, :]
    cos, sin = jnp.cos(ang), jnp.sin(ang)
    pad = jnp.zeros((seq, LANES - MLA_QK), F32)
    cos_t = jnp.concatenate([jnp.ones((seq, MLA_NOPE), F32), cos, cos, pad], axis=1)
    sin_t = jnp.concatenate([jnp.zeros((seq, MLA_NOPE), F32), -sin, sin, pad], axis=1)
    return cos_t, sin_t, cos_t.T, sin_t.T


def _dil_bias(group, dilation):
    L = DIL_L
    slopes = _alibi_slopes()[group * DIL_GROUP_HEADS:(group + 1) * DIL_GROUP_HEADS]
    steps = L + np.arange(L)[:, None] - np.arange(2 * L)[None, :]
    valid = (steps >= 0) & (steps <= L)
    dist = (steps * dilation).astype(np.float32)
    bias = np.where(valid[None], -slopes[:, None, None] * dist[None], np.float32(NEG_INF))
    return jnp.asarray(bias.astype(np.float32))


def _moba_key_aux(seq):
    slopes = _alibi_slopes()[DIL_HEADS:].astype(np.float64) * LOG2E
    pos = jnp.asarray((slopes[None, :] * np.arange(seq)[:, None]).astype(np.float32))
    aux = jnp.zeros((seq, MOBA_HEADS, LANES), F32)
    rest = pos
    for part in range(AUX_POS_PARTS):
        piece = rest.astype(BF16).astype(F32)
        aux = aux.at[:, :, AUX_POS + part].set(piece)
        rest = rest - piece
    onehot = jax.nn.one_hot(jnp.arange(seq) // MOBA_BLOCK, seq // MOBA_BLOCK, dtype=F32)
    aux = aux.at[:, :, AUX_SEL:AUX_SEL + seq // MOBA_BLOCK].set(onehot[:, None, :])
    return aux.reshape(seq, MOBA_KPAD).astype(BF16)


def _layer_weights(w_in, w_uq, w_ukv, q_norm, kv_norm):
    d = w_in.shape[0]
    z = lambda c: jnp.zeros((d, c), w_in.dtype)
    i0 = MLA_Q_RANK + MLA_KV_RANK
    kr = w_in[:, i0:i0 + MLA_ROPE]
    hr = MLA_ROPE // 2
    kr_pad = jnp.concatenate([z(MLA_NOPE), kr, z(LANES - MLA_QK)], axis=1)
    kr_swap = jnp.concatenate([z(MLA_NOPE), kr[:, hr:], kr[:, :hr], z(LANES - MLA_QK)], axis=1)
    moba = w_in[:, MLA_IN:MLA_IN + MOBA_IN]
    bq, bk, bv = (moba[:, c * MOBA_OUT:(c + 1) * MOBA_OUT] for c in range(3))
    bk = jnp.concatenate([bk.reshape(d, MOBA_HEADS, HEAD_DIM),
                          jnp.zeros((d, MOBA_HEADS, LANES - HEAD_DIM), w_in.dtype)],
                         axis=2).reshape(d, MOBA_KPAD)
    dil = w_in[:, MLA_IN + MOBA_IN:MLA_IN + MOBA_IN + DIL_IN].reshape(d, 3, DIL_HEADS, HEAD_DIM)
    groups = [dil[:, :, g * DIL_GROUP_HEADS:(g + 1) * DIL_GROUP_HEADS].reshape(d, DIL_COLS)
              for g in range(len(DIL_PATTERNS))]
    w1 = jnp.concatenate([w_in[:, :i0], kr_pad, kr_swap, bk] + groups, axis=1).astype(BF16)
    wt = jnp.concatenate([bq, bv], axis=1).T.astype(BF16)
    wgate = w_in[:, MLA_IN + MOBA_IN + DIL_IN:].astype(BF16)

    r = w_uq.shape[0]
    uq = w_uq.reshape(r, MLA_HEADS, MLA_QK)
    zq = jnp.zeros((r, MLA_HEADS, LANES - MLA_QK), w_uq.dtype)
    zn = jnp.zeros((r, MLA_HEADS, MLA_NOPE), w_uq.dtype)
    uq_pad = jnp.concatenate([uq, zq], axis=2).reshape(r, MLA_QPAD)
    uq_swap = jnp.concatenate(
        [zn, uq[:, :, MLA_NOPE + hr:], uq[:, :, MLA_NOPE:MLA_NOPE + hr], zq], axis=2
    ).reshape(r, MLA_QPAD)
    wuqt = jnp.concatenate([uq_pad, uq_swap], axis=1).T.astype(BF16)

    rk = w_ukv.shape[0]
    ukv = w_ukv.reshape(rk, MLA_HEADS, MLA_NOPE + MLA_V)
    zk = jnp.zeros((rk, MLA_HEADS, LANES - MLA_NOPE), w_ukv.dtype)
    wk = jnp.concatenate([ukv[:, :, :MLA_NOPE], zk], axis=2).reshape(rk, MLA_QPAD).astype(BF16)
    wvt = ukv[:, :, MLA_NOPE:].reshape(rk, MLA_OUT).T.astype(BF16)
    return dict(w1=w1, wt=wt, wgate=wgate, wuqt=wuqt, wk=wk, wvt=wvt,
                qn=q_norm.reshape(1, -1), kvn=kv_norm.reshape(1, -1))


def kernel(x, ffn1_norm, ffn1_w_gate, ffn1_w_up, ffn1_w_down, mix_norm, w_in, q_norm, w_uq, kv_norm, w_ukv, w_br_mla, w_br_moba, w_br_dil, w_out, ffn2_norm, ffn2_w_gate, ffn2_w_up, ffn2_w_down, final_norm):
    batch, seq, d = x.shape
    assert d == D_MODEL and seq % (DIL_PATTERNS[-1][1] * DIL_L) == 0 and seq % TM_PROJ == 0
    n = batch * seq
    x2 = x.reshape(n, d)
    tables = _rope_tables(seq) + (_moba_key_aux(seq),)
    dil_bias = [_dil_bias(g, dil) for g, (_, dil) in enumerate(DIL_PATTERNS)]
    final_gain = final_norm.reshape(1, d)
    bf = lambda w: w.astype(BF16)
    row = lambda v: v.reshape(1, -1)

    for l in range(DEPTH):
        x2 = _ffn(x2, row(ffn1_norm[l]), bf(ffn1_w_gate[l]), bf(ffn1_w_up[l]), bf(ffn1_w_down[l]),
                  final_gain, False)
        lw = _layer_weights(w_in[l], w_uq[l], w_ukv[l], q_norm[l], kv_norm[l])
        mqt, mk, mvt, bqt, bk, bvt, d0, d1, d2 = _inproj(
            x2.reshape(batch, seq, d), row(mix_norm[l]), lw, tables)
        a = _mla(mqt, mk, mvt)
        b = _moba(bqt, bk, bvt)
        dil_outs = [_dil(dg, dil_bias[g]) for g, dg in enumerate((d0, d1, d2))]
        x2 = _merge(x2.reshape(batch, seq, d), row(mix_norm[l]), lw["wgate"], a, b, dil_outs,
                    bf(w_br_mla[l]), bf(w_br_moba[l]), bf(w_br_dil[l]), bf(w_out[l])).reshape(n, d)
        x2 = _ffn(x2, row(ffn2_norm[l]), bf(ffn2_w_gate[l]), bf(ffn2_w_up[l]), bf(ffn2_w_down[l]),
                  final_gain, l == DEPTH - 1)
    return x2.reshape(batch, seq, d)
```

```python
import functools

import numpy as np
import jax
import jax.numpy as jnp
from jax import lax
from jax.experimental import pallas as pl
from jax.experimental.pallas import tpu as pltpu

F32 = jnp.float32
BF16 = jnp.bfloat16

D_MODEL = 1024
DEPTH = 4
HEAD_DIM = 64
MLA_HEADS = 6
MLA_NOPE = 64
MLA_ROPE = 32
MLA_V = 64
MLA_Q_RANK = 256
MLA_KV_RANK = 128
ROPE_THETA = 10000.0
MOBA_HEADS = 6
MOBA_BLOCK = 256
MOBA_TOPK = 3
DIL_PATTERNS = ((128, 1), (512, 4), (2048, 16))
DIL_GROUP_HEADS = 4
DIL_HEADS = DIL_GROUP_HEADS * len(DIL_PATTERNS)
N_ALIBI = DIL_HEADS + MOBA_HEADS
N_BRANCHES = 3
D_FF = 2816
NORM_EPS = 1e-6
NEG_INF = -1e30
MLA_IN = MLA_Q_RANK + MLA_KV_RANK + MLA_ROPE
MOBA_IN = 3 * MOBA_HEADS * HEAD_DIM
DIL_IN = 3 * DIL_HEADS * HEAD_DIM
MLA_OUT = MLA_HEADS * MLA_V
MOBA_OUT = MOBA_HEADS * HEAD_DIM
DIL_OUT = DIL_GROUP_HEADS * HEAD_DIM

LANES = 128
MLA_QK = MLA_NOPE + MLA_ROPE
MLA_QPAD = MLA_HEADS * LANES
LOG2E = 1.4426950408889634
MLA_SCALE = MLA_QK ** -0.5
HEAD_SCALE = HEAD_DIM ** -0.5
DIL_L = DIL_PATTERNS[0][0] // DIL_PATTERNS[0][1]
DIL_COLS = 3 * DIL_GROUP_HEADS * HEAD_DIM

C_CQ = 0
C_CKV = C_CQ + MLA_Q_RANK
C_KR = C_CKV + MLA_KV_RANK
C_KR2 = C_KR + LANES
C_MOBA_K = C_KR2 + LANES
MOBA_KPAD = MOBA_HEADS * LANES
C_DIL = C_MOBA_K + MOBA_KPAD
C_END = C_DIL + DIL_IN

AUX_POS = HEAD_DIM
AUX_POS_PARTS = 3
AUX_SEL = HEAD_DIM + 16

VMEM_LIMIT = 56 * 1024 * 1024

TM_FFN = 512
TM_PROJ = 512
TM_MERGE = 512
FF_CHUNK = 256
MLA_T = 256
MLA_NH = 6
MOBA_NH = 6
DIL_STEP_ROWS = 512


def _cparams(sem):
    return pltpu.CompilerParams(dimension_semantics=sem, vmem_limit_bytes=VMEM_LIMIT)


def _resident(shape):
    nd = len(shape)
    return pl.BlockSpec(shape, lambda *_: (0,) * nd, pipeline_mode=pl.Buffered(1))


def _rms(x, gain):
    ms = jnp.mean(x * x, axis=-1, keepdims=True)
    return x * lax.rsqrt(ms + NORM_EPS) * gain


def _dot(a, b):
    return jnp.dot(a, b, preferred_element_type=F32)


def _dot_nt(a, b):
    return lax.dot_general(a, b, (((1,), (1,)), ((), ())), preferred_element_type=F32)


def _ffn_kernel(x_ref, g_ref, wg_ref, wu_ref, wd_ref, fg_ref, o_ref, *, final):
    x = x_ref[...]
    h = _rms(x, g_ref[...]).astype(BF16)
    acc = jnp.zeros(x.shape, F32)
    for c in range(D_FF // FF_CHUNK):
        sl = slice(c * FF_CHUNK, (c + 1) * FF_CHUNK)
        a = _dot(h, wg_ref[:, sl])
        u = _dot(h, wu_ref[:, sl])
        g = (a * jax.nn.sigmoid(a) * u).astype(BF16)
        acc = acc + _dot(g, wd_ref[sl, :])
    y = x + 0.5 * acc
    if final:
        y = _rms(y, fg_ref[...])
    o_ref[...] = y


def _ffn(x2, gain, wg, wu, wd, final_gain, final):
    n = x2.shape[0]
    return pl.pallas_call(
        functools.partial(_ffn_kernel, final=final),
        grid=(n // TM_FFN,),
        in_specs=[
            pl.BlockSpec((TM_FFN, D_MODEL), lambda i: (i, 0)),
            _resident((1, D_MODEL)),
            _resident((D_MODEL, D_FF)),
            _resident((D_MODEL, D_FF)),
            _resident((D_FF, D_MODEL)),
            _resident((1, D_MODEL)),
        ],
        out_specs=pl.BlockSpec((TM_FFN, D_MODEL), lambda i: (i, 0)),
        out_shape=jax.ShapeDtypeStruct((n, D_MODEL), F32),
        compiler_params=_cparams(("parallel",)),
        name="ffn_final" if final else "ffn",
    )(x2, gain, wg, wu, wd, final_gain)


def _inproj_kernel(x_ref, gn_ref, w1_ref, wt_ref, qn_ref, wuqt_ref, kvn_ref, wk_ref, wvt_ref,
                   cos_ref, sin_ref, cost_ref, sint_ref, kaux_ref,
                   mqt_ref, mk_ref, mvt_ref, bqt_ref, bk_ref, bvt_ref, d0_ref, d1_ref, d2_ref,
                   stage_ref):
    x = x_ref[0]
    h = _rms(x, gn_ref[...]).astype(BF16)

    pm = _dot(h, w1_ref[:, C_CQ:C_MOBA_K])
    cq = _rms(pm[:, C_CQ:C_CKV], qn_ref[...]).astype(BF16)
    ckv = _rms(pm[:, C_CKV:C_KR], kvn_ref[...]).astype(BF16)
    k_rope = pm[:, C_KR:C_KR2] * cos_ref[...] + pm[:, C_KR2:C_MOBA_K] * sin_ref[...]
    kk = _dot(ckv, wk_ref[...])
    qqt = _dot_nt(wuqt_ref[...], cq)
    cost = cost_ref[...]
    sint = sint_ref[...]
    for hh in range(MLA_HEADS):
        sl = slice(hh * LANES, (hh + 1) * LANES)
        sl2 = slice(MLA_QPAD + hh * LANES, MLA_QPAD + (hh + 1) * LANES)
        mqt_ref[0, sl, :] = ((qqt[sl] * cost + qqt[sl2] * sint) * (MLA_SCALE * LOG2E)).astype(BF16)
        mk_ref[0, :, sl] = (kk[:, sl] + k_rope).astype(BF16)
    mvt_ref[0] = _dot_nt(wvt_ref[...], ckv).astype(BF16)

    bt = _dot_nt(wt_ref[...], h)
    bqt_ref[0] = (bt[:MOBA_OUT] * (HEAD_SCALE * LOG2E)).astype(BF16)
    bvt_ref[0] = bt[MOBA_OUT:].astype(BF16)
    bk_ref[0] = (_dot(h, w1_ref[:, C_MOBA_K:C_DIL]) + kaux_ref[...].astype(F32)).astype(BF16)

    tm = x.shape[0]
    for g, d_ref in enumerate((d0_ref, d1_ref, d2_ref)):
        dil = DIL_PATTERNS[g][1]
        dd = _dot(h, w1_ref[:, C_DIL + g * DIL_COLS:C_DIL + (g + 1) * DIL_COLS])
        for c in range(DIL_COLS // LANES):
            cols = slice(c * LANES, (c + 1) * LANES)
            blk = dd[:, cols]
            stage_ref[c] = blk * HEAD_SCALE if c < DIL_OUT // LANES else blk
            for r in range(dil):
                d_ref[0, r, :, cols] = stage_ref[c, pl.ds(r, tm // dil, stride=dil), :].astype(BF16)


def _inproj(x3, gain, lw, tables):
    batch, seq, _ = x3.shape
    tm = TM_PROJ
    cos_t, sin_t, cos_tt, sin_tt, kaux = tables
    tok = lambda c: pl.BlockSpec((1, tm, c), lambda b, i: (b, i, 0))
    feat = lambda c: pl.BlockSpec((1, c, tm), lambda b, i: (b, 0, i))
    tok_sds = lambda c: jax.ShapeDtypeStruct((batch, seq, c), BF16)
    feat_sds = lambda c: jax.ShapeDtypeStruct((batch, c, seq), BF16)
    dils = [dil for _, dil in DIL_PATTERNS]
    stream = lambda dil: pl.BlockSpec((1, dil, tm // dil, DIL_COLS), lambda b, i: (b, 0, i, 0))
    stream_sds = lambda dil: jax.ShapeDtypeStruct((batch, dil, seq // dil, DIL_COLS), BF16)
    return pl.pallas_call(
        _inproj_kernel,
        grid=(batch, seq // tm),
        in_specs=[
            tok(D_MODEL),
            _resident((1, D_MODEL)),
            _resident(lw["w1"].shape),
            _resident(lw["wt"].shape),
            _resident((1, MLA_Q_RANK)),
            _resident(lw["wuqt"].shape),
            _resident((1, MLA_KV_RANK)),
            _resident(lw["wk"].shape),
            _resident(lw["wvt"].shape),
            pl.BlockSpec((tm, LANES), lambda b, i: (i, 0)),
            pl.BlockSpec((tm, LANES), lambda b, i: (i, 0)),
            pl.BlockSpec((LANES, tm), lambda b, i: (0, i)),
            pl.BlockSpec((LANES, tm), lambda b, i: (0, i)),
            pl.BlockSpec((tm, MOBA_KPAD), lambda b, i: (i, 0)),
        ],
        out_specs=[feat(MLA_QPAD), tok(MLA_QPAD), feat(MLA_OUT),
                   feat(MOBA_OUT), tok(MOBA_KPAD), feat(MOBA_OUT)] + [stream(dil) for dil in dils],
        out_shape=[feat_sds(MLA_QPAD), tok_sds(MLA_QPAD), feat_sds(MLA_OUT),
                   feat_sds(MOBA_OUT), tok_sds(MOBA_KPAD), feat_sds(MOBA_OUT)]
        + [stream_sds(dil) for dil in dils],
        scratch_shapes=[pltpu.VMEM((DIL_COLS // LANES, tm, LANES), F32)],
        compiler_params=_cparams(("parallel", "parallel")),
        name="inproj",
    )(x3, gain, lw["w1"], lw["wt"], lw["qn"], lw["wuqt"], lw["kvn"], lw["wk"], lw["wvt"],
      cos_t, sin_t, cos_tt, sin_tt, kaux)


ROW_M, ROW_ALPHA, ROW_L, ROW_SMAX = range(4)


def _flash_heads(nh, i, t, dv, score_fn, value_fn, diag_fn, s_buf, acc_buf, stat_buf):
    heads = range(nh)
    col_max = lambda s: jnp.max(s, axis=0, keepdims=True)
    stat = lambda h, r: stat_buf[h, r:r + 1, :]

    def set_stat(h, r, v):
        stat_buf[h, r:r + 1, :] = v

    def values(h, j, p):
        pv = _dot(value_fn(j, h), p)
        acc_buf[h] = stat(h, ROW_ALPHA) * acc_buf[h] + pv

    def softmax(h, s, s_max):
        m = stat(h, ROW_M)
        m_new = jnp.maximum(m, s_max)
        alpha = jnp.exp2(m - m_new)
        p = jnp.exp2(s - m_new)
        set_stat(h, ROW_M, m_new)
        set_stat(h, ROW_ALPHA, alpha)
        set_stat(h, ROW_L, alpha * stat(h, ROW_L) + jnp.sum(p, axis=0, keepdims=True))
        return p.astype(BF16)

    for h in heads:
        sc = score_fn(0, h)
        s_buf[h] = sc
        set_stat(h, ROW_SMAX, col_max(sc))
        set_stat(h, ROW_M, jnp.full((1, t), NEG_INF, F32))
        set_stat(h, ROW_ALPHA, jnp.ones((1, t), F32))
        set_stat(h, ROW_L, jnp.zeros((1, t), F32))
        acc_buf[h] = jnp.zeros((dv, t), F32)

    @pl.loop(0, i)
    def _(j):
        for h in heads:
            sc = score_fn(j + 1, h)
            values(h, j, softmax(h, s_buf[h], stat(h, ROW_SMAX)))
            s_buf[h] = sc
            set_stat(h, ROW_SMAX, col_max(sc))

    for h in heads:
        s = diag_fn(h, s_buf[h])
        values(h, i, softmax(h, s, col_max(s)))
    return [acc_buf[h] / stat(h, ROW_L) for h in heads]


def _store_token_major(o_ref, outs):
    for p in range(len(outs) // 2):
        pair = jnp.concatenate([outs[2 * p], outs[2 * p + 1]], axis=0)
        o_ref[0, :, p * LANES:(p + 1) * LANES] = pair.T.astype(o_ref.dtype)


def _flash_scratch(nh, t, dv):
    return [pltpu.VMEM((nh, t, t), F32), pltpu.VMEM((nh, dv, t), F32), pltpu.VMEM((nh, 8, t), F32)]


def _causal_mask_t(t):
    key = lax.broadcasted_iota(jnp.int32, (t, t), 0)
    qry = lax.broadcasted_iota(jnp.int32, (t, t), 1)
    return key <= qry


def _mla_kernel(qt_ref, k_ref, vt_ref, o_ref, *flash_scratch):
    i = pl.program_id(2)
    t = MLA_T
    nh = MLA_NH
    qs = [qt_ref[0, h * LANES:h * LANES + MLA_QK, :] for h in range(nh)]

    def scores(j, h):
        start = pl.multiple_of(j * t, t)
        return _dot(k_ref[0, pl.ds(start, t), h * LANES:h * LANES + MLA_QK], qs[h])

    def values(j, h):
        return vt_ref[0, h * MLA_V:(h + 1) * MLA_V, pl.ds(pl.multiple_of(j * t, t), t)]

    causal = _causal_mask_t(t)
    outs = _flash_heads(nh, i, t, MLA_V, scores, values,
                        lambda h, s: jnp.where(causal, s, NEG_INF),
                        *flash_scratch)
    _store_token_major(o_ref, outs)


def _mla(mqt, mk, mvt):
    batch, seq, _ = mk.shape
    t = MLA_T
    nh = MLA_NH
    return pl.pallas_call(
        _mla_kernel,
        grid=(batch, MLA_HEADS // nh, seq // t),
        in_specs=[pl.BlockSpec((1, nh * LANES, t), lambda b, g, i: (b, g, i)),
                  pl.BlockSpec((1, seq, nh * LANES), lambda b, g, i: (b, 0, g)),
                  pl.BlockSpec((1, nh * MLA_V, seq), lambda b, g, i: (b, g, 0))],
        out_specs=pl.BlockSpec((1, t, nh * MLA_V), lambda b, g, i: (b, i, g)),
        out_shape=jax.ShapeDtypeStruct((batch, seq, MLA_OUT), BF16),
        scratch_shapes=_flash_scratch(nh, t, MLA_V),
        compiler_params=_cparams(("parallel", "parallel", "arbitrary")),
        name="mla_attn",
    )(mqt, mk, mvt)


def _moba_kernel(qt_ref, k_ref, vt_ref, o_ref, kmh_ref, kml_ref, *flash_scratch, nblk):
    i = pl.program_id(2)
    t = MOBA_BLOCK
    nh = MOBA_NH

    @pl.when(i == 0)
    def _():
        kf = k_ref[0].astype(F32).reshape(nblk, t, nh * LANES)
        km = jnp.sum(kf, axis=1) * (1.0 / t)
        hi = km.astype(BF16)
        kmh_ref[...] = hi
        kml_ref[...] = (km - hi.astype(F32)).astype(BF16)

    blk = lax.broadcasted_iota(jnp.int32, (nblk, 1), 0)
    blk_f = blk.astype(F32)
    pos_rows =lax.broadcasted_iota(jnp.int32, (AUX_SEL - AUX_POS, 1), 0) < AUX_POS_PARTS
    ones_rows = jnp.broadcast_to(jnp.where(pos_rows, 1.0, 0.0), (AUX_SEL - AUX_POS, t)).astype(BF16)
    pad_q = jnp.zeros((LANES - HEAD_DIM, t), BF16)
    qs = []
    for h in range(nh):
        q = qt_ref[0, h * HEAD_DIM:(h + 1) * HEAD_DIM, :]
        q_plain = jnp.concatenate([q, pad_q], axis=0)
        cols = slice(h * LANES, (h + 1) * LANES)
        gate = _dot(kmh_ref[:, cols], q_plain) + _dot(kml_ref[:, cols], q_plain)
        gate = jnp.where(blk < i, gate, NEG_INF)
        chosen = blk == i
        for _ in range(min(MOBA_TOPK, nblk)):
            best = jnp.max(gate, axis=0, keepdims=True)
            first = jnp.min(jnp.where(gate == best, blk_f, float(nblk)), axis=0, keepdims=True)
            pick = blk_f == first
            chosen = chosen | (pick & (blk < i))
            gate = jnp.where(pick, -jnp.inf, gate)
        sel = jnp.where(chosen, 0.0, NEG_INF).astype(BF16)
        qs.append(jnp.concatenate([q, ones_rows, sel], axis=0))

    def scores(j, h):
        start = pl.multiple_of(j * t, t)
        return _dot(k_ref[0, pl.ds(start, t), h * LANES:h * LANES + AUX_SEL + nblk], qs[h])

    def values(j, h):
        return vt_ref[0, h * HEAD_DIM:(h + 1) * HEAD_DIM, pl.ds(pl.multiple_of(j * t, t), t)]

    causal = _causal_mask_t(t)
    outs = _flash_heads(nh, i, t, HEAD_DIM, scores, values,
                        lambda h, s: jnp.where(causal, s, NEG_INF),
                        *flash_scratch)
    _store_token_major(o_ref, outs)


def _moba(bqt, bk, bvt):
    batch, seq, _ = bk.shape
    t = MOBA_BLOCK
    nh = MOBA_NH
    nblk = seq // t
    assert AUX_SEL + nblk <= LANES
    cols = nh * HEAD_DIM
    return pl.pallas_call(
        functools.partial(_moba_kernel, nblk=nblk),
        grid=(batch, MOBA_HEADS // nh, nblk),
        in_specs=[
            pl.BlockSpec((1, cols, t), lambda b, g, i: (b, g, i)),
            pl.BlockSpec((1, seq, nh * LANES), lambda b, g, i: (b, 0, g)),
            pl.BlockSpec((1, cols, seq), lambda b, g, i: (b, g, 0)),
        ],
        out_specs=pl.BlockSpec((1, t, cols), lambda b, g, i: (b, i, g)),
        out_shape=jax.ShapeDtypeStruct((batch, seq, MOBA_OUT), BF16),
        scratch_shapes=[
            pltpu.VMEM((nblk, nh * LANES), BF16),
            pltpu.VMEM((nblk, nh * LANES), BF16),
        ] + _flash_scratch(nh, t, HEAD_DIM),
        compiler_params=_cparams(("parallel", "parallel", "arbitrary")),
        name="moba_attn",
    )(bqt, bk, bvt)


def _dil_kernel(own_ref, prev_ref, bias_ref, o_ref, lse_ref, *, ns, qb):
    n = pl.program_id(2)
    L = DIL_L
    low = lax.broadcasted_iota(jnp.int32, (1, LANES), 1) < HEAD_DIM
    first_cols = lax.broadcasted_iota(jnp.int32, (1, 2 * L), 1) < L
    problems = [(si, pair, blk) for si in range(ns) for pair in range(DIL_OUT // LANES)
                for blk in range(qb // L)]

    def window(si, blk, cols):
        if blk == 0:
            return jnp.concatenate([prev_ref[0, si, :, cols], own_ref[0, si, :L, cols]], axis=0)
        return own_ref[0, si, (blk - 1) * L:(blk + 1) * L, cols]

    scores = []
    for si, pair, blk in problems:
        qp = own_ref[0, si, blk * L:(blk + 1) * L, pair * LANES:(pair + 1) * LANES]
        zero = jnp.zeros_like(qp)
        keys = window(si, blk, slice(DIL_OUT + pair * LANES, DIL_OUT + (pair + 1) * LANES))
        scores.append([_dot_nt(jnp.where(low, qp, zero), keys),
                       _dot_nt(jnp.where(low, zero, qp), keys)])

    probs = []
    for (si, pair, blk), sc in zip(problems, scores):
        per_head = []
        for hh in range(2):
            s = sc[hh] + bias_ref[2 * pair + hh]
            if blk == 0:
                s = jnp.where(first_cols & (n == 0), NEG_INF, s)
            m = jnp.max(s, axis=-1, keepdims=True)
            e = jnp.exp(s - m)
            den = jnp.sum(e, axis=-1, keepdims=True)
            per_head.append((e.astype(BF16), den, m + jnp.log(den)))
        probs.append(per_head)

    for (si, pair, blk), per_head in zip(problems, probs):
        vals = window(si, blk, slice(2 * DIL_OUT + pair * LANES, 2 * DIL_OUT + (pair + 1) * LANES))
        o_pair = [_dot(e, vals) / den for e, den, _ in per_head]
        lse_pair = [jnp.broadcast_to(lse, (L, LANES)) for _, _, lse in per_head]
        dst = (0, si, slice(blk * L, (blk + 1) * L), slice(pair * LANES, (pair + 1) * LANES))
        o_ref[dst] = jnp.where(low, o_pair[0], o_pair[1])
        lse_ref[dst] = jnp.where(low, lse_pair[0], lse_pair[1])


def _dil(dg, bias):
    batch, dilation, rows, _ = dg.shape
    L = DIL_L
    qb = min(DIL_STEP_ROWS, rows)
    ns = min(DIL_STEP_ROWS // qb, dilation)
    per = qb // L
    out_sds = jax.ShapeDtypeStruct((batch, dilation, rows, DIL_OUT), F32)
    return pl.pallas_call(
        functools.partial(_dil_kernel, ns=ns, qb=qb),
        grid=(batch, dilation // ns, rows // qb),
        in_specs=[
            pl.BlockSpec((1, ns, qb, DIL_COLS), lambda b, r, n: (b, r, n, 0)),
            pl.BlockSpec((1, ns, L, DIL_COLS),
                         lambda b, r, n: (b, r, jnp.maximum(n * per - 1, 0), 0)),
            _resident(bias.shape),
        ],
        out_specs=[pl.BlockSpec((1, ns, qb, DIL_OUT), lambda b, r, n: (b, r, n, 0))] * 2,
        out_shape=[out_sds, out_sds],
        compiler_params=_cparams(("parallel", "parallel", "arbitrary")),
        name="dil_attn_d%d" % dilation,
    )(dg, dg, bias)


def _merge_kernel(x_ref, gn_ref, wgate_ref, a_ref, b_ref, o0_ref, o1_ref, o2_ref,
                  l0_ref, l1_ref, l2_ref, wa_ref, wb_ref, wc_ref, wo_ref, out_ref, *scratch):
    x = x_ref[0]
    h = _rms(x, gn_ref[...]).astype(BF16)
    tm = x.shape[0]
    scratch = list(scratch)

    def token_order(ref):
        dil = ref.shape[1]
        if dil == 1:
            return ref[0, 0]
        buf = scratch.pop()
        for c in range(buf.shape[0]):
            for r in range(dil):
                buf[c, pl.ds(r, tm // dil, stride=dil), :] = ref[0, r, :, c * LANES:(c + 1) * LANES]
        return jnp.concatenate([buf[c] for c in range(buf.shape[0])], axis=1)

    lses = [token_order(r) for r in (l0_ref, l1_ref, l2_ref)]
    outs = [token_order(r) for r in (o0_ref, o1_ref, o2_ref)]
    mx = jnp.maximum(jnp.maximum(lses[0], lses[1]), lses[2])
    es = [jnp.exp(l - mx) for l in lses]
    den = es[0] + es[1] + es[2]
    c = (es[0] / den) * outs[0] + (es[1] / den) * outs[1] + (es[2] / den) * outs[2]

    branches = ((a_ref[0], wa_ref), (b_ref[0], wb_ref), (c.astype(BF16), wc_ref))
    merged = jnp.zeros(x.shape, F32)
    for k, (act, w_ref) in enumerate(branches):
        gate = jax.nn.sigmoid(_dot(h, wgate_ref[:, k * D_MODEL:(k + 1) * D_MODEL]))
        merged = merged + gate * _dot(act, w_ref[...])
    out_ref[0] = x + _dot(merged.astype(BF16), wo_ref[...])


def _merge(x3, gain, wgate, a, b, dil_outs, wa, wb, wc, wo):
    batch, seq, _ = x3.shape
    tm = TM_MERGE
    tok = lambda c: pl.BlockSpec((1, tm, c), lambda b, i: (b, i, 0))
    stream = lambda arr: pl.BlockSpec((1, arr.shape[1], tm // arr.shape[1], DIL_OUT),
                                      lambda b, i: (b, 0, i, 0))
    (o0, l0), (o1, l1), (o2, l2) = dil_outs
    streams = (o0, o1, o2, l0, l1, l2)
    n_reordered = sum(arr.shape[1] > 1 for arr in streams)
    return pl.pallas_call(
        _merge_kernel,
        grid=(batch, seq // tm),
        in_specs=[tok(D_MODEL), _resident((1, D_MODEL)), _resident(wgate.shape),
                  tok(MLA_OUT), tok(MOBA_OUT)]
        + [stream(arr) for arr in streams]
        + [_resident(wa.shape), _resident(wb.shape), _resident(wc.shape), _resident(wo.shape)],
        out_specs=tok(D_MODEL),
        out_shape=jax.ShapeDtypeStruct((batch, seq, D_MODEL), F32),
        scratch_shapes=[pltpu.VMEM((DIL_OUT // LANES, tm, LANES), F32)] * n_reordered,
        compiler_params=_cparams(("parallel", "parallel")),
        name="merge",
    )(x3, gain, wgate, a, b, *streams, wa, wb, wc, wo)


def _alibi_slopes():
    return (2.0 ** (-8.0 * np.arange(1, N_ALIBI + 1, dtype=np.float32) / N_ALIBI)).astype(np.float32)


def _rope_tables(seq):
    pos = jnp.arange(seq, dtype=F32)
    inv = ROPE_THETA ** (-jnp.arange(0, MLA_ROPE, 2, dtype=F32) / MLA_ROPE)
    ang = pos[:, None] * inv[None, :]
    cos, sin = jnp.cos(ang), jnp.sin(ang)
    pad = jnp.zeros((seq, LANES - MLA_QK), F32)
    cos_t = jnp.concatenate([jnp.ones((seq, MLA_NOPE), F32), cos, cos, pad], axis=1)
    sin_t = jnp.concatenate([jnp.zeros((seq, MLA_NOPE), F32), -sin, sin, pad], axis=1)
    return cos_t, sin_t, cos_t.T, sin_t.T


def _dil_bias(group, dilation):
    L = DIL_L
    slopes = _alibi_slopes()[group * DIL_GROUP_HEADS:(group + 1) * DIL_GROUP_HEADS]
    steps = L + np.arange(L)[:, None] - np.arange(2 * L)[None, :]
    valid = (steps >= 0) & (steps <= L)
    dist = (steps * dilation).astype(np.float32)
    bias = np.where(valid[None], -slopes[:, None, None] * dist[None], np.float32(NEG_INF))
    return jnp.asarray(bias.astype(np.float32))


def _moba_key_aux(seq):
    slopes = _alibi_slopes()[DIL_HEADS:].astype(np.float64) * LOG2E
    pos = jnp.asarray((slopes[None, :] * np.arange(seq)[:, None]).astype(np.float32))
    aux = jnp.zeros((seq, MOBA_HEADS, LANES), F32)
    rest = pos
    for part in range(AUX_POS_PARTS):
        piece = rest.astype(BF16).astype(F32)
        aux = aux.at[:, :, AUX_POS + part].set(piece)
        rest = rest - piece
    onehot = jax.nn.one_hot(jnp.arange(seq) // MOBA_BLOCK, seq // MOBA_BLOCK, dtype=F32)
    aux = aux.at[:, :, AUX_SEL:AUX_SEL + seq // MOBA_BLOCK].set(onehot[:, None, :])
    return aux.reshape(seq, MOBA_KPAD).astype(BF16)


def _layer_weights(w_in, w_uq, w_ukv, q_norm, kv_norm):
    d = w_in.shape[0]
    z = lambda c: jnp.zeros((d, c), w_in.dtype)
    i0 = MLA_Q_RANK + MLA_KV_RANK
    kr = w_in[:, i0:i0 + MLA_ROPE]
    hr = MLA_ROPE // 2
    kr_pad = jnp.concatenate([z(MLA_NOPE), kr, z(LANES - MLA_QK)], axis=1)
    kr_swap = jnp.concatenate([z(MLA_NOPE), kr[:, hr:], kr[:, :hr], z(LANES - MLA_QK)], axis=1)
    moba = w_in[:, MLA_IN:MLA_IN + MOBA_IN]
    bq, bk, bv = (moba[:, c * MOBA_OUT:(c + 1) * MOBA_OUT] for c in range(3))
    bk = jnp.concatenate([bk.reshape(d, MOBA_HEADS, HEAD_DIM),
                          jnp.zeros((d, MOBA_HEADS, LANES - HEAD_DIM), w_in.dtype)],
                         axis=2).reshape(d, MOBA_KPAD)
    dil = w_in[:, MLA_IN + MOBA_IN:MLA_IN + MOBA_IN + DIL_IN].reshape(d, 3, DIL_HEADS, HEAD_DIM)
    groups = [dil[:, :, g * DIL_GROUP_HEADS:(g + 1) * DIL_GROUP_HEADS].reshape(d, DIL_COLS)
              for g in range(len(DIL_PATTERNS))]
    w1 = jnp.concatenate([w_in[:, :i0], kr_pad, kr_swap, bk] + groups, axis=1).astype(BF16)
    wt = jnp.concatenate([bq, bv], axis=1).T.astype(BF16)
    wgate = w_in[:, MLA_IN + MOBA_IN + DIL_IN:].astype(BF16)

    r = w_uq.shape[0]
    uq = w_uq.reshape(r, MLA_HEADS, MLA_QK)
    zq = jnp.zeros((r, MLA_HEADS, LANES - MLA_QK), w_uq.dtype)
    zn = jnp.zeros((r, MLA_HEADS, MLA_NOPE), w_uq.dtype)
    uq_pad = jnp.concatenate([uq, zq], axis=2).reshape(r, MLA_QPAD)
    uq_swap = jnp.concatenate(
        [zn, uq[:, :, MLA_NOPE + hr:], uq[:, :, MLA_NOPE:MLA_NOPE + hr], zq], axis=2
    ).reshape(r, MLA_QPAD)
    wuqt = jnp.concatenate([uq_pad, uq_swap], axis=1).T.astype(BF16)

    rk = w_ukv.shape[0]
    ukv = w_ukv.reshape(rk, MLA_HEADS, MLA_NOPE + MLA_V)
    zk = jnp.zeros((rk, MLA_HEADS, LANES - MLA_NOPE), w_ukv.dtype)
    wk = jnp.concatenate([ukv[:, :, :MLA_NOPE], zk], axis=2).reshape(rk, MLA_QPAD).astype(BF16)
    wvt = ukv[:, :, MLA_NOPE:].reshape(rk, MLA_OUT).T.astype(BF16)
    return dict(w1=w1, wt=wt, wgate=wgate, wuqt=wuqt, wk=wk, wvt=wvt,
                qn=q_norm.reshape(1, -1), kvn=kv_norm.reshape(1, -1))


def kernel(x, ffn1_norm, ffn1_w_gate, ffn1_w_up, ffn1_w_down, mix_norm, w_in, q_norm, w_uq, kv_norm, w_ukv, w_br_mla, w_br_moba, w_br_dil, w_out, ffn2_norm, ffn2_w_gate, ffn2_w_up, ffn2_w_down, final_norm):
    batch, seq, d = x.shape
    assert d == D_MODEL and seq % (DIL_PATTERNS[-1][1] * DIL_L) == 0 and seq % TM_PROJ == 0
    n = batch * seq
    x2 = x.reshape(n, d)
    tables = _rope_tables(seq) + (_moba_key_aux(seq),)
    dil_bias = [_dil_bias(g, dil) for g, (_, dil) in enumerate(DIL_PATTERNS)]
    final_gain = final_norm.reshape(1, d)
    bf = lambda w: w.astype(BF16)
    row = lambda v: v.reshape(1, -1)

    for l in range(DEPTH):
        x2 = _ffn(x2, row(ffn1_norm[l]), bf(ffn1_w_gate[l]), bf(ffn1_w_up[l]), bf(ffn1_w_down[l]),
                  final_gain, False)
        lw = _layer_weights(w_in[l], w_uq[l], w_ukv[l], q_norm[l], kv_norm[l])
        mqt, mk, mvt, bqt, bk, bvt, d0, d1, d2 = _inproj(
            x2.reshape(batch, seq, d), row(mix_norm[l]), lw, tables)
        a = _mla(mqt, mk, mvt)
        b = _moba(bqt, bk, bvt)
        dil_outs = [_dil(dg, dil_bias[g]) for g, dg in enumerate((d0, d1, d2))]
        x2 = _merge(x2.reshape(batch, seq, d), row(mix_norm[l]), lw["wgate"], a, b, dil_outs,
                    bf(w_br_mla[l]), bf(w_br_moba[l]), bf(w_br_dil[l]), bf(w_out[l])).reshape(n, d)
        x2 = _ffn(x2, row(ffn2_norm[l]), bf(ffn2_w_gate[l]), bf(ffn2_w_up[l]), bf(ffn2_w_down[l]),
                  final_gain, l == DEPTH - 1)
    return x2.reshape(batch, seq, d)
```

```python
import functools

import numpy as np
import jax
import jax.numpy as jnp
from jax import lax
from jax.experimental import pallas as pl
from jax.experimental.pallas import tpu as pltpu

F32 = jnp.float32
BF16 = jnp.bfloat16

D_MODEL = 1024
DEPTH = 4
HEAD_DIM = 64
MLA_HEADS = 6
MLA_NOPE = 64
MLA_ROPE = 32
MLA_V = 64
MLA_Q_RANK = 256
MLA_KV_RANK = 128
ROPE_THETA = 10000.0
MOBA_HEADS = 6
MOBA_BLOCK = 256
MOBA_TOPK = 3
DIL_PATTERNS = ((128, 1), (512, 4), (2048, 16))
DIL_GROUP_HEADS = 4
DIL_HEADS = DIL_GROUP_HEADS * len(DIL_PATTERNS)
N_ALIBI = DIL_HEADS + MOBA_HEADS
N_BRANCHES = 3
D_FF = 2816
NORM_EPS = 1e-6
NEG_INF = -1e30
MLA_IN = MLA_Q_RANK + MLA_KV_RANK + MLA_ROPE
MOBA_IN = 3 * MOBA_HEADS * HEAD_DIM
DIL_IN = 3 * DIL_HEADS * HEAD_DIM
MLA_OUT = MLA_HEADS * MLA_V
MOBA_OUT = MOBA_HEADS * HEAD_DIM
DIL_OUT = DIL_GROUP_HEADS * HEAD_DIM

LANES = 128
MLA_QK = MLA_NOPE + MLA_ROPE
MLA_QPAD = MLA_HEADS * LANES
LOG2E = 1.4426950408889634
MLA_SCALE = MLA_QK ** -0.5
HEAD_SCALE = HEAD_DIM ** -0.5
DIL_L = DIL_PATTERNS[0][0] // DIL_PATTERNS[0][1]
DIL_COLS = 3 * DIL_GROUP_HEADS * HEAD_DIM

C_CQ = 0
C_CKV = C_CQ + MLA_Q_RANK
C_KR = C_CKV + MLA_KV_RANK
C_KR2 = C_KR + LANES
C_MOBA_K = C_KR2 + LANES
MOBA_KPAD = MOBA_HEADS * LANES
C_DIL = C_MOBA_K + MOBA_KPAD
C_END = C_DIL + DIL_IN

AUX_POS = HEAD_DIM
AUX_POS_PARTS = 3
AUX_SEL = HEAD_DIM + 16

VMEM_LIMIT = 56 * 1024 * 1024

TM_FFN = 512
TM_PROJ = 512
TM_MERGE = 512
FF_CHUNK = 256
MLA_T = 256
MLA_NH = 6
MOBA_NH = 6
DIL_STEP_ROWS = 512


def _cparams(sem):
    return pltpu.CompilerParams(dimension_semantics=sem, vmem_limit_bytes=VMEM_LIMIT)


def _resident(shape):
    nd = len(shape)
    return pl.BlockSpec(shape, lambda *_: (0,) * nd, pipeline_mode=pl.Buffered(1))


def _rms(x, gain):
    ms = jnp.mean(x * x, axis=-1, keepdims=True)
    return x * lax.rsqrt(ms + NORM_EPS) * gain


def _dot(a, b):
    return jnp.dot(a, b, preferred_element_type=F32)


def _dot_nt(a, b):
    return lax.dot_general(a, b, (((1,), (1,)), ((), ())), preferred_element_type=F32)


def _ffn_kernel(x_ref, g_ref, wg_ref, wu_ref, wd_ref, fg_ref, o_ref, *, final):
    x = x_ref[...]
    h = _rms(x, g_ref[...]).astype(BF16)
    acc = jnp.zeros(x.shape, F32)
    for c in range(D_FF // FF_CHUNK):
        sl = slice(c * FF_CHUNK, (c + 1) * FF_CHUNK)
        a = _dot(h, wg_ref[:, sl])
        u = _dot(h, wu_ref[:, sl])
        g = (a * jax.nn.sigmoid(a) * u).astype(BF16)
        acc = acc + _dot(g, wd_ref[sl, :])
    y = x + 0.5 * acc
    if final:
        y = _rms(y, fg_ref[...])
    o_ref[...] = y


def _ffn(x2, gain, wg, wu, wd, final_gain, final):
    n = x2.shape[0]
    return pl.pallas_call(
        functools.partial(_ffn_kernel, final=final),
        grid=(n // TM_FFN,),
        in_specs=[
            pl.BlockSpec((TM_FFN, D_MODEL), lambda i: (i, 0)),
            _resident((1, D_MODEL)),
            _resident((D_MODEL, D_FF)),
            _resident((D_MODEL, D_FF)),
            _resident((D_FF, D_MODEL)),
            _resident((1, D_MODEL)),
        ],
        out_specs=pl.BlockSpec((TM_FFN, D_MODEL), lambda i: (i, 0)),
        out_shape=jax.ShapeDtypeStruct((n, D_MODEL), F32),
        compiler_params=_cparams(("parallel",)),
        name="ffn_final" if final else "ffn",
    )(x2, gain, wg, wu, wd, final_gain)


def _inproj_kernel(x_ref, gn_ref, w1_ref, wt_ref, qn_ref, wuqt_ref, kvn_ref, wk_ref, wvt_ref,
                   cos_ref, sin_ref, cost_ref, sint_ref, kaux_ref,
                   mqt_ref, mk_ref, mvt_ref, bqt_ref, bk_ref, bvt_ref, d0_ref, d1_ref, d2_ref,
                   stage_ref):
    x = x_ref[0]
    h = _rms(x, gn_ref[...]).astype(BF16)

    pm = _dot(h, w1_ref[:, C_CQ:C_MOBA_K])
    cq = _rms(pm[:, C_CQ:C_CKV], qn_ref[...]).astype(BF16)
    ckv = _rms(pm[:, C_CKV:C_KR], kvn_ref[...]).astype(BF16)
    k_rope = pm[:, C_KR:C_KR2] * cos_ref[...] + pm[:, C_KR2:C_MOBA_K] * sin_ref[...]
    kk = _dot(ckv, wk_ref[...])
    qqt = _dot_nt(wuqt_ref[...], cq)
    cost = cost_ref[...]
    sint = sint_ref[...]
    for hh in range(MLA_HEADS):
        sl = slice(hh * LANES, (hh + 1) * LANES)
        sl2 = slice(MLA_QPAD + hh * LANES, MLA_QPAD + (hh + 1) * LANES)
        mqt_ref[0, sl, :] = ((qqt[sl] * cost + qqt[sl2] * sint) * (MLA_SCALE * LOG2E)).astype(BF16)
        mk_ref[0, :, sl] = (kk[:, sl] + k_rope).astype(BF16)
    mvt_ref[0] = _dot_nt(wvt_ref[...], ckv).astype(BF16)

    bt = _dot_nt(wt_ref[...], h)
    bqt_ref[0] = (bt[:MOBA_OUT] * (HEAD_SCALE * LOG2E)).astype(BF16)
    bvt_ref[0] = bt[MOBA_OUT:].astype(BF16)
    bk_ref[0] = (_dot(h, w1_ref[:, C_MOBA_K:C_DIL]) + kaux_ref[...].astype(F32)).astype(BF16)

    tm = x.shape[0]
    for g, d_ref in enumerate((d0_ref, d1_ref, d2_ref)):
        dil = DIL_PATTERNS[g][1]
        dd = _dot(h, w1_ref[:, C_DIL + g * DIL_COLS:C_DIL + (g + 1) * DIL_COLS])
        for c in range(DIL_COLS // LANES):
            cols = slice(c * LANES, (c + 1) * LANES)
            blk = dd[:, cols]
            stage_ref[c] = blk * HEAD_SCALE if c < DIL_OUT // LANES else blk
            for r in range(dil):
                d_ref[0, r, :, cols] = stage_ref[c, pl.ds(r, tm // dil, stride=dil), :].astype(BF16)


def _inproj(x3, gain, lw, tables):
    batch, seq, _ = x3.shape
    tm = TM_PROJ
    cos_t, sin_t, cos_tt, sin_tt, kaux = tables
    tok = lambda c: pl.BlockSpec((1, tm, c), lambda b, i: (b, i, 0))
    feat = lambda c: pl.BlockSpec((1, c, tm), lambda b, i: (b, 0, i))
    tok_sds = lambda c: jax.ShapeDtypeStruct((batch, seq, c), BF16)
    feat_sds = lambda c: jax.ShapeDtypeStruct((batch, c, seq), BF16)
    dils = [dil for _, dil in DIL_PATTERNS]
    stream = lambda dil: pl.BlockSpec((1, dil, tm // dil, DIL_COLS), lambda b, i: (b, 0, i, 0))
    stream_sds = lambda dil: jax.ShapeDtypeStruct((batch, dil, seq // dil, DIL_COLS), BF16)
    return pl.pallas_call(
        _inproj_kernel,
        grid=(batch, seq // tm),
        in_specs=[
            tok(D_MODEL),
            _resident((1, D_MODEL)),
            _resident(lw["w1"].shape),
            _resident(lw["wt"].shape),
            _resident((1, MLA_Q_RANK)),
            _resident(lw["wuqt"].shape),
            _resident((1, MLA_KV_RANK)),
            _resident(lw["wk"].shape),
            _resident(lw["wvt"].shape),
            pl.BlockSpec((tm, LANES), lambda b, i: (i, 0)),
            pl.BlockSpec((tm, LANES), lambda b, i: (i, 0)),
            pl.BlockSpec((LANES, tm), lambda b, i: (0, i)),
            pl.BlockSpec((LANES, tm), lambda b, i: (0, i)),
            pl.BlockSpec((tm, MOBA_KPAD), lambda b, i: (i, 0)),
        ],
        out_specs=[feat(MLA_QPAD), tok(MLA_QPAD), feat(MLA_OUT),
                   feat(MOBA_OUT), tok(MOBA_KPAD), feat(MOBA_OUT)] + [stream(dil) for dil in dils],
        out_shape=[feat_sds(MLA_QPAD), tok_sds(MLA_QPAD), feat_sds(MLA_OUT),
                   feat_sds(MOBA_OUT), tok_sds(MOBA_KPAD), feat_sds(MOBA_OUT)]
        + [stream_sds(dil) for dil in dils],
        scratch_shapes=[pltpu.VMEM((DIL_COLS // LANES, tm, LANES), F32)],
        compiler_params=_cparams(("parallel", "parallel")),
        name="inproj",
    )(x3, gain, lw["w1"], lw["wt"], lw["qn"], lw["wuqt"], lw["kvn"], lw["wk"], lw["wvt"],
      cos_t, sin_t, cos_tt, sin_tt, kaux)


ROW_M, ROW_ALPHA, ROW_L, ROW_SMAX = range(4)
FLASH_UNROLL = 8
SUM_ROWS = 16


def _flash_heads(nh, i, t, dv, score_fn, value_fn, diag_fn, s_buf, acc_buf, stat_buf):
    heads = range(nh)
    ones = jnp.ones((SUM_ROWS, t), BF16)
    col_max = lambda s: jnp.max(s, axis=0, keepdims=True)
    stat = lambda h, r: stat_buf[h, r:r + 1, :]

    def set_stat(h, r, v):
        stat_buf[h, r:r + 1, :] = v

    def values(h, j, p):
        pv = _dot(jnp.concatenate([value_fn(j, h), ones], axis=0), p)
        alpha = stat(h, ROW_ALPHA)
        acc_buf[h] = alpha * acc_buf[h] + pv[:dv]
        set_stat(h, ROW_L, alpha * stat(h, ROW_L) + pv[dv:dv + 1])

    def softmax(h, s, s_max):
        m = stat(h, ROW_M)
        m_new = jnp.maximum(m, s_max)
        alpha = jnp.exp2(m - m_new)
        p = jnp.exp2(s - m_new)
        set_stat(h, ROW_M, m_new)
        set_stat(h, ROW_ALPHA, alpha)
        return p.astype(BF16)

    for h in heads:
        sc = score_fn(0, h)
        s_buf[h] = sc
        set_stat(h, ROW_SMAX, col_max(sc))
        set_stat(h, ROW_M, jnp.full((1, t), NEG_INF, F32))
        set_stat(h, ROW_ALPHA, jnp.ones((1, t), F32))
        set_stat(h, ROW_L, jnp.zeros((1, t), F32))
        acc_buf[h] = jnp.zeros((dv, t), F32)

    def block_step(j):
        for h in heads:
            sc = score_fn(j + 1, h)
            values(h, j, softmax(h, s_buf[h], stat(h, ROW_SMAX)))
            s_buf[h] = sc
            set_stat(h, ROW_SMAX, col_max(sc))

    @pl.loop(0, i // FLASH_UNROLL)
    def _(jj):
        for u in range(FLASH_UNROLL):
            block_step(FLASH_UNROLL * jj + u)

    done = i - i % FLASH_UNROLL
    chunk = FLASH_UNROLL // 2
    while chunk:
        @pl.when(i % (2 * chunk) >= chunk)
        def _(done=done, chunk=chunk):
            for u in range(chunk):
                block_step(done + u)
        done = done + jnp.where(i % (2 * chunk) >= chunk, chunk, 0)
        chunk //= 2

    for h in heads:
        s = diag_fn(h, s_buf[h])
        values(h, i, softmax(h, s, col_max(s)))
    return [acc_buf[h] / stat(h, ROW_L) for h in heads]


def _store_token_major(o_ref, outs):
    for p in range(len(outs) // 2):
        pair = jnp.concatenate([outs[2 * p], outs[2 * p + 1]], axis=0)
        o_ref[0, :, p * LANES:(p + 1) * LANES] = pair.T.astype(o_ref.dtype)


def _flash_scratch(nh, t, dv):
    return [pltpu.VMEM((nh, t, t), F32), pltpu.VMEM((nh, dv, t), F32), pltpu.VMEM((nh, 8, t), F32)]


def _causal_mask_t(t):
    key = lax.broadcasted_iota(jnp.int32, (t, t), 0)
    qry = lax.broadcasted_iota(jnp.int32, (t, t), 1)
    return key <= qry


def _mla_kernel(qt_ref, k_ref, vt_ref, o_ref, *flash_scratch):
    i = pl.program_id(2)
    t = MLA_T
    nh = MLA_NH
    qs = [qt_ref[0, h * LANES:h * LANES + MLA_QK, :] for h in range(nh)]

    def scores(j, h):
        start = pl.multiple_of(j * t, t)
        return _dot(k_ref[0, pl.ds(start, t), h * LANES:h * LANES + MLA_QK], qs[h])

    def values(j, h):
        return vt_ref[0, h * MLA_V:(h + 1) * MLA_V, pl.ds(pl.multiple_of(j * t, t), t)]

    causal = _causal_mask_t(t)
    outs = _flash_heads(nh, i, t, MLA_V, scores, values,
                        lambda h, s: jnp.where(causal, s, NEG_INF),
                        *flash_scratch)
    _store_token_major(o_ref, outs)


def _mla(mqt, mk, mvt):
    batch, seq, _ = mk.shape
    t = MLA_T
    nh = MLA_NH
    return pl.pallas_call(
        _mla_kernel,
        grid=(batch, MLA_HEADS // nh, seq // t),
        in_specs=[pl.BlockSpec((1, nh * LANES, t), lambda b, g, i: (b, g, i)),
                  pl.BlockSpec((1, seq, nh * LANES), lambda b, g, i: (b, 0, g)),
                  pl.BlockSpec((1, nh * MLA_V, seq), lambda b, g, i: (b, g, 0))],
        out_specs=pl.BlockSpec((1, t, nh * MLA_V), lambda b, g, i: (b, i, g)),
        out_shape=jax.ShapeDtypeStruct((batch, seq, MLA_OUT), BF16),
        scratch_shapes=_flash_scratch(nh, t, MLA_V),
        compiler_params=_cparams(("parallel", "parallel", "arbitrary")),
        name="mla_attn",
    )(mqt, mk, mvt)


def _moba_kernel(qt_ref, k_ref, vt_ref, o_ref, kmh_ref, kml_ref, *flash_scratch, nblk):
    i = pl.program_id(2)
    t = MOBA_BLOCK
    nh = MOBA_NH

    @pl.when(i == 0)
    def _():
        kf = k_ref[0].astype(F32).reshape(nblk, t, nh * LANES)
        km = jnp.sum(kf, axis=1) * (1.0 / t)
        hi = km.astype(BF16)
        kmh_ref[...] = hi
        kml_ref[...] = (km - hi.astype(F32)).astype(BF16)

    blk = lax.broadcasted_iota(jnp.int32, (nblk, 1), 0)
    blk_f = blk.astype(F32)
    pos_rows =lax.broadcasted_iota(jnp.int32, (AUX_SEL - AUX_POS, 1), 0) < AUX_POS_PARTS
    ones_rows = jnp.broadcast_to(jnp.where(pos_rows, 1.0, 0.0), (AUX_SEL - AUX_POS, t)).astype(BF16)
    pad_q = jnp.zeros((LANES - HEAD_DIM, t), BF16)
    qs = []
    for h in range(nh):
        q = qt_ref[0, h * HEAD_DIM:(h + 1) * HEAD_DIM, :]
        q_plain = jnp.concatenate([q, pad_q], axis=0)
        cols = slice(h * LANES, (h + 1) * LANES)
        gate = _dot(kmh_ref[:, cols], q_plain) + _dot(kml_ref[:, cols], q_plain)
        gate = jnp.where(blk < i, gate, NEG_INF)
        chosen = blk == i
        for _ in range(min(MOBA_TOPK, nblk)):
            best = jnp.max(gate, axis=0, keepdims=True)
            first = jnp.min(jnp.where(gate == best, blk_f, float(nblk)), axis=0, keepdims=True)
            pick = blk_f == first
            chosen = chosen | (pick & (blk < i))
            gate = jnp.where(pick, -jnp.inf, gate)
        sel = jnp.where(chosen, 0.0, NEG_INF).astype(BF16)
        qs.append(jnp.concatenate([q, ones_rows, sel], axis=0))

    def scores(j, h):
        start = pl.multiple_of(j * t, t)
        return _dot(k_ref[0, pl.ds(start, t), h * LANES:h * LANES + AUX_SEL + nblk], qs[h])

    def values(j, h):
        return vt_ref[0, h * HEAD_DIM:(h + 1) * HEAD_DIM, pl.ds(pl.multiple_of(j * t, t), t)]

    causal = _causal_mask_t(t)
    outs = _flash_heads(nh, i, t, HEAD_DIM, scores, values,
                        lambda h, s: jnp.where(causal, s, NEG_INF),
                        *flash_scratch)
    _store_token_major(o_ref, outs)


def _moba(bqt, bk, bvt):
    batch, seq, _ = bk.shape
    t = MOBA_BLOCK
    nh = MOBA_NH
    nblk = seq // t
    assert AUX_SEL + nblk <= LANES
    cols = nh * HEAD_DIM
    return pl.pallas_call(
        functools.partial(_moba_kernel, nblk=nblk),
        grid=(batch, MOBA_HEADS // nh, nblk),
        in_specs=[
            pl.BlockSpec((1, cols, t), lambda b, g, i: (b, g, i)),
            pl.BlockSpec((1, seq, nh * LANES), lambda b, g, i: (b, 0, g)),
            pl.BlockSpec((1, cols, seq), lambda b, g, i: (b, g, 0)),
        ],
        out_specs=pl.BlockSpec((1, t, cols), lambda b, g, i: (b, i, g)),
        out_shape=jax.ShapeDtypeStruct((batch, seq, MOBA_OUT), BF16),
        scratch_shapes=[
            pltpu.VMEM((nblk, nh * LANES), BF16),
            pltpu.VMEM((nblk, nh * LANES), BF16),
        ] + _flash_scratch(nh, t, HEAD_DIM),
        compiler_params=_cparams(("parallel", "parallel", "arbitrary")),
        name="moba_attn",
    )(bqt, bk, bvt)


def _dil_kernel(own_ref, prev_ref, bias_ref, o_ref, lse_ref, *, ns, qb):
    n = pl.program_id(2)
    L = DIL_L
    low = lax.broadcasted_iota(jnp.int32, (1, LANES), 1) < HEAD_DIM
    first_cols = lax.broadcasted_iota(jnp.int32, (1, 2 * L), 1) < L
    problems = [(si, pair, blk) for si in range(ns) for pair in range(DIL_OUT // LANES)
                for blk in range(qb // L)]

    def window(si, blk, cols):
        if blk == 0:
            return jnp.concatenate([prev_ref[0, si, :, cols], own_ref[0, si, :L, cols]], axis=0)
        return own_ref[0, si, (blk - 1) * L:(blk + 1) * L, cols]

    scores = []
    for si, pair, blk in problems:
        qp = own_ref[0, si, blk * L:(blk + 1) * L, pair * LANES:(pair + 1) * LANES]
        zero = jnp.zeros_like(qp)
        keys = window(si, blk, slice(DIL_OUT + pair * LANES, DIL_OUT + (pair + 1) * LANES))
        scores.append([_dot_nt(jnp.where(low, qp, zero), keys),
                       _dot_nt(jnp.where(low, zero, qp), keys)])

    probs = []
    for (si, pair, blk), sc in zip(problems, scores):
        per_head = []
        for hh in range(2):
            s = sc[hh] + bias_ref[2 * pair + hh]
            if blk == 0:
                s = jnp.where(first_cols & (n == 0), NEG_INF, s)
            m = jnp.max(s, axis=-1, keepdims=True)
            e = jnp.exp(s - m)
            den = jnp.sum(e, axis=-1, keepdims=True)
            per_head.append((e.astype(BF16), den, m + jnp.log(den)))
        probs.append(per_head)

    for (si, pair, blk), per_head in zip(problems, probs):
        vals = window(si, blk, slice(2 * DIL_OUT + pair * LANES, 2 * DIL_OUT + (pair + 1) * LANES))
        o_pair = [_dot(e, vals) / den for e, den, _ in per_head]
        lse_pair = [jnp.broadcast_to(lse, (L, LANES)) for _, _, lse in per_head]
        dst = (0, si, slice(blk * L, (blk + 1) * L), slice(pair * LANES, (pair + 1) * LANES))
        o_ref[dst] = jnp.where(low, o_pair[0], o_pair[1])
        lse_ref[dst] = jnp.where(low, lse_pair[0], lse_pair[1])


def _dil(dg, bias):
    batch, dilation, rows, _ = dg.shape
    L = DIL_L
    qb = min(DIL_STEP_ROWS, rows)
    ns = min(DIL_STEP_ROWS // qb, dilation)
    per = qb // L
    out_sds = jax.ShapeDtypeStruct((batch, dilation, rows, DIL_OUT), F32)
    return pl.pallas_call(
        functools.partial(_dil_kernel, ns=ns, qb=qb),
        grid=(batch, dilation // ns, rows // qb),
        in_specs=[
            pl.BlockSpec((1, ns, qb, DIL_COLS), lambda b, r, n: (b, r, n, 0)),
            pl.BlockSpec((1, ns, L, DIL_COLS),
                         lambda b, r, n: (b, r, jnp.maximum(n * per - 1, 0), 0)),
            _resident(bias.shape),
        ],
        out_specs=[pl.BlockSpec((1, ns, qb, DIL_OUT), lambda b, r, n: (b, r, n, 0))] * 2,
        out_shape=[out_sds, out_sds],
        compiler_params=_cparams(("parallel", "parallel", "arbitrary")),
        name="dil_attn_d%d" % dilation,
    )(dg, dg, bias)


def _merge_kernel(x_ref, gn_ref, wgate_ref, a_ref, b_ref, o0_ref, o1_ref, o2_ref,
                  l0_ref, l1_ref, l2_ref, wa_ref, wb_ref, wc_ref, wo_ref, out_ref, *scratch):
    x = x_ref[0]
    h = _rms(x, gn_ref[...]).astype(BF16)
    tm = x.shape[0]
    scratch = list(scratch)

    def token_order(ref):
        dil = ref.shape[1]
        if dil == 1:
            return ref[0, 0]
        buf = scratch.pop()
        for c in range(buf.shape[0]):
            for r in range(dil):
                buf[c, pl.ds(r, tm // dil, stride=dil), :] = ref[0, r, :, c * LANES:(c + 1) * LANES]
        return jnp.concatenate([buf[c] for c in range(buf.shape[0])], axis=1)

    lses = [token_order(r) for r in (l0_ref, l1_ref, l2_ref)]
    outs = [token_order(r) for r in (o0_ref, o1_ref, o2_ref)]
    mx = jnp.maximum(jnp.maximum(lses[0], lses[1]), lses[2])
    es = [jnp.exp(l - mx) for l in lses]
    den = es[0] + es[1] + es[2]
    c = (es[0] / den) * outs[0] + (es[1] / den) * outs[1] + (es[2] / den) * outs[2]

    branches = ((a_ref[0], wa_ref), (b_ref[0], wb_ref), (c.astype(BF16), wc_ref))
    merged = jnp.zeros(x.shape, F32)
    for k, (act, w_ref) in enumerate(branches):
        gate = jax.nn.sigmoid(_dot(h, wgate_ref[:, k * D_MODEL:(k + 1) * D_MODEL]))
        merged = merged + gate * _dot(act, w_ref[...])
    out_ref[0] = x + _dot(merged.astype(BF16), wo_ref[...])


def _merge(x3, gain, wgate, a, b, dil_outs, wa, wb, wc, wo):
    batch, seq, _ = x3.shape
    tm = TM_MERGE
    tok = lambda c: pl.BlockSpec((1, tm, c), lambda b, i: (b, i, 0))
    stream = lambda arr: pl.BlockSpec((1, arr.shape[1], tm // arr.shape[1], DIL_OUT),
                                      lambda b, i: (b, 0, i, 0))
    (o0, l0), (o1, l1), (o2, l2) = dil_outs
    streams = (o0, o1, o2, l0, l1, l2)
    n_reordered = sum(arr.shape[1] > 1 for arr in streams)
    return pl.pallas_call(
        _merge_kernel,
        grid=(batch, seq // tm),
        in_specs=[tok(D_MODEL), _resident((1, D_MODEL)), _resident(wgate.shape),
                  tok(MLA_OUT), tok(MOBA_OUT)]
        + [stream(arr) for arr in streams]
        + [_resident(wa.shape), _resident(wb.shape), _resident(wc.shape), _resident(wo.shape)],
        out_specs=tok(D_MODEL),
        out_shape=jax.ShapeDtypeStruct((batch, seq, D_MODEL), F32),
        scratch_shapes=[pltpu.VMEM((DIL_OUT // LANES, tm, LANES), F32)] * n_reordered,
        compiler_params=_cparams(("parallel", "parallel")),
        name="merge",
    )(x3, gain, wgate, a, b, *streams, wa, wb, wc, wo)


def _alibi_slopes():
    return (2.0 ** (-8.0 * np.arange(1, N_ALIBI + 1, dtype=np.float32) / N_ALIBI)).astype(np.float32)


def _rope_tables(seq):
    pos = jnp.arange(seq, dtype=F32)
    inv = ROPE_THETA ** (-jnp.arange(0, MLA_ROPE, 2, dtype=F32) / MLA_ROPE)
    ang = pos[:, None] * inv[None, :]
    cos, sin = jnp.cos(ang), jnp.sin(ang)
    pad = jnp.zeros((seq, LANES - MLA_QK), F32)
    cos_t = jnp.concatenate([jnp.ones((seq, MLA_NOPE), F32), cos, cos, pad], axis=1)
    sin_t = jnp.concatenate([jnp.zeros((seq, MLA_NOPE), F32), -sin, sin, pad], axis=1)
    return cos_t, sin_t, cos_t.T, sin_t.T


def _dil_bias(group, dilation):
    L = DIL_L
    slopes = _alibi_slopes()[group * DIL_GROUP_HEADS:(group + 1) * DIL_GROUP_HEADS]
    steps = L + np.arange(L)[:, None] - np.arange(2 * L)[None, :]
    valid = (steps >= 0) & (steps <= L)
    dist = (steps * dilation).astype(np.float32)
    bias = np.where(valid[None], -slopes[:, None, None] * dist[None], np.float32(NEG_INF))
    return jnp.asarray(bias.astype(np.float32))


def _moba_key_aux(seq):
    slopes = _alibi_slopes()[DIL_HEADS:].astype(np.float64) * LOG2E
    pos = jnp.asarray((slopes[None, :] * np.arange(seq)[:, None]).astype(np.float32))
    aux = jnp.zeros((seq, MOBA_HEADS, LANES), F32)
    rest = pos
    for part in range(AUX_POS_PARTS):
        piece = rest.astype(BF16).astype(F32)
        aux = aux.at[:, :, AUX_POS + part].set(piece)
        rest = rest - piece
    onehot = jax.nn.one_hot(jnp.arange(seq) // MOBA_BLOCK, seq // MOBA_BLOCK, dtype=F32)
    aux = aux.at[:, :, AUX_SEL:AUX_SEL + seq // MOBA_BLOCK].set(onehot[:, None, :])
    return aux.reshape(seq, MOBA_KPAD).astype(BF16)


def _layer_weights(w_in, w_uq, w_ukv, q_norm, kv_norm):
    d = w_in.shape[0]
    z = lambda c: jnp.zeros((d, c), w_in.dtype)
    i0 = MLA_Q_RANK + MLA_KV_RANK
    kr = w_in[:, i0:i0 + MLA_ROPE]
    hr = MLA_ROPE // 2
    kr_pad = jnp.concatenate([z(MLA_NOPE), kr, z(LANES - MLA_QK)], axis=1)
    kr_swap = jnp.concatenate([z(MLA_NOPE), kr[:, hr:], kr[:, :hr], z(LANES - MLA_QK)], axis=1)
    moba = w_in[:, MLA_IN:MLA_IN + MOBA_IN]
    bq, bk, bv = (moba[:, c * MOBA_OUT:(c + 1) * MOBA_OUT] for c in range(3))
    bk = jnp.concatenate([bk.reshape(d, MOBA_HEADS, HEAD_DIM),
                          jnp.zeros((d, MOBA_HEADS, LANES - HEAD_DIM), w_in.dtype)],
                         axis=2).reshape(d, MOBA_KPAD)
    dil = w_in[:, MLA_IN + MOBA_IN:MLA_IN + MOBA_IN + DIL_IN].reshape(d, 3, DIL_HEADS, HEAD_DIM)
    groups = [dil[:, :, g * DIL_GROUP_HEADS:(g + 1) * DIL_GROUP_HEADS].reshape(d, DIL_COLS)
              for g in range(len(DIL_PATTERNS))]
    w1 = jnp.concatenate([w_in[:, :i0], kr_pad, kr_swap, bk] + groups, axis=1).astype(BF16)
    wt = jnp.concatenate([bq, bv], axis=1).T.astype(BF16)
    wgate = w_in[:, MLA_IN + MOBA_IN + DIL_IN:].astype(BF16)

    r = w_uq.shape[0]
    uq = w_uq.reshape(r, MLA_HEADS, MLA_QK)
    zq = jnp.zeros((r, MLA_HEADS, LANES - MLA_QK), w_uq.dtype)
    zn = jnp.zeros((r, MLA_HEADS, MLA_NOPE), w_uq.dtype)
    uq_pad = jnp.concatenate([uq, zq], axis=2).reshape(r, MLA_QPAD)
    uq_swap = jnp.concatenate(
        [zn, uq[:, :, MLA_NOPE + hr:], uq[:, :, MLA_NOPE:MLA_NOPE + hr], zq], axis=2
    ).reshape(r, MLA_QPAD)
    wuqt = jnp.concatenate([uq_pad, uq_swap], axis=1).T.astype(BF16)

    rk = w_ukv.shape[0]
    ukv = w_ukv.reshape(rk, MLA_HEADS, MLA_NOPE + MLA_V)
    zk = jnp.zeros((rk, MLA_HEADS, LANES - MLA_NOPE), w_ukv.dtype)
    wk = jnp.concatenate([ukv[:, :, :MLA_NOPE], zk], axis=2).reshape(rk, MLA_QPAD).astype(BF16)
    wvt = ukv[:, :, MLA_NOPE:].reshape(rk, MLA_OUT).T.astype(BF16)
    return dict(w1=w1, wt=wt, wgate=wgate, wuqt=wuqt, wk=wk, wvt=wvt,
                qn=q_norm.reshape(1, -1), kvn=kv_norm.reshape(1, -1))


def kernel(x, ffn1_norm, ffn1_w_gate, ffn1_w_up, ffn1_w_down, mix_norm, w_in, q_norm, w_uq, kv_norm, w_ukv, w_br_mla, w_br_moba, w_br_dil, w_out, ffn2_norm, ffn2_w_gate, ffn2_w_up, ffn2_w_down, final_norm):
    batch, seq, d = x.shape
    assert d == D_MODEL and seq % (DIL_PATTERNS[-1][1] * DIL_L) == 0 and seq % TM_PROJ == 0
    n = batch * seq
    x2 = x.reshape(n, d)
    tables = _rope_tables(seq) + (_moba_key_aux(seq),)
    dil_bias = [_dil_bias(g, dil) for g, (_, dil) in enumerate(DIL_PATTERNS)]
    final_gain = final_norm.reshape(1, d)
    bf = lambda w: w.astype(BF16)
    row = lambda v: v.reshape(1, -1)

    for l in range(DEPTH):
        x2 = _ffn(x2, row(ffn1_norm[l]), bf(ffn1_w_gate[l]), bf(ffn1_w_up[l]), bf(ffn1_w_down[l]),
                  final_gain, False)
        lw = _layer_weights(w_in[l], w_uq[l], w_ukv[l], q_norm[l], kv_norm[l])
        mqt, mk, mvt, bqt, bk, bvt, d0, d1, d2 = _inproj(
            x2.reshape(batch, seq, d), row(mix_norm[l]), lw, tables)
        a = _mla(mqt, mk, mvt)
        b = _moba(bqt, bk, bvt)
        dil_outs = [_dil(dg, dil_bias[g]) for g, dg in enumerate((d0, d1, d2))]
        x2 = _merge(x2.reshape(batch, seq, d), row(mix_norm[l]), lw["wgate"], a, b, dil_outs,
                    bf(w_br_mla[l]), bf(w_br_moba[l]), bf(w_br_dil[l]), bf(w_out[l])).reshape(n, d)
        x2 = _ffn(x2, row(ffn2_norm[l]), bf(ffn2_w_gate[l]), bf(ffn2_w_up[l]), bf(ffn2_w_down[l]),
                  final_gain, l == DEPTH - 1)
    return x2.reshape(batch, seq, d)
```

```python
import functools

import numpy as np
import jax
import jax.numpy as jnp
from jax import lax
from jax.experimental import pallas as pl
from jax.experimental.pallas import tpu as pltpu

F32 = jnp.float32
BF16 = jnp.bfloat16

D_MODEL = 1024
DEPTH = 4
HEAD_DIM = 64
MLA_HEADS = 6
MLA_NOPE = 64
MLA_ROPE = 32
MLA_V = 64
MLA_Q_RANK = 256
MLA_KV_RANK = 128
ROPE_THETA = 10000.0
MOBA_HEADS = 6
MOBA_BLOCK = 256
MOBA_TOPK = 3
DIL_PATTERNS = ((128, 1), (512, 4), (2048, 16))
DIL_GROUP_HEADS = 4
DIL_HEADS = DIL_GROUP_HEADS * len(DIL_PATTERNS)
N_ALIBI = DIL_HEADS + MOBA_HEADS
N_BRANCHES = 3
D_FF = 2816
NORM_EPS = 1e-6
NEG_INF = -1e30
MLA_IN = MLA_Q_RANK + MLA_KV_RANK + MLA_ROPE
MOBA_IN = 3 * MOBA_HEADS * HEAD_DIM
DIL_IN = 3 * DIL_HEADS * HEAD_DIM
MLA_OUT = MLA_HEADS * MLA_V
MOBA_OUT = MOBA_HEADS * HEAD_DIM
DIL_OUT = DIL_GROUP_HEADS * HEAD_DIM

LANES = 128
MLA_QK = MLA_NOPE + MLA_ROPE
MLA_QPAD = MLA_HEADS * LANES
LOG2E = 1.4426950408889634
MLA_SCALE = MLA_QK ** -0.5
HEAD_SCALE = HEAD_DIM ** -0.5
DIL_L = DIL_PATTERNS[0][0] // DIL_PATTERNS[0][1]
DIL_COLS = 3 * DIL_GROUP_HEADS * HEAD_DIM

C_CQ = 0
C_CKV = C_CQ + MLA_Q_RANK
C_KR = C_CKV + MLA_KV_RANK
C_KR2 = C_KR + LANES
C_MOBA_K = C_KR2 + LANES
MOBA_KPAD = MOBA_HEADS * LANES
C_DIL = C_MOBA_K + MOBA_KPAD
C_END = C_DIL + DIL_IN

AUX_POS = HEAD_DIM
AUX_POS_PARTS = 3
AUX_SEL = HEAD_DIM + 16

VMEM_LIMIT = 56 * 1024 * 1024

TM_FFN = 512
TM_PROJ = 512
TM_MERGE = 512
FF_CHUNK = 256
MLA_T = 256
MLA_NH = 6
MOBA_NH = 6
DIL_STEP_ROWS = 1024


def _cparams(sem):
    return pltpu.CompilerParams(dimension_semantics=sem, vmem_limit_bytes=VMEM_LIMIT)


def _resident(shape):
    nd = len(shape)
    return pl.BlockSpec(shape, lambda *_: (0,) * nd, pipeline_mode=pl.Buffered(1))


def _rms(x, gain):
    ms = jnp.mean(x * x, axis=-1, keepdims=True)
    return x * lax.rsqrt(ms + NORM_EPS) * gain


def _dot(a, b):
    return jnp.dot(a, b, preferred_element_type=F32)


def _dot_nt(a, b):
    return lax.dot_general(a, b, (((1,), (1,)), ((), ())), preferred_element_type=F32)


def _dot_tn(a, b):
    return lax.dot_general(a, b, (((0,), (0,)), ((), ())), preferred_element_type=F32)


def _ffn_kernel(x_ref, g_ref, wg_ref, wu_ref, wd_ref, fg_ref, o_ref, *, final):
    x = x_ref[...]
    h = _rms(x, g_ref[...]).astype(BF16)
    acc = jnp.zeros(x.shape, F32)
    for c in range(D_FF // FF_CHUNK):
        sl = slice(c * FF_CHUNK, (c + 1) * FF_CHUNK)
        a = _dot(h, wg_ref[:, sl])
        u = _dot(h, wu_ref[:, sl])
        g = (a * jax.nn.sigmoid(a) * u).astype(BF16)
        acc = acc + _dot(g, wd_ref[sl, :])
    y = x + 0.5 * acc
    if final:
        y = _rms(y, fg_ref[...])
    o_ref[...] = y


def _ffn(x2, gain, wg, wu, wd, final_gain, final):
    n = x2.shape[0]
    return pl.pallas_call(
        functools.partial(_ffn_kernel, final=final),
        grid=(n // TM_FFN,),
        in_specs=[
            pl.BlockSpec((TM_FFN, D_MODEL), lambda i: (i, 0)),
            _resident((1, D_MODEL)),
            _resident((D_MODEL, D_FF)),
            _resident((D_MODEL, D_FF)),
            _resident((D_FF, D_MODEL)),
            _resident((1, D_MODEL)),
        ],
        out_specs=pl.BlockSpec((TM_FFN, D_MODEL), lambda i: (i, 0)),
        out_shape=jax.ShapeDtypeStruct((n, D_MODEL), F32),
        compiler_params=_cparams(("parallel",)),
        name="ffn_final" if final else "ffn",
    )(x2, gain, wg, wu, wd, final_gain)


def _inproj_kernel(x_ref, gn_ref, w1_ref, wt_ref, qn_ref, wuqt_ref, kvn_ref, wk_ref, wvt_ref,
                   cos_ref, sin_ref, cost_ref, sint_ref, kaux_ref,
                   mqt_ref, mk_ref, mvt_ref, bqt_ref, bk_ref, bvt_ref, d0_ref, d1_ref, d2_ref,
                   stage_ref):
    x = x_ref[0]
    h = _rms(x, gn_ref[...]).astype(BF16)

    pm = _dot(h, w1_ref[:, C_CQ:C_MOBA_K])
    cq = _rms(pm[:, C_CQ:C_CKV], qn_ref[...]).astype(BF16)
    ckv = _rms(pm[:, C_CKV:C_KR], kvn_ref[...]).astype(BF16)
    k_rope = pm[:, C_KR:C_KR2] * cos_ref[...] + pm[:, C_KR2:C_MOBA_K] * sin_ref[...]
    kk = _dot(ckv, wk_ref[...])
    qqt = _dot_nt(wuqt_ref[...], cq)
    cost = cost_ref[...]
    sint = sint_ref[...]
    for hh in range(MLA_HEADS):
        sl = slice(hh * LANES, (hh + 1) * LANES)
        sl2 = slice(MLA_QPAD + hh * LANES, MLA_QPAD + (hh + 1) * LANES)
        mqt_ref[0, sl, :] = ((qqt[sl] * cost + qqt[sl2] * sint) * (MLA_SCALE * LOG2E)).astype(BF16)
        mk_ref[0, :, sl] = (kk[:, sl] + k_rope).astype(BF16)
    mvt_ref[0] = _dot_nt(wvt_ref[...], ckv).astype(BF16)

    bt = _dot_nt(wt_ref[...], h)
    bqt_ref[0] = (bt[:MOBA_OUT] * (HEAD_SCALE * LOG2E)).astype(BF16)
    bvt_ref[0] = bt[MOBA_OUT:].astype(BF16)
    bk_ref[0] = (_dot(h, w1_ref[:, C_MOBA_K:C_DIL]) + kaux_ref[...].astype(F32)).astype(BF16)

    tm = x.shape[0]
    for g, d_ref in enumerate((d0_ref, d1_ref, d2_ref)):
        dil = DIL_PATTERNS[g][1]
        dd = _dot(h, w1_ref[:, C_DIL + g * DIL_COLS:C_DIL + (g + 1) * DIL_COLS])
        for c in range(DIL_COLS // LANES):
            cols = slice(c * LANES, (c + 1) * LANES)
            blk = dd[:, cols]
            stage_ref[c] = blk * HEAD_SCALE if c < DIL_OUT // LANES else blk
            for r in range(dil):
                d_ref[0, r, :, cols] = stage_ref[c, pl.ds(r, tm // dil, stride=dil), :].astype(BF16)


def _inproj(x3, gain, lw, tables):
    batch, seq, _ = x3.shape
    tm = TM_PROJ
    cos_t, sin_t, cos_tt, sin_tt, kaux = tables
    tok = lambda c: pl.BlockSpec((1, tm, c), lambda b, i: (b, i, 0))
    feat = lambda c: pl.BlockSpec((1, c, tm), lambda b, i: (b, 0, i))
    tok_sds = lambda c: jax.ShapeDtypeStruct((batch, seq, c), BF16)
    feat_sds = lambda c: jax.ShapeDtypeStruct((batch, c, seq), BF16)
    dils = [dil for _, dil in DIL_PATTERNS]
    stream = lambda dil: pl.BlockSpec((1, dil, tm // dil, DIL_COLS), lambda b, i: (b, 0, i, 0))
    stream_sds = lambda dil: jax.ShapeDtypeStruct((batch, dil, seq // dil, DIL_COLS), BF16)
    return pl.pallas_call(
        _inproj_kernel,
        grid=(batch, seq // tm),
        in_specs=[
            tok(D_MODEL),
            _resident((1, D_MODEL)),
            _resident(lw["w1"].shape),
            _resident(lw["wt"].shape),
            _resident((1, MLA_Q_RANK)),
            _resident(lw["wuqt"].shape),
            _resident((1, MLA_KV_RANK)),
            _resident(lw["wk"].shape),
            _resident(lw["wvt"].shape),
            pl.BlockSpec((tm, LANES), lambda b, i: (i, 0)),
            pl.BlockSpec((tm, LANES), lambda b, i: (i, 0)),
            pl.BlockSpec((LANES, tm), lambda b, i: (0, i)),
            pl.BlockSpec((LANES, tm), lambda b, i: (0, i)),
            pl.BlockSpec((tm, MOBA_KPAD), lambda b, i: (i, 0)),
        ],
        out_specs=[feat(MLA_QPAD), tok(MLA_QPAD), feat(MLA_OUT),
                   feat(MOBA_OUT), tok(MOBA_KPAD), feat(MOBA_OUT)] + [stream(dil) for dil in dils],
        out_shape=[feat_sds(MLA_QPAD), tok_sds(MLA_QPAD), feat_sds(MLA_OUT),
                   feat_sds(MOBA_OUT), tok_sds(MOBA_KPAD), feat_sds(MOBA_OUT)]
        + [stream_sds(dil) for dil in dils],
        scratch_shapes=[pltpu.VMEM((DIL_COLS // LANES, tm, LANES), F32)],
        compiler_params=_cparams(("parallel", "parallel")),
        name="inproj",
    )(x3, gain, lw["w1"], lw["wt"], lw["qn"], lw["wuqt"], lw["kvn"], lw["wk"], lw["wvt"],
      cos_t, sin_t, cos_tt, sin_tt, kaux)


ROW_M, ROW_ALPHA, ROW_L, ROW_SMAX = range(4)
FLASH_UNROLL = 8
SUM_ROWS = 16


def _flash_heads(nh, i, t, dv, score_fn, value_fn, diag_fn, s_buf, acc_buf, stat_buf):
    heads = range(nh)
    ones = jnp.ones((SUM_ROWS, t), BF16)
    col_max = lambda s: jnp.max(s, axis=0, keepdims=True)
    stat = lambda h, r: stat_buf[h, r:r + 1, :]

    def set_stat(h, r, v):
        stat_buf[h, r:r + 1, :] = v

    def values(h, j, p):
        pv = _dot(jnp.concatenate([value_fn(j, h), ones], axis=0), p)
        alpha = stat(h, ROW_ALPHA)
        acc_buf[h] = alpha * acc_buf[h] + pv[:dv]
        set_stat(h, ROW_L, alpha * stat(h, ROW_L) + pv[dv:dv + 1])

    def softmax(h, s, s_max):
        m = stat(h, ROW_M)
        m_new = jnp.maximum(m, s_max)
        alpha = jnp.exp2(m - m_new)
        p = jnp.exp2(s - m_new)
        set_stat(h, ROW_M, m_new)
        set_stat(h, ROW_ALPHA, alpha)
        return p.astype(BF16)

    for h in heads:
        sc = score_fn(0, h)
        s_buf[h] = sc
        set_stat(h, ROW_SMAX, col_max(sc))
        set_stat(h, ROW_M, jnp.full((1, t), NEG_INF, F32))
        set_stat(h, ROW_ALPHA, jnp.ones((1, t), F32))
        set_stat(h, ROW_L, jnp.zeros((1, t), F32))
        acc_buf[h] = jnp.zeros((dv, t), F32)

    def block_step(j):
        for h in heads:
            sc = score_fn(j + 1, h)
            values(h, j, softmax(h, s_buf[h], stat(h, ROW_SMAX)))
            s_buf[h] = sc
            set_stat(h, ROW_SMAX, col_max(sc))

    @pl.loop(0, i // FLASH_UNROLL)
    def _(jj):
        for u in range(FLASH_UNROLL):
            block_step(FLASH_UNROLL * jj + u)

    done = i - i % FLASH_UNROLL
    chunk = FLASH_UNROLL // 2
    while chunk:
        @pl.when(i % (2 * chunk) >= chunk)
        def _(done=done, chunk=chunk):
            for u in range(chunk):
                block_step(done + u)
        done = done + jnp.where(i % (2 * chunk) >= chunk, chunk, 0)
        chunk //= 2

    for h in heads:
        s = diag_fn(h, s_buf[h])
        values(h, i, softmax(h, s, col_max(s)))
    return [acc_buf[h] / stat(h, ROW_L) for h in heads]


def _store_feature_major(o_ref, outs):
    dv = outs[0].shape[0]
    for h, out in enumerate(outs):
        o_ref[0, h * dv:(h + 1) * dv, :] = out.astype(o_ref.dtype)


def _flash_scratch(nh, t, dv):
    return [pltpu.VMEM((nh, t, t), F32), pltpu.VMEM((nh, dv, t), F32), pltpu.VMEM((nh, 8, t), F32)]


def _causal_mask_t(t):
    key = lax.broadcasted_iota(jnp.int32, (t, t), 0)
    qry = lax.broadcasted_iota(jnp.int32, (t, t), 1)
    return key <= qry


def _mla_kernel(qt_ref, k_ref, vt_ref, o_ref, *flash_scratch):
    i = pl.program_id(2)
    t = MLA_T
    nh = MLA_NH
    qs = [qt_ref[0, h * LANES:h * LANES + MLA_QK, :] for h in range(nh)]

    def scores(j, h):
        start = pl.multiple_of(j * t, t)
        return _dot(k_ref[0, pl.ds(start, t), h * LANES:h * LANES + MLA_QK], qs[h])

    def values(j, h):
        return vt_ref[0, h * MLA_V:(h + 1) * MLA_V, pl.ds(pl.multiple_of(j * t, t), t)]

    causal = _causal_mask_t(t)
    outs = _flash_heads(nh, i, t, MLA_V, scores, values,
                        lambda h, s: jnp.where(causal, s, NEG_INF),
                        *flash_scratch)
    _store_feature_major(o_ref, outs)


def _mla(mqt, mk, mvt):
    batch, seq, _ = mk.shape
    t = MLA_T
    nh = MLA_NH
    return pl.pallas_call(
        _mla_kernel,
        grid=(batch, MLA_HEADS // nh, seq // t),
        in_specs=[pl.BlockSpec((1, nh * LANES, t), lambda b, g, i: (b, g, i)),
                  pl.BlockSpec((1, seq, nh * LANES), lambda b, g, i: (b, 0, g)),
                  pl.BlockSpec((1, nh * MLA_V, seq), lambda b, g, i: (b, g, 0))],
        out_specs=pl.BlockSpec((1, nh * MLA_V, t), lambda b, g, i: (b, g, i)),
        out_shape=jax.ShapeDtypeStruct((batch, MLA_OUT, seq), BF16),
        scratch_shapes=_flash_scratch(nh, t, MLA_V),
        compiler_params=_cparams(("parallel", "parallel", "arbitrary")),
        name="mla_attn",
    )(mqt, mk, mvt)


def _moba_kernel(qt_ref, k_ref, vt_ref, o_ref, kmh_ref, kml_ref, *flash_scratch, nblk):
    i = pl.program_id(2)
    t = MOBA_BLOCK
    nh = MOBA_NH

    @pl.when(i == 0)
    def _():
        kf = k_ref[0].astype(F32).reshape(nblk, t, nh * LANES)
        km = jnp.sum(kf, axis=1) * (1.0 / t)
        hi = km.astype(BF16)
        kmh_ref[...] = hi
        kml_ref[...] = (km - hi.astype(F32)).astype(BF16)

    blk = lax.broadcasted_iota(jnp.int32, (nblk, 1), 0)
    blk_f = blk.astype(F32)
    pos_rows =lax.broadcasted_iota(jnp.int32, (AUX_SEL - AUX_POS, 1), 0) < AUX_POS_PARTS
    ones_rows = jnp.broadcast_to(jnp.where(pos_rows, 1.0, 0.0), (AUX_SEL - AUX_POS, t)).astype(BF16)
    pad_q = jnp.zeros((LANES - HEAD_DIM, t), BF16)
    qs = []
    for h in range(nh):
        q = qt_ref[0, h * HEAD_DIM:(h + 1) * HEAD_DIM, :]
        q_plain = jnp.concatenate([q, pad_q], axis=0)
        cols = slice(h * LANES, (h + 1) * LANES)
        gate = _dot(kmh_ref[:, cols], q_plain) + _dot(kml_ref[:, cols], q_plain)
        gate = jnp.where(blk < i, gate, NEG_INF)
        chosen = blk == i
        for _ in range(min(MOBA_TOPK, nblk)):
            best = jnp.max(gate, axis=0, keepdims=True)
            first = jnp.min(jnp.where(gate == best, blk_f, float(nblk)), axis=0, keepdims=True)
            pick = blk_f == first
            chosen = chosen | (pick & (blk < i))
            gate = jnp.where(pick, -jnp.inf, gate)
        sel = jnp.where(chosen, 0.0, NEG_INF).astype(BF16)
        qs.append(jnp.concatenate([q, ones_rows, sel], axis=0))

    def scores(j, h):
        start = pl.multiple_of(j * t, t)
        return _dot(k_ref[0, pl.ds(start, t), h * LANES:h * LANES + AUX_SEL + nblk], qs[h])

    def values(j, h):
        return vt_ref[0, h * HEAD_DIM:(h + 1) * HEAD_DIM, pl.ds(pl.multiple_of(j * t, t), t)]

    causal = _causal_mask_t(t)
    outs = _flash_heads(nh, i, t, HEAD_DIM, scores, values,
                        lambda h, s: jnp.where(causal, s, NEG_INF),
                        *flash_scratch)
    _store_feature_major(o_ref, outs)


def _moba(bqt, bk, bvt):
    batch, seq, _ = bk.shape
    t = MOBA_BLOCK
    nh = MOBA_NH
    nblk = seq // t
    assert AUX_SEL + nblk <= LANES
    cols = nh * HEAD_DIM
    return pl.pallas_call(
        functools.partial(_moba_kernel, nblk=nblk),
        grid=(batch, MOBA_HEADS // nh, nblk),
        in_specs=[
            pl.BlockSpec((1, cols, t), lambda b, g, i: (b, g, i)),
            pl.BlockSpec((1, seq, nh * LANES), lambda b, g, i: (b, 0, g)),
            pl.BlockSpec((1, cols, seq), lambda b, g, i: (b, g, 0)),
        ],
        out_specs=pl.BlockSpec((1, cols, t), lambda b, g, i: (b, g, i)),
        out_shape=jax.ShapeDtypeStruct((batch, MOBA_OUT, seq), BF16),
        scratch_shapes=[
            pltpu.VMEM((nblk, nh * LANES), BF16),
            pltpu.VMEM((nblk, nh * LANES), BF16),
        ] + _flash_scratch(nh, t, HEAD_DIM),
        compiler_params=_cparams(("parallel", "parallel", "arbitrary")),
        name="moba_attn",
    )(bqt, bk, bvt)


def _dil_kernel(own_ref, prev_ref, bias_ref, o_ref, lse_ref, *, ns, qb):
    n = pl.program_id(2)
    L = DIL_L
    low = lax.broadcasted_iota(jnp.int32, (1, LANES), 1) < HEAD_DIM
    first_cols = lax.broadcasted_iota(jnp.int32, (1, 2 * L), 1) < L
    problems = [(si, pair, blk) for si in range(ns) for pair in range(DIL_OUT // LANES)
                for blk in range(qb // L)]

    def window(si, blk, cols):
        if blk == 0:
            return jnp.concatenate([prev_ref[0, si, :, cols], own_ref[0, si, :L, cols]], axis=0)
        return own_ref[0, si, (blk - 1) * L:(blk + 1) * L, cols]

    scores = []
    for si, pair, blk in problems:
        qp = own_ref[0, si, blk * L:(blk + 1) * L, pair * LANES:(pair + 1) * LANES]
        zero = jnp.zeros_like(qp)
        keys = window(si, blk, slice(DIL_OUT + pair * LANES, DIL_OUT + (pair + 1) * LANES))
        scores.append([_dot_nt(jnp.where(low, qp, zero), keys),
                       _dot_nt(jnp.where(low, zero, qp), keys)])

    probs = []
    for (si, pair, blk), sc in zip(problems, scores):
        per_head = []
        for hh in range(2):
            s = sc[hh] + bias_ref[2 * pair + hh]
            if blk == 0:
                s = jnp.where(first_cols & (n == 0), NEG_INF, s)
            m = jnp.max(s, axis=-1, keepdims=True)
            e = jnp.exp(s - m)
            den = jnp.sum(e, axis=-1, keepdims=True)
            per_head.append((e.astype(BF16), den, m + jnp.log(den)))
        probs.append(per_head)

    for (si, pair, blk), per_head in zip(problems, probs):
        vals = window(si, blk, slice(2 * DIL_OUT + pair * LANES, 2 * DIL_OUT + (pair + 1) * LANES))
        o_pair = [_dot(e, vals) / den for e, den, _ in per_head]
        lse_pair = [jnp.broadcast_to(lse, (L, LANES)) for _, _, lse in per_head]
        dst = (0, si, slice(blk * L, (blk + 1) * L), slice(pair * LANES, (pair + 1) * LANES))
        o_ref[dst] = jnp.where(low, o_pair[0], o_pair[1])
        lse_ref[dst] = jnp.where(low, lse_pair[0], lse_pair[1])


def _dil(dg, bias):
    batch, dilation, rows, _ = dg.shape
    L = DIL_L
    qb = min(DIL_STEP_ROWS, rows)
    ns = min(DIL_STEP_ROWS // qb, dilation)
    per = qb // L
    out_sds = jax.ShapeDtypeStruct((batch, dilation, rows, DIL_OUT), F32)
    return pl.pallas_call(
        functools.partial(_dil_kernel, ns=ns, qb=qb),
        grid=(batch, dilation // ns, rows // qb),
        in_specs=[
            pl.BlockSpec((1, ns, qb, DIL_COLS), lambda b, r, n: (b, r, n, 0)),
            pl.BlockSpec((1, ns, L, DIL_COLS),
                         lambda b, r, n: (b, r, jnp.maximum(n * per - 1, 0), 0)),
            _resident(bias.shape),
        ],
        out_specs=[pl.BlockSpec((1, ns, qb, DIL_OUT), lambda b, r, n: (b, r, n, 0))] * 2,
        out_shape=[out_sds, out_sds],
        compiler_params=_cparams(("parallel", "parallel", "arbitrary")),
        name="dil_attn_d%d" % dilation,
    )(dg, dg, bias)


def _merge_kernel(x_ref, gn_ref, wgate_ref, a_ref, b_ref, o0_ref, o1_ref, o2_ref,
                  l0_ref, l1_ref, l2_ref, wa_ref, wb_ref, wc_ref, wo_ref, out_ref, *scratch):
    x = x_ref[0]
    h = _rms(x, gn_ref[...]).astype(BF16)
    tm = x.shape[0]
    scratch = list(scratch)

    def token_order(ref):
        dil = ref.shape[1]
        if dil == 1:
            return ref[0, 0]
        buf = scratch.pop()
        for c in range(buf.shape[0]):
            for r in range(dil):
                buf[c, pl.ds(r, tm // dil, stride=dil), :] = ref[0, r, :, c * LANES:(c + 1) * LANES]
        return jnp.concatenate([buf[c] for c in range(buf.shape[0])], axis=1)

    lses = [token_order(r) for r in (l0_ref, l1_ref, l2_ref)]
    outs = [token_order(r) for r in (o0_ref, o1_ref, o2_ref)]
    mx = jnp.maximum(jnp.maximum(lses[0], lses[1]), lses[2])
    es = [jnp.exp(l - mx) for l in lses]
    den = es[0] + es[1] + es[2]
    c = (es[0] / den) * outs[0] + (es[1] / den) * outs[1] + (es[2] / den) * outs[2]

    projected = (_dot_tn(a_ref[0], wa_ref[...]), _dot_tn(b_ref[0], wb_ref[...]),
                 _dot(c.astype(BF16), wc_ref[...]))
    merged = jnp.zeros(x.shape, F32)
    for k, proj in enumerate(projected):
        gate = jax.nn.sigmoid(_dot(h, wgate_ref[:, k * D_MODEL:(k + 1) * D_MODEL]))
        merged = merged + gate * proj
    out_ref[0] = x + _dot(merged.astype(BF16), wo_ref[...])


def _merge(x3, gain, wgate, a, b, dil_outs, wa, wb, wc, wo):
    batch, seq, _ = x3.shape
    tm = TM_MERGE
    tok = lambda c: pl.BlockSpec((1, tm, c), lambda b, i: (b, i, 0))
    feat = lambda c: pl.BlockSpec((1, c, tm), lambda b, i: (b, 0, i))
    stream = lambda arr: pl.BlockSpec((1, arr.shape[1], tm // arr.shape[1], DIL_OUT),
                                      lambda b, i: (b, 0, i, 0))
    (o0, l0), (o1, l1), (o2, l2) = dil_outs
    streams = (o0, o1, o2, l0, l1, l2)
    n_reordered = sum(arr.shape[1] > 1 for arr in streams)
    return pl.pallas_call(
        _merge_kernel,
        grid=(batch, seq // tm),
        in_specs=[tok(D_MODEL), _resident((1, D_MODEL)), _resident(wgate.shape),
                  feat(MLA_OUT), feat(MOBA_OUT)]
        + [stream(arr) for arr in streams]
        + [_resident(wa.shape), _resident(wb.shape), _resident(wc.shape), _resident(wo.shape)],
        out_specs=tok(D_MODEL),
        out_shape=jax.ShapeDtypeStruct((batch, seq, D_MODEL), F32),
        scratch_shapes=[pltpu.VMEM((DIL_OUT // LANES, tm, LANES), F32)] * n_reordered,
        compiler_params=_cparams(("parallel", "parallel")),
        name="merge",
    )(x3, gain, wgate, a, b, *streams, wa, wb, wc, wo)


def _alibi_slopes():
    return (2.0 ** (-8.0 * np.arange(1, N_ALIBI + 1, dtype=np.float32) / N_ALIBI)).astype(np.float32)


def _rope_tables(seq):
    pos = jnp.arange(seq, dtype=F32)
    inv = ROPE_THETA ** (-jnp.arange(0, MLA_ROPE, 2, dtype=F32) / MLA_ROPE)
    ang = pos[:, None] * inv[None, :]
    cos, sin = jnp.cos(ang), jnp.sin(ang)
    pad = jnp.zeros((seq, LANES - MLA_QK), F32)
    cos_t = jnp.concatenate([jnp.ones((seq, MLA_NOPE), F32), cos, cos, pad], axis=1)
    sin_t = jnp.concatenate([jnp.zeros((seq, MLA_NOPE), F32), -sin, sin, pad], axis=1)
    return cos_t, sin_t, cos_t.T, sin_t.T


def _dil_bias(group, dilation):
    L = DIL_L
    slopes = _alibi_slopes()[group * DIL_GROUP_HEADS:(group + 1) * DIL_GROUP_HEADS]
    steps = L + np.arange(L)[:, None] - np.arange(2 * L)[None, :]
    valid = (steps >= 0) & (steps <= L)
    dist = (steps * dilation).astype(np.float32)
    bias = np.where(valid[None], -slopes[:, None, None] * dist[None], np.float32(NEG_INF))
    return jnp.asarray(bias.astype(np.float32))


def _moba_key_aux(seq):
    slopes = _alibi_slopes()[DIL_HEADS:].astype(np.float64) * LOG2E
    pos = jnp.asarray((slopes[None, :] * np.arange(seq)[:, None]).astype(np.float32))
    aux = jnp.zeros((seq, MOBA_HEADS, LANES), F32)
    rest = pos
    for part in range(AUX_POS_PARTS):
        piece = rest.astype(BF16).astype(F32)
        aux = aux.at[:, :, AUX_POS + part].set(piece)
        rest = rest - piece
    onehot = jax.nn.one_hot(jnp.arange(seq) // MOBA_BLOCK, seq // MOBA_BLOCK, dtype=F32)
    aux = aux.at[:, :, AUX_SEL:AUX_SEL + seq // MOBA_BLOCK].set(onehot[:, None, :])
    return aux.reshape(seq, MOBA_KPAD).astype(BF16)


def _layer_weights(w_in, w_uq, w_ukv, q_norm, kv_norm):
    d = w_in.shape[0]
    z = lambda c: jnp.zeros((d, c), w_in.dtype)
    i0 = MLA_Q_RANK + MLA_KV_RANK
    kr = w_in[:, i0:i0 + MLA_ROPE]
    hr = MLA_ROPE // 2
    kr_pad = jnp.concatenate([z(MLA_NOPE), kr, z(LANES - MLA_QK)], axis=1)
    kr_swap = jnp.concatenate([z(MLA_NOPE), kr[:, hr:], kr[:, :hr], z(LANES - MLA_QK)], axis=1)
    moba = w_in[:, MLA_IN:MLA_IN + MOBA_IN]
    bq, bk, bv = (moba[:, c * MOBA_OUT:(c + 1) * MOBA_OUT] for c in range(3))
    bk = jnp.concatenate([bk.reshape(d, MOBA_HEADS, HEAD_DIM),
                          jnp.zeros((d, MOBA_HEADS, LANES - HEAD_DIM), w_in.dtype)],
                         axis=2).reshape(d, MOBA_KPAD)
    dil = w_in[:, MLA_IN + MOBA_IN:MLA_IN + MOBA_IN + DIL_IN].reshape(d, 3, DIL_HEADS, HEAD_DIM)
    groups = [dil[:, :, g * DIL_GROUP_HEADS:(g + 1) * DIL_GROUP_HEADS].reshape(d, DIL_COLS)
              for g in range(len(DIL_PATTERNS))]
    w1 = jnp.concatenate([w_in[:, :i0], kr_pad, kr_swap, bk] + groups, axis=1).astype(BF16)
    wt = jnp.concatenate([bq, bv], axis=1).T.astype(BF16)
    wgate = w_in[:, MLA_IN + MOBA_IN + DIL_IN:].astype(BF16)

    r = w_uq.shape[0]
    uq = w_uq.reshape(r, MLA_HEADS, MLA_QK)
    zq = jnp.zeros((r, MLA_HEADS, LANES - MLA_QK), w_uq.dtype)
    zn = jnp.zeros((r, MLA_HEADS, MLA_NOPE), w_uq.dtype)
    uq_pad = jnp.concatenate([uq, zq], axis=2).reshape(r, MLA_QPAD)
    uq_swap = jnp.concatenate(
        [zn, uq[:, :, MLA_NOPE + hr:], uq[:, :, MLA_NOPE:MLA_NOPE + hr], zq], axis=2
    ).reshape(r, MLA_QPAD)
    wuqt = jnp.concatenate([uq_pad, uq_swap], axis=1).T.astype(BF16)

    rk = w_ukv.shape[0]
    ukv = w_ukv.reshape(rk, MLA_HEADS, MLA_NOPE + MLA_V)
    zk = jnp.zeros((rk, MLA_HEADS, LANES - MLA_NOPE), w_ukv.dtype)
    wk = jnp.concatenate([ukv[:, :, :MLA_NOPE], zk], axis=2).reshape(rk, MLA_QPAD).astype(BF16)
    wvt = ukv[:, :, MLA_NOPE:].reshape(rk, MLA_OUT).T.astype(BF16)
    return dict(w1=w1, wt=wt, wgate=wgate, wuqt=wuqt, wk=wk, wvt=wvt,
                qn=q_norm.reshape(1, -1), kvn=kv_norm.reshape(1, -1))


def kernel(x, ffn1_norm, ffn1_w_gate, ffn1_w_up, ffn1_w_down, mix_norm, w_in, q_norm, w_uq, kv_norm, w_ukv, w_br_mla, w_br_moba, w_br_dil, w_out, ffn2_norm, ffn2_w_gate, ffn2_w_up, ffn2_w_down, final_norm):
    batch, seq, d = x.shape
    assert d == D_MODEL and seq % (DIL_PATTERNS[-1][1] * DIL_L) == 0 and seq % TM_PROJ == 0
    n = batch * seq
    x2 = x.reshape(n, d)
    tables = _rope_tables(seq) + (_moba_key_aux(seq),)
    dil_bias = [_dil_bias(g, dil) for g, (_, dil) in enumerate(DIL_PATTERNS)]
    final_gain = final_norm.reshape(1, d)
    bf = lambda w: w.astype(BF16)
    row = lambda v: v.reshape(1, -1)

    for l in range(DEPTH):
        x2 = _ffn(x2, row(ffn1_norm[l]), bf(ffn1_w_gate[l]), bf(ffn1_w_up[l]), bf(ffn1_w_down[l]),
                  final_gain, False)
        lw = _layer_weights(w_in[l], w_uq[l], w_ukv[l], q_norm[l], kv_norm[l])
        mqt, mk, mvt, bqt, bk, bvt, d0, d1, d2 = _inproj(
            x2.reshape(batch, seq, d), row(mix_norm[l]), lw, tables)
        a = _mla(mqt, mk, mvt)
        b = _moba(bqt, bk, bvt)
        dil_outs = [_dil(dg, dil_bias[g]) for g, dg in enumerate((d0, d1, d2))]
        x2 = _merge(x2.reshape(batch, seq, d), row(mix_norm[l]), lw["wgate"], a, b, dil_outs,
                    bf(w_br_mla[l]), bf(w_br_moba[l]), bf(w_br_dil[l]), bf(w_out[l])).reshape(n, d)
        x2 = _ffn(x2, row(ffn2_norm[l]), bf(ffn2_w_gate[l]), bf(ffn2_w_up[l]), bf(ffn2_w_down[l]),
                  final_gain, l == DEPTH - 1)
    return x2.reshape(batch, seq, d)
```

```python
import functools

import numpy as np
import jax
import jax.numpy as jnp
from jax import lax
from jax.experimental import pallas as pl
from jax.experimental.pallas import tpu as pltpu

F32 = jnp.float32
BF16 = jnp.bfloat16

D_MODEL = 1024
DEPTH = 4
HEAD_DIM = 64
MLA_HEADS = 6
MLA_NOPE = 64
MLA_ROPE = 32
MLA_V = 64
MLA_Q_RANK = 256
MLA_KV_RANK = 128
ROPE_THETA = 10000.0
MOBA_HEADS = 6
MOBA_BLOCK = 256
MOBA_TOPK = 3
DIL_PATTERNS = ((128, 1), (512, 4), (2048, 16))
DIL_GROUP_HEADS = 4
DIL_HEADS = DIL_GROUP_HEADS * len(DIL_PATTERNS)
N_ALIBI = DIL_HEADS + MOBA_HEADS
N_BRANCHES = 3
D_FF = 2816
NORM_EPS = 1e-6
NEG_INF = -1e30
MLA_IN = MLA_Q_RANK + MLA_KV_RANK + MLA_ROPE
MOBA_IN = 3 * MOBA_HEADS * HEAD_DIM
DIL_IN = 3 * DIL_HEADS * HEAD_DIM
MLA_OUT = MLA_HEADS * MLA_V
MOBA_OUT = MOBA_HEADS * HEAD_DIM
DIL_OUT = DIL_GROUP_HEADS * HEAD_DIM

LANES = 128
SUBLANES = 8
BF16_ROWS = 16
MLA_QK = MLA_NOPE + MLA_ROPE
MLA_QPAD = MLA_HEADS * LANES
LOG2E = 1.4426950408889634
MLA_SCALE = MLA_QK ** -0.5
HEAD_SCALE = HEAD_DIM ** -0.5
DIL_L = DIL_PATTERNS[0][0] // DIL_PATTERNS[0][1]
DIL_COLS = 3 * DIL_GROUP_HEADS * HEAD_DIM

C_CQ = 0
C_CKV = C_CQ + MLA_Q_RANK
C_KR = C_CKV + MLA_KV_RANK
C_KR2 = C_KR + LANES
C_MOBA_K = C_KR2 + LANES
MOBA_KPAD = MOBA_HEADS * LANES
C_DIL = C_MOBA_K + MOBA_KPAD
C_END = C_DIL + DIL_IN

AUX_POS = HEAD_DIM
AUX_POS_PARTS = 3
AUX_SEL = HEAD_DIM + BF16_ROWS

VMEM_LIMIT = 56 * 1024 * 1024

TM_FFN = 512
TM_PROJ = 512
TM_MERGE = 512
FF_CHUNK = 256
ATT_TQ = 512
ATT_TK = MOBA_BLOCK
MLA_NH = 6
MOBA_NH = 6
DIL_STEP_ROWS = 1024


def _cparams(sem):
    return pltpu.CompilerParams(dimension_semantics=sem, vmem_limit_bytes=VMEM_LIMIT)


def _resident(shape):
    nd = len(shape)
    return pl.BlockSpec(shape, lambda *_: (0,) * nd, pipeline_mode=pl.Buffered(1))


def _rms(x, gain):
    ms = jnp.mean(x * x, axis=-1, keepdims=True)
    return x * lax.rsqrt(ms + NORM_EPS) * gain


def _dot(a, b):
    return jnp.dot(a, b, preferred_element_type=F32)


def _dot_nt(a, b):
    return lax.dot_general(a, b, (((1,), (1,)), ((), ())), preferred_element_type=F32)


def _dot_tn(a, b):
    return lax.dot_general(a, b, (((0,), (0,)), ((), ())), preferred_element_type=F32)


def _ffn_kernel(x_ref, g_ref, wg_ref, wu_ref, wd_ref, fg_ref, o_ref, *, final):
    x = x_ref[...]
    h = _rms(x, g_ref[...]).astype(BF16)
    acc = jnp.zeros(x.shape, F32)
    for c in range(D_FF // FF_CHUNK):
        sl = slice(c * FF_CHUNK, (c + 1) * FF_CHUNK)
        a = _dot(h, wg_ref[:, sl])
        u = _dot(h, wu_ref[:, sl])
        g = (a * jax.nn.sigmoid(a) * u).astype(BF16)
        acc = acc + _dot(g, wd_ref[sl, :])
    y = x + 0.5 * acc
    if final:
        y = _rms(y, fg_ref[...])
    o_ref[...] = y


def _ffn(x2, gain, wg, wu, wd, final_gain, final):
    n = x2.shape[0]
    return pl.pallas_call(
        functools.partial(_ffn_kernel, final=final),
        grid=(n // TM_FFN,),
        in_specs=[
            pl.BlockSpec((TM_FFN, D_MODEL), lambda i: (i, 0)),
            _resident((1, D_MODEL)),
            _resident((D_MODEL, D_FF)),
            _resident((D_MODEL, D_FF)),
            _resident((D_FF, D_MODEL)),
            _resident((1, D_MODEL)),
        ],
        out_specs=pl.BlockSpec((TM_FFN, D_MODEL), lambda i: (i, 0)),
        out_shape=jax.ShapeDtypeStruct((n, D_MODEL), F32),
        compiler_params=_cparams(("parallel",)),
        name="ffn_final" if final else "ffn",
    )(x2, gain, wg, wu, wd, final_gain)


def _inproj_kernel(x_ref, gn_ref, w1_ref, wt_ref, qn_ref, wuqt_ref, kvn_ref, wk_ref, wvt_ref,
                   cos_ref, sin_ref, cost_ref, sint_ref, kaux_ref,
                   mqt_ref, mk_ref, mvt_ref, bqt_ref, bk_ref, bvt_ref, d0_ref, d1_ref, d2_ref,
                   stage_ref):
    x = x_ref[0]
    h = _rms(x, gn_ref[...]).astype(BF16)

    pm = _dot(h, w1_ref[:, C_CQ:C_MOBA_K])
    cq = _rms(pm[:, C_CQ:C_CKV], qn_ref[...]).astype(BF16)
    ckv = _rms(pm[:, C_CKV:C_KR], kvn_ref[...]).astype(BF16)
    k_rope = pm[:, C_KR:C_KR2] * cos_ref[...] + pm[:, C_KR2:C_MOBA_K] * sin_ref[...]
    kk = _dot(ckv, wk_ref[...])
    qqt = _dot_nt(wuqt_ref[...], cq)
    cost = cost_ref[...]
    sint = sint_ref[...]
    for hh in range(MLA_HEADS):
        sl = slice(hh * LANES, (hh + 1) * LANES)
        sl2 = slice(MLA_QPAD + hh * LANES, MLA_QPAD + (hh + 1) * LANES)
        mqt_ref[0, sl, :] = ((qqt[sl] * cost + qqt[sl2] * sint) * (MLA_SCALE * LOG2E)).astype(BF16)
        mk_ref[0, :, sl] = (kk[:, sl] + k_rope).astype(BF16)
    mvt_ref[0] = _dot_nt(wvt_ref[...], ckv).astype(BF16)

    bt = _dot_nt(wt_ref[...], h)
    bqt_ref[0] = (bt[:MOBA_OUT] * (HEAD_SCALE * LOG2E)).astype(BF16)
    bvt_ref[0] = bt[MOBA_OUT:].astype(BF16)
    bk_ref[0] = (_dot(h, w1_ref[:, C_MOBA_K:C_DIL]) + kaux_ref[...].astype(F32)).astype(BF16)

    tm = x.shape[0]
    for g, d_ref in enumerate((d0_ref, d1_ref, d2_ref)):
        dil = DIL_PATTERNS[g][1]
        dd = _dot(h, w1_ref[:, C_DIL + g * DIL_COLS:C_DIL + (g + 1) * DIL_COLS])
        for c in range(DIL_COLS // LANES):
            cols = slice(c * LANES, (c + 1) * LANES)
            blk = dd[:, cols]
            stage_ref[c] = blk * HEAD_SCALE if c < DIL_OUT // LANES else blk
            for r in range(dil):
                d_ref[0, r, :, cols] = stage_ref[c, pl.ds(r, tm // dil, stride=dil), :].astype(BF16)


def _inproj(x3, gain, lw, tables):
    batch, seq, _ = x3.shape
    tm = TM_PROJ
    cos_t, sin_t, cos_tt, sin_tt, kaux = tables
    tok = lambda c: pl.BlockSpec((1, tm, c), lambda b, i: (b, i, 0))
    feat = lambda c: pl.BlockSpec((1, c, tm), lambda b, i: (b, 0, i))
    tok_sds = lambda c: jax.ShapeDtypeStruct((batch, seq, c), BF16)
    feat_sds = lambda c: jax.ShapeDtypeStruct((batch, c, seq), BF16)
    dils = [dil for _, dil in DIL_PATTERNS]
    stream = lambda dil: pl.BlockSpec((1, dil, tm // dil, DIL_COLS), lambda b, i: (b, 0, i, 0))
    stream_sds = lambda dil: jax.ShapeDtypeStruct((batch, dil, seq // dil, DIL_COLS), BF16)
    return pl.pallas_call(
        _inproj_kernel,
        grid=(batch, seq // tm),
        in_specs=[
            tok(D_MODEL),
            _resident((1, D_MODEL)),
            _resident(lw["w1"].shape),
            _resident(lw["wt"].shape),
            _resident((1, MLA_Q_RANK)),
            _resident(lw["wuqt"].shape),
            _resident((1, MLA_KV_RANK)),
            _resident(lw["wk"].shape),
            _resident(lw["wvt"].shape),
            pl.BlockSpec((tm, LANES), lambda b, i: (i, 0)),
            pl.BlockSpec((tm, LANES), lambda b, i: (i, 0)),
            pl.BlockSpec((LANES, tm), lambda b, i: (0, i)),
            pl.BlockSpec((LANES, tm), lambda b, i: (0, i)),
            pl.BlockSpec((tm, MOBA_KPAD), lambda b, i: (i, 0)),
        ],
        out_specs=[feat(MLA_QPAD), tok(MLA_QPAD), feat(MLA_OUT),
                   feat(MOBA_OUT), tok(MOBA_KPAD), feat(MOBA_OUT)] + [stream(dil) for dil in dils],
        out_shape=[feat_sds(MLA_QPAD), tok_sds(MLA_QPAD), feat_sds(MLA_OUT),
                   feat_sds(MOBA_OUT), tok_sds(MOBA_KPAD), feat_sds(MOBA_OUT)]
        + [stream_sds(dil) for dil in dils],
        scratch_shapes=[pltpu.VMEM((DIL_COLS // LANES, tm, LANES), F32)],
        compiler_params=_cparams(("parallel", "parallel")),
        name="inproj",
    )(x3, gain, lw["w1"], lw["wt"], lw["qn"], lw["wuqt"], lw["kvn"], lw["wk"], lw["wvt"],
      cos_t, sin_t, cos_tt, sin_tt, kaux)


ROW_M, ROW_ALPHA, ROW_L, ROW_SMAX = range(4)
FLASH_UNROLL = 8
SUM_ROWS = 16


def _flash_heads(nh, n_full, n_diag, tq, tk, dv, score_fn, value_fn, mask_fn,
                 s_buf, acc_buf, stat_buf):
    i, t = n_full, tq
    heads = range(nh)
    ones = jnp.ones((SUM_ROWS, tk), BF16)
    col_max = lambda s: jnp.max(s, axis=0, keepdims=True)
    stat = lambda h, r: stat_buf[h, r:r + 1, :]

    def set_stat(h, r, v):
        stat_buf[h, r:r + 1, :] = v

    def values(h, j, p):
        pv = _dot(jnp.concatenate([value_fn(j, h), ones], axis=0), p)
        alpha = stat(h, ROW_ALPHA)
        acc_buf[h] = alpha * acc_buf[h] + pv[:dv]
        set_stat(h, ROW_L, alpha * stat(h, ROW_L) + pv[dv:dv + 1])

    def softmax(h, s, s_max):
        m = stat(h, ROW_M)
        m_new = jnp.maximum(m, s_max)
        alpha = jnp.exp2(m - m_new)
        p = jnp.exp2(s - m_new)
        set_stat(h, ROW_M, m_new)
        set_stat(h, ROW_ALPHA, alpha)
        return p.astype(BF16)

    for h in heads:
        sc = score_fn(0, h)
        s_buf[h] = sc
        set_stat(h, ROW_SMAX, col_max(sc))
        set_stat(h, ROW_M, jnp.full((1, t), NEG_INF, F32))
        set_stat(h, ROW_L, jnp.zeros((1, t), F32))
        acc_buf[h] = jnp.zeros((dv, t), F32)

    def block_step(j):
        for h in heads:
            sc = score_fn(j + 1, h)
            values(h, j, softmax(h, s_buf[h], stat(h, ROW_SMAX)))
            s_buf[h] = sc
            set_stat(h, ROW_SMAX, col_max(sc))

    @pl.loop(0, i // FLASH_UNROLL)
    def _(jj):
        for u in range(FLASH_UNROLL):
            block_step(FLASH_UNROLL * jj + u)

    done = i - i % FLASH_UNROLL
    chunk = FLASH_UNROLL // 2
    while chunk >= n_diag:
        @pl.when(i % (2 * chunk) >= chunk)
        def _(done=done, chunk=chunk):
            for u in range(chunk):
                block_step(done + u)
        done = done + jnp.where(i % (2 * chunk) >= chunk, chunk, 0)
        chunk //= 2

    for d in range(n_diag):
        for h in heads:
            sc = score_fn(i + d + 1, h) if d + 1 < n_diag else None
            s = mask_fn(d, s_buf[h])
            values(h, i + d, softmax(h, s, col_max(s)))
            if sc is not None:
                s_buf[h] = sc
    return [acc_buf[h] / stat(h, ROW_L) for h in heads]


def _store_feature_major(o_ref, outs):
    dv = outs[0].shape[0]
    for h, out in enumerate(outs):
        o_ref[0, h * dv:(h + 1) * dv, :] = out.astype(o_ref.dtype)


def _flash_scratch(nh, tq, tk, dv):
    return [pltpu.VMEM((nh, tk, tq), F32), pltpu.VMEM((nh, dv, tq), F32),
            pltpu.VMEM((nh, SUBLANES, tq), F32)]


def _causal_masker(tq, tk):
    ahead = (lax.broadcasted_iota(jnp.int32, (tk, tq), 1)
             - lax.broadcasted_iota(jnp.int32, (tk, tq), 0))
    return lambda d, s: jnp.where(ahead >= d * tk, s, NEG_INF)


def _mla_kernel(qt_ref, k_ref, vt_ref, o_ref, *flash_scratch):
    tq, tk = ATT_TQ, ATT_TK
    n_diag = tq // tk
    nh = MLA_NH
    qs = [qt_ref[0, h * LANES:h * LANES + MLA_QK, :] for h in range(nh)]

    def scores(j, h):
        start = pl.multiple_of(j * tk, tk)
        return _dot(k_ref[0, pl.ds(start, tk), h * LANES:h * LANES + MLA_QK], qs[h])

    def values(j, h):
        return vt_ref[0, h * MLA_V:(h + 1) * MLA_V, pl.ds(pl.multiple_of(j * tk, tk), tk)]

    outs = _flash_heads(nh, n_diag * pl.program_id(2), n_diag, tq, tk, MLA_V, scores, values,
                        _causal_masker(tq, tk), *flash_scratch)
    _store_feature_major(o_ref, outs)


def _mla(mqt, mk, mvt):
    batch, seq, _ = mk.shape
    tq = ATT_TQ
    nh = MLA_NH
    return pl.pallas_call(
        _mla_kernel,
        grid=(batch, MLA_HEADS // nh, seq // tq),
        in_specs=[pl.BlockSpec((1, nh * LANES, tq), lambda b, g, i: (b, g, i)),
                  pl.BlockSpec((1, seq, nh * LANES), lambda b, g, i: (b, 0, g)),
                  pl.BlockSpec((1, nh * MLA_V, seq), lambda b, g, i: (b, g, 0))],
        out_specs=pl.BlockSpec((1, nh * MLA_V, tq), lambda b, g, i: (b, g, i)),
        out_shape=jax.ShapeDtypeStruct((batch, MLA_OUT, seq), BF16),
        scratch_shapes=_flash_scratch(nh, tq, ATT_TK, MLA_V),
        compiler_params=_cparams(("parallel", "parallel", "arbitrary")),
        name="mla_attn",
    )(mqt, mk, mvt)


def _moba_kernel(qt_ref, k_ref, vt_ref, o_ref, kmh_ref, kml_ref, *flash_scratch, nblk):
    i = pl.program_id(2)
    tq, tk = ATT_TQ, MOBA_BLOCK
    n_diag = tq // tk
    nh = MOBA_NH

    @pl.when(i == 0)
    def _():
        kf = k_ref[0].astype(F32).reshape(nblk, tk, nh * LANES)
        km = jnp.sum(kf, axis=1) * (1.0 / tk)
        hi = km.astype(BF16)
        kmh_ref[...] = hi
        kml_ref[...] = (km - hi.astype(F32)).astype(BF16)

    blk = lax.broadcasted_iota(jnp.int32, (nblk, 1), 0)
    blk_f = blk.astype(F32)
    own = n_diag * i + lax.broadcasted_iota(jnp.int32, (1, tq), 1) // tk
    pos_rows = lax.broadcasted_iota(jnp.int32, (AUX_SEL - AUX_POS, 1), 0) < AUX_POS_PARTS
    ones_rows = jnp.broadcast_to(jnp.where(pos_rows, 1.0, 0.0), (AUX_SEL - AUX_POS, tq)).astype(BF16)
    pad_q = jnp.zeros((LANES - HEAD_DIM, tq), BF16)
    qs = []
    for h in range(nh):
        q = qt_ref[0, h * HEAD_DIM:(h + 1) * HEAD_DIM, :]
        q_plain = jnp.concatenate([q, pad_q], axis=0)
        cols = slice(h * LANES, (h + 1) * LANES)
        gate = _dot(kmh_ref[:, cols], q_plain) + _dot(kml_ref[:, cols], q_plain)
        gate = jnp.where(blk < own, gate, NEG_INF)
        chosen = blk == own
        for _ in range(min(MOBA_TOPK, nblk)):
            best = jnp.max(gate, axis=0, keepdims=True)
            first = jnp.min(jnp.where(gate == best, blk_f, float(nblk)), axis=0, keepdims=True)
            pick = blk_f == first
            chosen = chosen | (pick & (blk < own))
            gate = jnp.where(pick, -jnp.inf, gate)
        sel = jnp.where(chosen, 0.0, NEG_INF).astype(BF16)
        qs.append(jnp.concatenate([q, ones_rows, sel], axis=0))

    def scores(j, h):
        start = pl.multiple_of(j * tk, tk)
        return _dot(k_ref[0, pl.ds(start, tk), h * LANES:h * LANES + AUX_SEL + nblk], qs[h])

    def values(j, h):
        return vt_ref[0, h * HEAD_DIM:(h + 1) * HEAD_DIM, pl.ds(pl.multiple_of(j * tk, tk), tk)]

    outs = _flash_heads(nh, n_diag * i, n_diag, tq, tk, HEAD_DIM, scores, values,
                        _causal_masker(tq, tk), *flash_scratch)
    _store_feature_major(o_ref, outs)


def _moba(bqt, bk, bvt):
    batch, seq, _ = bk.shape
    tq = ATT_TQ
    nh = MOBA_NH
    nblk = seq // MOBA_BLOCK
    assert AUX_SEL + nblk <= LANES
    cols = nh * HEAD_DIM
    return pl.pallas_call(
        functools.partial(_moba_kernel, nblk=nblk),
        grid=(batch, MOBA_HEADS // nh, seq // tq),
        in_specs=[
            pl.BlockSpec((1, cols, tq), lambda b, g, i: (b, g, i)),
            pl.BlockSpec((1, seq, nh * LANES), lambda b, g, i: (b, 0, g)),
            pl.BlockSpec((1, cols, seq), lambda b, g, i: (b, g, 0)),
        ],
        out_specs=pl.BlockSpec((1, cols, tq), lambda b, g, i: (b, g, i)),
        out_shape=jax.ShapeDtypeStruct((batch, MOBA_OUT, seq), BF16),
        scratch_shapes=[
            pltpu.VMEM((nblk, nh * LANES), BF16),
            pltpu.VMEM((nblk, nh * LANES), BF16),
        ] + _flash_scratch(nh, tq, MOBA_BLOCK, HEAD_DIM),
        compiler_params=_cparams(("parallel", "parallel", "arbitrary")),
        name="moba_attn",
    )(bqt, bk, bvt)


def _dil_kernel(own_ref, prev_ref, bias_ref, o_ref, lse_ref, *, ns, qb):
    n = pl.program_id(2)
    L = DIL_L
    low = lax.broadcasted_iota(jnp.int32, (1, LANES), 1) < HEAD_DIM
    first_cols = lax.broadcasted_iota(jnp.int32, (1, 2 * L), 1) < L
    problems = [(si, pair, blk) for si in range(ns) for pair in range(DIL_OUT // LANES)
                for blk in range(qb // L)]

    def window(si, blk, cols):
        if blk == 0:
            return jnp.concatenate([prev_ref[0, si, :, cols], own_ref[0, si, :L, cols]], axis=0)
        return own_ref[0, si, (blk - 1) * L:(blk + 1) * L, cols]

    scores = []
    for si, pair, blk in problems:
        qp = own_ref[0, si, blk * L:(blk + 1) * L, pair * LANES:(pair + 1) * LANES]
        zero = jnp.zeros_like(qp)
        keys = window(si, blk, slice(DIL_OUT + pair * LANES, DIL_OUT + (pair + 1) * LANES))
        scores.append([_dot_nt(jnp.where(low, qp, zero), keys),
                       _dot_nt(jnp.where(low, zero, qp), keys)])

    probs = []
    for (si, pair, blk), sc in zip(problems, scores):
        per_head = []
        for hh in range(2):
            s = sc[hh] + bias_ref[2 * pair + hh]
            if blk == 0:
                s = jnp.where(first_cols & (n == 0), NEG_INF, s)
            m = jnp.max(s, axis=-1, keepdims=True)
            e = jnp.exp(s - m)
            den = jnp.sum(e, axis=-1, keepdims=True)
            per_head.append((e.astype(BF16), den, m + jnp.log(den)))
        probs.append(per_head)

    for (si, pair, blk), per_head in zip(problems, probs):
        vals = window(si, blk, slice(2 * DIL_OUT + pair * LANES, 2 * DIL_OUT + (pair + 1) * LANES))
        o_pair = [_dot(e, vals) / den for e, den, _ in per_head]
        lse_pair = [jnp.broadcast_to(lse, (L, LANES)) for _, _, lse in per_head]
        dst = (0, si, slice(blk * L, (blk + 1) * L), slice(pair * LANES, (pair + 1) * LANES))
        o_ref[dst] = jnp.where(low, o_pair[0], o_pair[1])
        lse_ref[dst] = jnp.where(low, lse_pair[0], lse_pair[1])


def _dil(dg, bias):
    batch, dilation, rows, _ = dg.shape
    L = DIL_L
    qb = min(DIL_STEP_ROWS, rows)
    ns = min(DIL_STEP_ROWS // qb, dilation)
    per = qb // L
    out_sds = jax.ShapeDtypeStruct((batch, dilation, rows, DIL_OUT), F32)
    return pl.pallas_call(
        functools.partial(_dil_kernel, ns=ns, qb=qb),
        grid=(batch, dilation // ns, rows // qb),
        in_specs=[
            pl.BlockSpec((1, ns, qb, DIL_COLS), lambda b, r, n: (b, r, n, 0)),
            pl.BlockSpec((1, ns, L, DIL_COLS),
                         lambda b, r, n: (b, r, jnp.maximum(n * per - 1, 0), 0)),
            _resident(bias.shape),
        ],
        out_specs=[pl.BlockSpec((1, ns, qb, DIL_OUT), lambda b, r, n: (b, r, n, 0))] * 2,
        out_shape=[out_sds, out_sds],
        compiler_params=_cparams(("parallel", "parallel", "arbitrary")),
        name="dil_attn_d%d" % dilation,
    )(dg, dg, bias)


def _merge_kernel(x_ref, gn_ref, wgate_ref, a_ref, b_ref, o0_ref, o1_ref, o2_ref,
                  l0_ref, l1_ref, l2_ref, wa_ref, wb_ref, wc_ref, wo_ref, out_ref, *scratch):
    x = x_ref[0]
    h = _rms(x, gn_ref[...]).astype(BF16)
    tm = x.shape[0]
    scratch = list(scratch)

    def token_order(ref):
        dil = ref.shape[1]
        if dil == 1:
            return ref[0, 0]
        buf = scratch.pop()
        for c in range(buf.shape[0]):
            for r in range(dil):
                buf[c, pl.ds(r, tm // dil, stride=dil), :] = ref[0, r, :, c * LANES:(c + 1) * LANES]
        return jnp.concatenate([buf[c] for c in range(buf.shape[0])], axis=1)

    lses = [token_order(r) for r in (l0_ref, l1_ref, l2_ref)]
    outs = [token_order(r) for r in (o0_ref, o1_ref, o2_ref)]
    mx = jnp.maximum(jnp.maximum(lses[0], lses[1]), lses[2])
    es = [jnp.exp(l - mx) for l in lses]
    den = es[0] + es[1] + es[2]
    c = (es[0] / den) * outs[0] + (es[1] / den) * outs[1] + (es[2] / den) * outs[2]

    projected = (_dot_tn(a_ref[0], wa_ref[...]), _dot_tn(b_ref[0], wb_ref[...]),
                 _dot(c.astype(BF16), wc_ref[...]))
    merged = jnp.zeros(x.shape, F32)
    for k, proj in enumerate(projected):
        gate = jax.nn.sigmoid(_dot(h, wgate_ref[:, k * D_MODEL:(k + 1) * D_MODEL]))
        merged = merged + gate * proj
    out_ref[0] = x + _dot(merged.astype(BF16), wo_ref[...])


def _merge(x3, gain, wgate, a, b, dil_outs, wa, wb, wc, wo):
    batch, seq, _ = x3.shape
    tm = TM_MERGE
    tok = lambda c: pl.BlockSpec((1, tm, c), lambda b, i: (b, i, 0))
    feat = lambda c: pl.BlockSpec((1, c, tm), lambda b, i: (b, 0, i))
    stream = lambda arr: pl.BlockSpec((1, arr.shape[1], tm // arr.shape[1], DIL_OUT),
                                      lambda b, i: (b, 0, i, 0))
    (o0, l0), (o1, l1), (o2, l2) = dil_outs
    streams = (o0, o1, o2, l0, l1, l2)
    n_reordered = sum(arr.shape[1] > 1 for arr in streams)
    return pl.pallas_call(
        _merge_kernel,
        grid=(batch, seq // tm),
        in_specs=[tok(D_MODEL), _resident((1, D_MODEL)), _resident(wgate.shape),
                  feat(MLA_OUT), feat(MOBA_OUT)]
        + [stream(arr) for arr in streams]
        + [_resident(wa.shape), _resident(wb.shape), _resident(wc.shape), _resident(wo.shape)],
        out_specs=tok(D_MODEL),
        out_shape=jax.ShapeDtypeStruct((batch, seq, D_MODEL), F32),
        scratch_shapes=[pltpu.VMEM((DIL_OUT // LANES, tm, LANES), F32)] * n_reordered,
        compiler_params=_cparams(("parallel", "parallel")),
        name="merge",
    )(x3, gain, wgate, a, b, *streams, wa, wb, wc, wo)


def _alibi_slopes():
    return (2.0 ** (-8.0 * np.arange(1, N_ALIBI + 1, dtype=np.float32) / N_ALIBI)).astype(np.float32)


def _rope_tables(seq):
    pos = jnp.arange(seq, dtype=F32)
    inv = ROPE_THETA ** (-jnp.arange(0, MLA_ROPE, 2, dtype=F32) / MLA_ROPE)
    ang = pos[:, None] * inv[None, :]
    cos, sin = jnp.cos(ang), jnp.sin(ang)
    pad = jnp.zeros((seq, LANES - MLA_QK), F32)
    cos_t = jnp.concatenate([jnp.ones((seq, MLA_NOPE), F32), cos, cos, pad], axis=1)
    sin_t = jnp.concatenate([jnp.zeros((seq, MLA_NOPE), F32), -sin, sin, pad], axis=1)
    return cos_t, sin_t, cos_t.T, sin_t.T


def _dil_bias(group, dilation):
    L = DIL_L
    slopes = _alibi_slopes()[group * DIL_GROUP_HEADS:(group + 1) * DIL_GROUP_HEADS]
    steps = L + np.arange(L)[:, None] - np.arange(2 * L)[None, :]
    valid = (steps >= 0) & (steps <= L)
    dist = (steps * dilation).astype(np.float32)
    bias = np.where(valid[None], -slopes[:, None, None] * dist[None], np.float32(NEG_INF))
    return jnp.asarray(bias.astype(np.float32))


def _moba_key_aux(seq):
    slopes = _alibi_slopes()[DIL_HEADS:].astype(np.float64) * LOG2E
    pos = jnp.asarray((slopes[None, :] * np.arange(seq)[:, None]).astype(np.float32))
    aux = jnp.zeros((seq, MOBA_HEADS, LANES), F32)
    rest = pos
    for part in range(AUX_POS_PARTS):
        piece = rest.astype(BF16).astype(F32)
        aux = aux.at[:, :, AUX_POS + part].set(piece)
        rest = rest - piece
    onehot = jax.nn.one_hot(jnp.arange(seq) // MOBA_BLOCK, seq // MOBA_BLOCK, dtype=F32)
    aux = aux.at[:, :, AUX_SEL:AUX_SEL + seq // MOBA_BLOCK].set(onehot[:, None, :])
    return aux.reshape(seq, MOBA_KPAD).astype(BF16)


def _layer_weights(w_in, w_uq, w_ukv, q_norm, kv_norm):
    d = w_in.shape[0]
    z = lambda c: jnp.zeros((d, c), w_in.dtype)
    i0 = MLA_Q_RANK + MLA_KV_RANK
    kr = w_in[:, i0:i0 + MLA_ROPE]
    hr = MLA_ROPE // 2
    kr_pad = jnp.concatenate([z(MLA_NOPE), kr, z(LANES - MLA_QK)], axis=1)
    kr_swap = jnp.concatenate([z(MLA_NOPE), kr[:, hr:], kr[:, :hr], z(LANES - MLA_QK)], axis=1)
    moba = w_in[:, MLA_IN:MLA_IN + MOBA_IN]
    bq, bk, bv = (moba[:, c * MOBA_OUT:(c + 1) * MOBA_OUT] for c in range(3))
    bk = jnp.concatenate([bk.reshape(d, MOBA_HEADS, HEAD_DIM),
                          jnp.zeros((d, MOBA_HEADS, LANES - HEAD_DIM), w_in.dtype)],
                         axis=2).reshape(d, MOBA_KPAD)
    dil = w_in[:, MLA_IN + MOBA_IN:MLA_IN + MOBA_IN + DIL_IN].reshape(d, 3, DIL_HEADS, HEAD_DIM)
    groups = [dil[:, :, g * DIL_GROUP_HEADS:(g + 1) * DIL_GROUP_HEADS].reshape(d, DIL_COLS)
              for g in range(len(DIL_PATTERNS))]
    w1 = jnp.concatenate([w_in[:, :i0], kr_pad, kr_swap, bk] + groups, axis=1).astype(BF16)
    wt = jnp.concatenate([bq, bv], axis=1).T.astype(BF16)
    wgate = w_in[:, MLA_IN + MOBA_IN + DIL_IN:].astype(BF16)

    r = w_uq.shape[0]
    uq = w_uq.reshape(r, MLA_HEADS, MLA_QK)
    zq = jnp.zeros((r, MLA_HEADS, LANES - MLA_QK), w_uq.dtype)
    zn = jnp.zeros((r, MLA_HEADS, MLA_NOPE), w_uq.dtype)
    uq_pad = jnp.concatenate([uq, zq], axis=2).reshape(r, MLA_QPAD)
    uq_swap = jnp.concatenate(
        [zn, uq[:, :, MLA_NOPE + hr:], uq[:, :, MLA_NOPE:MLA_NOPE + hr], zq], axis=2
    ).reshape(r, MLA_QPAD)
    wuqt = jnp.concatenate([uq_pad, uq_swap], axis=1).T.astype(BF16)

    rk = w_ukv.shape[0]
    ukv = w_ukv.reshape(rk, MLA_HEADS, MLA_NOPE + MLA_V)
    zk = jnp.zeros((rk, MLA_HEADS, LANES - MLA_NOPE), w_ukv.dtype)
    wk = jnp.concatenate([ukv[:, :, :MLA_NOPE], zk], axis=2).reshape(rk, MLA_QPAD).astype(BF16)
    wvt = ukv[:, :, MLA_NOPE:].reshape(rk, MLA_OUT).T.astype(BF16)
    return dict(w1=w1, wt=wt, wgate=wgate, wuqt=wuqt, wk=wk, wvt=wvt,
                qn=q_norm.reshape(1, -1), kvn=kv_norm.reshape(1, -1))


def kernel(x, ffn1_norm, ffn1_w_gate, ffn1_w_up, ffn1_w_down, mix_norm, w_in, q_norm, w_uq, kv_norm, w_ukv, w_br_mla, w_br_moba, w_br_dil, w_out, ffn2_norm, ffn2_w_gate, ffn2_w_up, ffn2_w_down, final_norm):
    batch, seq, d = x.shape
    assert d == D_MODEL and seq % (DIL_PATTERNS[-1][1] * DIL_L) == 0 and seq % TM_PROJ == 0
    n = batch * seq
    x2 = x.reshape(n, d)
    tables = _rope_tables(seq) + (_moba_key_aux(seq),)
    dil_bias = [_dil_bias(g, dil) for g, (_, dil) in enumerate(DIL_PATTERNS)]
    final_gain = final_norm.reshape(1, d)
    bf = lambda w: w.astype(BF16)
    row = lambda v: v.reshape(1, -1)

    for l in range(DEPTH):
        x2 = _ffn(x2, row(ffn1_norm[l]), bf(ffn1_w_gate[l]), bf(ffn1_w_up[l]), bf(ffn1_w_down[l]),
                  final_gain, False)
        lw = _layer_weights(w_in[l], w_uq[l], w_ukv[l], q_norm[l], kv_norm[l])
        mqt, mk, mvt, bqt, bk, bvt, d0, d1, d2 = _inproj(
            x2.reshape(batch, seq, d), row(mix_norm[l]), lw, tables)
        a = _mla(mqt, mk, mvt)
        b = _moba(bqt, bk, bvt)
        dil_outs = [_dil(dg, dil_bias[g]) for g, dg in enumerate((d0, d1, d2))]
        x2 = _merge(x2.reshape(batch, seq, d), row(mix_norm[l]), lw["wgate"], a, b, dil_outs,
                    bf(w_br_mla[l]), bf(w_br_moba[l]), bf(w_br_dil[l]), bf(w_out[l])).reshape(n, d)
        x2 = _ffn(x2, row(ffn2_norm[l]), bf(ffn2_w_gate[l]), bf(ffn2_w_up[l]), bf(ffn2_w_down[l]),
                  final_gain, l == DEPTH - 1)
    return x2.reshape(batch, seq, d)
```

```python
import functools

import numpy as np
import jax
import jax.numpy as jnp
from jax import lax
from jax.experimental import pallas as pl
from jax.experimental.pallas import tpu as pltpu

F32 = jnp.float32
BF16 = jnp.bfloat16

D_MODEL = 1024
DEPTH = 4
HEAD_DIM = 64
MLA_HEADS = 6
MLA_NOPE = 64
MLA_ROPE = 32
MLA_V = 64
MLA_Q_RANK = 256
MLA_KV_RANK = 128
ROPE_THETA = 10000.0
MOBA_HEADS = 6
MOBA_BLOCK = 256
MOBA_TOPK = 3
DIL_PATTERNS = ((128, 1), (512, 4), (2048, 16))
DIL_GROUP_HEADS = 4
DIL_HEADS = DIL_GROUP_HEADS * len(DIL_PATTERNS)
N_ALIBI = DIL_HEADS + MOBA_HEADS
N_BRANCHES = 3
D_FF = 2816
NORM_EPS = 1e-6
NEG_INF = -1e30
MLA_IN = MLA_Q_RANK + MLA_KV_RANK + MLA_ROPE
MOBA_IN = 3 * MOBA_HEADS * HEAD_DIM
DIL_IN = 3 * DIL_HEADS * HEAD_DIM
MLA_OUT = MLA_HEADS * MLA_V
MOBA_OUT = MOBA_HEADS * HEAD_DIM
DIL_OUT = DIL_GROUP_HEADS * HEAD_DIM

LANES = 128
SUBLANES = 8
BF16_ROWS = 16
MLA_QK = MLA_NOPE + MLA_ROPE
MLA_QPAD = MLA_HEADS * LANES
LOG2E = 1.4426950408889634
MLA_SCALE = MLA_QK ** -0.5
HEAD_SCALE = HEAD_DIM ** -0.5
DIL_L = DIL_PATTERNS[0][0] // DIL_PATTERNS[0][1]
DIL_COLS = 3 * DIL_GROUP_HEADS * HEAD_DIM

C_CQ = 0
C_CKV = C_CQ + MLA_Q_RANK
C_KR = C_CKV + MLA_KV_RANK
C_KR2 = C_KR + LANES
C_MOBA_K = C_KR2 + LANES
MOBA_KPAD = MOBA_HEADS * LANES
C_DIL = C_MOBA_K + MOBA_KPAD
C_END = C_DIL + DIL_IN

AUX_POS = HEAD_DIM
AUX_POS_PARTS = 3
AUX_SEL = HEAD_DIM + BF16_ROWS

VMEM_LIMIT = 56 * 1024 * 1024

TM_FFN = 512
TM_PROJ = 512
TM_MERGE = 512
FF_CHUNK = 256
MERGE_CHUNK = 256
ATT_TQ = 512
ATT_TK = MOBA_BLOCK
MLA_NH = 6
MOBA_NH = 6
DIL_STEP_ROWS = 1024


def _cparams(sem):
    return pltpu.CompilerParams(dimension_semantics=sem, vmem_limit_bytes=VMEM_LIMIT)


def _resident(shape):
    nd = len(shape)
    return pl.BlockSpec(shape, lambda *_: (0,) * nd, pipeline_mode=pl.Buffered(1))


def _rms(x, gain):
    ms = jnp.mean(x * x, axis=-1, keepdims=True)
    return x * lax.rsqrt(ms + NORM_EPS) * gain


def _dot(a, b):
    return jnp.dot(a, b, preferred_element_type=F32)


def _dot_nt(a, b):
    return lax.dot_general(a, b, (((1,), (1,)), ((), ())), preferred_element_type=F32)


def _dot_tn(a, b):
    return lax.dot_general(a, b, (((0,), (0,)), ((), ())), preferred_element_type=F32)


def _ffn_kernel(x_ref, g_ref, wg_ref, wu_ref, wd_ref, fg_ref, o_ref, *, final):
    x = x_ref[...]
    h = _rms(x, g_ref[...]).astype(BF16)
    acc = jnp.zeros(x.shape, F32)
    for c in range(D_FF // FF_CHUNK):
        sl = slice(c * FF_CHUNK, (c + 1) * FF_CHUNK)
        a = _dot(h, wg_ref[:, sl])
        u = _dot(h, wu_ref[:, sl])
        g = (a * jax.nn.sigmoid(a) * u).astype(BF16)
        acc = acc + _dot(g, wd_ref[sl, :])
    y = x + 0.5 * acc
    if final:
        y = _rms(y, fg_ref[...])
    o_ref[...] = y


def _ffn(x2, gain, wg, wu, wd, final_gain, final):
    n = x2.shape[0]
    return pl.pallas_call(
        functools.partial(_ffn_kernel, final=final),
        grid=(n // TM_FFN,),
        in_specs=[
            pl.BlockSpec((TM_FFN, D_MODEL), lambda i: (i, 0)),
            _resident((1, D_MODEL)),
            _resident((D_MODEL, D_FF)),
            _resident((D_MODEL, D_FF)),
            _resident((D_FF, D_MODEL)),
            _resident((1, D_MODEL)),
        ],
        out_specs=pl.BlockSpec((TM_FFN, D_MODEL), lambda i: (i, 0)),
        out_shape=jax.ShapeDtypeStruct((n, D_MODEL), F32),
        compiler_params=_cparams(("parallel",)),
        name="ffn_final" if final else "ffn",
    )(x2, gain, wg, wu, wd, final_gain)


def _inproj_kernel(x_ref, gn_ref, w1_ref, wt_ref, qn_ref, wuqt_ref, kvn_ref, wk_ref, wvt_ref,
                   cos_ref, sin_ref, cost_ref, sint_ref, kaux_ref,
                   mqt_ref, mk_ref, mvt_ref, bqt_ref, bk_ref, bvt_ref, d0_ref, d1_ref, d2_ref,
                   stage_ref):
    x = x_ref[0]
    h = _rms(x, gn_ref[...]).astype(BF16)

    pm = _dot(h, w1_ref[:, C_CQ:C_MOBA_K])
    cq = _rms(pm[:, C_CQ:C_CKV], qn_ref[...]).astype(BF16)
    ckv = _rms(pm[:, C_CKV:C_KR], kvn_ref[...]).astype(BF16)
    k_rope = pm[:, C_KR:C_KR2] * cos_ref[...] + pm[:, C_KR2:C_MOBA_K] * sin_ref[...]
    kk = _dot(ckv, wk_ref[...])
    qqt = _dot_nt(wuqt_ref[...], cq)
    cost = cost_ref[...]
    sint = sint_ref[...]
    for hh in range(MLA_HEADS):
        sl = slice(hh * LANES, (hh + 1) * LANES)
        sl2 = slice(MLA_QPAD + hh * LANES, MLA_QPAD + (hh + 1) * LANES)
        mqt_ref[0, sl, :] = ((qqt[sl] * cost + qqt[sl2] * sint) * (MLA_SCALE * LOG2E)).astype(BF16)
        mk_ref[0, :, sl] = (kk[:, sl] + k_rope).astype(BF16)
    mvt_ref[0] = _dot_nt(wvt_ref[...], ckv).astype(BF16)

    bt = _dot_nt(wt_ref[...], h)
    bqt_ref[0] = (bt[:MOBA_OUT] * (HEAD_SCALE * LOG2E)).astype(BF16)
    bvt_ref[0] = bt[MOBA_OUT:].astype(BF16)
    bk_ref[0] = (_dot(h, w1_ref[:, C_MOBA_K:C_DIL]) + kaux_ref[...].astype(F32)).astype(BF16)

    tm = x.shape[0]
    for g, d_ref in enumerate((d0_ref, d1_ref, d2_ref)):
        dil = DIL_PATTERNS[g][1]
        dd = _dot(h, w1_ref[:, C_DIL + g * DIL_COLS:C_DIL + (g + 1) * DIL_COLS])
        for c in range(DIL_COLS // LANES):
            cols = slice(c * LANES, (c + 1) * LANES)
            blk = dd[:, cols]
            stage_ref[c] = blk * HEAD_SCALE if c < DIL_OUT // LANES else blk
            for r in range(dil):
                d_ref[0, r, :, cols] = stage_ref[c, pl.ds(r, tm // dil, stride=dil), :].astype(BF16)


def _inproj(x3, gain, lw, tables):
    batch, seq, _ = x3.shape
    tm = TM_PROJ
    cos_t, sin_t, cos_tt, sin_tt, kaux = tables
    tok = lambda c: pl.BlockSpec((1, tm, c), lambda b, i: (b, i, 0))
    feat = lambda c: pl.BlockSpec((1, c, tm), lambda b, i: (b, 0, i))
    tok_sds = lambda c: jax.ShapeDtypeStruct((batch, seq, c), BF16)
    feat_sds = lambda c: jax.ShapeDtypeStruct((batch, c, seq), BF16)
    dils = [dil for _, dil in DIL_PATTERNS]
    stream = lambda dil: pl.BlockSpec((1, dil, tm // dil, DIL_COLS), lambda b, i: (b, 0, i, 0))
    stream_sds = lambda dil: jax.ShapeDtypeStruct((batch, dil, seq // dil, DIL_COLS), BF16)
    return pl.pallas_call(
        _inproj_kernel,
        grid=(batch, seq // tm),
        in_specs=[
            tok(D_MODEL),
            _resident((1, D_MODEL)),
            _resident(lw["w1"].shape),
            _resident(lw["wt"].shape),
            _resident((1, MLA_Q_RANK)),
            _resident(lw["wuqt"].shape),
            _resident((1, MLA_KV_RANK)),
            _resident(lw["wk"].shape),
            _resident(lw["wvt"].shape),
            pl.BlockSpec((tm, LANES), lambda b, i: (i, 0)),
            pl.BlockSpec((tm, LANES), lambda b, i: (i, 0)),
            pl.BlockSpec((LANES, tm), lambda b, i: (0, i)),
            pl.BlockSpec((LANES, tm), lambda b, i: (0, i)),
            pl.BlockSpec((tm, MOBA_KPAD), lambda b, i: (i, 0)),
        ],
        out_specs=[feat(MLA_QPAD), tok(MLA_QPAD), feat(MLA_OUT),
                   feat(MOBA_OUT), tok(MOBA_KPAD), feat(MOBA_OUT)] + [stream(dil) for dil in dils],
        out_shape=[feat_sds(MLA_QPAD), tok_sds(MLA_QPAD), feat_sds(MLA_OUT),
                   feat_sds(MOBA_OUT), tok_sds(MOBA_KPAD), feat_sds(MOBA_OUT)]
        + [stream_sds(dil) for dil in dils],
        scratch_shapes=[pltpu.VMEM((DIL_COLS // LANES, tm, LANES), F32)],
        compiler_params=_cparams(("parallel", "parallel")),
        name="inproj",
    )(x3, gain, lw["w1"], lw["wt"], lw["qn"], lw["wuqt"], lw["kvn"], lw["wk"], lw["wvt"],
      cos_t, sin_t, cos_tt, sin_tt, kaux)


ROW_M, ROW_ALPHA, ROW_L, ROW_SMAX = range(4)
FLASH_UNROLL = 8
SUM_ROWS = 16


def _flash_heads(nh, n_full, n_diag, tq, tk, dv, score_fn, value_fn, mask_fn,
                 s_buf, acc_buf, stat_buf):
    i, t = n_full, tq
    heads = range(nh)
    ones = jnp.ones((SUM_ROWS, tk), BF16)
    col_max = lambda s: jnp.max(s, axis=0, keepdims=True)
    stat = lambda h, r: stat_buf[h, r:r + 1, :]

    def set_stat(h, r, v):
        stat_buf[h, r:r + 1, :] = v

    def values(h, j, p):
        pv = _dot(jnp.concatenate([value_fn(j, h), ones], axis=0), p)
        alpha = stat(h, ROW_ALPHA)
        acc_buf[h] = alpha * acc_buf[h] + pv[:dv]
        set_stat(h, ROW_L, alpha * stat(h, ROW_L) + pv[dv:dv + 1])

    def softmax(h, s, s_max):
        m = stat(h, ROW_M)
        m_new = jnp.maximum(m, s_max)
        alpha = jnp.exp2(m - m_new)
        p = jnp.exp2(s - m_new)
        set_stat(h, ROW_M, m_new)
        set_stat(h, ROW_ALPHA, alpha)
        return p.astype(BF16)

    for h in heads:
        sc = score_fn(0, h)
        s_buf[h] = sc
        set_stat(h, ROW_SMAX, col_max(sc))
        set_stat(h, ROW_M, jnp.full((1, t), NEG_INF, F32))
        set_stat(h, ROW_L, jnp.zeros((1, t), F32))
        acc_buf[h] = jnp.zeros((dv, t), F32)

    def block_step(j):
        for h in heads:
            sc = score_fn(j + 1, h)
            values(h, j, softmax(h, s_buf[h], stat(h, ROW_SMAX)))
            s_buf[h] = sc
            set_stat(h, ROW_SMAX, col_max(sc))

    @pl.loop(0, i // FLASH_UNROLL)
    def _(jj):
        for u in range(FLASH_UNROLL):
            block_step(FLASH_UNROLL * jj + u)

    done = i - i % FLASH_UNROLL
    chunk = FLASH_UNROLL // 2
    while chunk >= n_diag:
        @pl.when(i % (2 * chunk) >= chunk)
        def _(done=done, chunk=chunk):
            for u in range(chunk):
                block_step(done + u)
        done = done + jnp.where(i % (2 * chunk) >= chunk, chunk, 0)
        chunk //= 2

    for d in range(n_diag):
        for h in heads:
            sc = score_fn(i + d + 1, h) if d + 1 < n_diag else None
            s = mask_fn(d, s_buf[h])
            values(h, i + d, softmax(h, s, col_max(s)))
            if sc is not None:
                s_buf[h] = sc
    return [acc_buf[h] / stat(h, ROW_L) for h in heads]


def _store_feature_major(o_ref, outs):
    dv = outs[0].shape[0]
    for h, out in enumerate(outs):
        o_ref[0, h * dv:(h + 1) * dv, :] = out.astype(o_ref.dtype)


def _flash_scratch(nh, tq, tk, dv):
    return [pltpu.VMEM((nh, tk, tq), F32), pltpu.VMEM((nh, dv, tq), F32),
            pltpu.VMEM((nh, SUBLANES, tq), F32)]


def _causal_masker(tq, tk):
    ahead = (lax.broadcasted_iota(jnp.int32, (tk, tq), 1)
             - lax.broadcasted_iota(jnp.int32, (tk, tq), 0))
    return lambda d, s: jnp.where(ahead >= d * tk, s, NEG_INF)


def _mla_kernel(qt_ref, k_ref, vt_ref, o_ref, *flash_scratch):
    tq, tk = ATT_TQ, ATT_TK
    n_diag = tq // tk
    nh = MLA_NH
    qs = [qt_ref[0, h * LANES:h * LANES + MLA_QK, :] for h in range(nh)]

    def scores(j, h):
        start = pl.multiple_of(j * tk, tk)
        return _dot(k_ref[0, pl.ds(start, tk), h * LANES:h * LANES + MLA_QK], qs[h])

    def values(j, h):
        return vt_ref[0, h * MLA_V:(h + 1) * MLA_V, pl.ds(pl.multiple_of(j * tk, tk), tk)]

    outs = _flash_heads(nh, n_diag * pl.program_id(2), n_diag, tq, tk, MLA_V, scores, values,
                        _causal_masker(tq, tk), *flash_scratch)
    _store_feature_major(o_ref, outs)


def _mla(mqt, mk, mvt):
    batch, seq, _ = mk.shape
    tq = ATT_TQ
    nh = MLA_NH
    return pl.pallas_call(
        _mla_kernel,
        grid=(batch, MLA_HEADS // nh, seq // tq),
        in_specs=[pl.BlockSpec((1, nh * LANES, tq), lambda b, g, i: (b, g, i)),
                  pl.BlockSpec((1, seq, nh * LANES), lambda b, g, i: (b, 0, g)),
                  pl.BlockSpec((1, nh * MLA_V, seq), lambda b, g, i: (b, g, 0))],
        out_specs=pl.BlockSpec((1, nh * MLA_V, tq), lambda b, g, i: (b, g, i)),
        out_shape=jax.ShapeDtypeStruct((batch, MLA_OUT, seq), BF16),
        scratch_shapes=_flash_scratch(nh, tq, ATT_TK, MLA_V),
        compiler_params=_cparams(("parallel", "parallel", "arbitrary")),
        name="mla_attn",
    )(mqt, mk, mvt)


def _moba_kernel(qt_ref, k_ref, vt_ref, o_ref, kmh_ref, kml_ref, *flash_scratch, nblk):
    i = pl.program_id(2)
    tq, tk = ATT_TQ, MOBA_BLOCK
    n_diag = tq // tk
    nh = MOBA_NH

    @pl.when(i == 0)
    def _():
        kf = k_ref[0].astype(F32).reshape(nblk, tk, nh * LANES)
        km = jnp.sum(kf, axis=1) * (1.0 / tk)
        hi = km.astype(BF16)
        kmh_ref[...] = hi
        kml_ref[...] = (km - hi.astype(F32)).astype(BF16)

    blk = lax.broadcasted_iota(jnp.int32, (nblk, 1), 0)
    blk_f = blk.astype(F32)
    own = n_diag * i + lax.broadcasted_iota(jnp.int32, (1, tq), 1) // tk
    pos_rows = lax.broadcasted_iota(jnp.int32, (AUX_SEL - AUX_POS, 1), 0) < AUX_POS_PARTS
    ones_rows = jnp.broadcast_to(jnp.where(pos_rows, 1.0, 0.0), (AUX_SEL - AUX_POS, tq)).astype(BF16)
    pad_q = jnp.zeros((LANES - HEAD_DIM, tq), BF16)
    qs = []
    for h in range(nh):
        q = qt_ref[0, h * HEAD_DIM:(h + 1) * HEAD_DIM, :]
        q_plain = jnp.concatenate([q, pad_q], axis=0)
        cols = slice(h * LANES, (h + 1) * LANES)
        gate = _dot(kmh_ref[:, cols], q_plain) + _dot(kml_ref[:, cols], q_plain)
        gate = jnp.where(blk < own, gate, NEG_INF)
        chosen = blk == own
        for _ in range(min(MOBA_TOPK, nblk)):
            best = jnp.max(gate, axis=0, keepdims=True)
            first = jnp.min(jnp.where(gate == best, blk_f, float(nblk)), axis=0, keepdims=True)
            pick = blk_f == first
            chosen = chosen | (pick & (blk < own))
            gate = jnp.where(pick, -jnp.inf, gate)
        sel = jnp.where(chosen, 0.0, NEG_INF).astype(BF16)
        qs.append(jnp.concatenate([q, ones_rows, sel], axis=0))

    def scores(j, h):
        start = pl.multiple_of(j * tk, tk)
        return _dot(k_ref[0, pl.ds(start, tk), h * LANES:h * LANES + AUX_SEL + nblk], qs[h])

    def values(j, h):
        return vt_ref[0, h * HEAD_DIM:(h + 1) * HEAD_DIM, pl.ds(pl.multiple_of(j * tk, tk), tk)]

    outs = _flash_heads(nh, n_diag * i, n_diag, tq, tk, HEAD_DIM, scores, values,
                        _causal_masker(tq, tk), *flash_scratch)
    _store_feature_major(o_ref, outs)


def _moba(bqt, bk, bvt):
    batch, seq, _ = bk.shape
    tq = ATT_TQ
    nh = MOBA_NH
    nblk = seq // MOBA_BLOCK
    assert AUX_SEL + nblk <= LANES
    cols = nh * HEAD_DIM
    return pl.pallas_call(
        functools.partial(_moba_kernel, nblk=nblk),
        grid=(batch, MOBA_HEADS // nh, seq // tq),
        in_specs=[
            pl.BlockSpec((1, cols, tq), lambda b, g, i: (b, g, i)),
            pl.BlockSpec((1, seq, nh * LANES), lambda b, g, i: (b, 0, g)),
            pl.BlockSpec((1, cols, seq), lambda b, g, i: (b, g, 0)),
        ],
        out_specs=pl.BlockSpec((1, cols, tq), lambda b, g, i: (b, g, i)),
        out_shape=jax.ShapeDtypeStruct((batch, MOBA_OUT, seq), BF16),
        scratch_shapes=[
            pltpu.VMEM((nblk, nh * LANES), BF16),
            pltpu.VMEM((nblk, nh * LANES), BF16),
        ] + _flash_scratch(nh, tq, MOBA_BLOCK, HEAD_DIM),
        compiler_params=_cparams(("parallel", "parallel", "arbitrary")),
        name="moba_attn",
    )(bqt, bk, bvt)


def _dil_kernel(own_ref, prev_ref, bias_ref, o_ref, lse_ref, *, ns, qb):
    n = pl.program_id(2)
    L = DIL_L
    low = lax.broadcasted_iota(jnp.int32, (1, LANES), 1) < HEAD_DIM
    first_cols = lax.broadcasted_iota(jnp.int32, (1, 2 * L), 1) < L
    problems = [(si, pair, blk) for si in range(ns) for pair in range(DIL_OUT // LANES)
                for blk in range(qb // L)]

    def window(si, blk, cols):
        if blk == 0:
            return jnp.concatenate([prev_ref[0, si, :, cols], own_ref[0, si, :L, cols]], axis=0)
        return own_ref[0, si, (blk - 1) * L:(blk + 1) * L, cols]

    scores = []
    for si, pair, blk in problems:
        qp = own_ref[0, si, blk * L:(blk + 1) * L, pair * LANES:(pair + 1) * LANES]
        zero = jnp.zeros_like(qp)
        keys = window(si, blk, slice(DIL_OUT + pair * LANES, DIL_OUT + (pair + 1) * LANES))
        scores.append([_dot_nt(jnp.where(low, qp, zero), keys),
                       _dot_nt(jnp.where(low, zero, qp), keys)])

    probs = []
    for (si, pair, blk), sc in zip(problems, scores):
        per_head = []
        for hh in range(2):
            s = sc[hh] + bias_ref[2 * pair + hh]
            if blk == 0:
                s = jnp.where(first_cols & (n == 0), NEG_INF, s)
            m = jnp.max(s, axis=-1, keepdims=True)
            e = jnp.exp(s - m)
            den = jnp.sum(e, axis=-1, keepdims=True)
            per_head.append((e.astype(BF16), den, m + jnp.log(den)))
        probs.append(per_head)

    for (si, pair, blk), per_head in zip(problems, probs):
        vals = window(si, blk, slice(2 * DIL_OUT + pair * LANES, 2 * DIL_OUT + (pair + 1) * LANES))
        o_pair = [_dot(e, vals) / den for e, den, _ in per_head]
        lse_pair = [jnp.broadcast_to(lse, (L, LANES)) for _, _, lse in per_head]
        dst = (0, si, slice(blk * L, (blk + 1) * L), slice(pair * LANES, (pair + 1) * LANES))
        o_ref[dst] = jnp.where(low, o_pair[0], o_pair[1])
        lse_ref[dst] = jnp.where(low, lse_pair[0], lse_pair[1])


def _dil(dg, bias):
    batch, dilation, rows, _ = dg.shape
    L = DIL_L
    qb = min(DIL_STEP_ROWS, rows)
    ns = min(DIL_STEP_ROWS // qb, dilation)
    per = qb // L
    out_sds = jax.ShapeDtypeStruct((batch, dilation, rows, DIL_OUT), F32)
    return pl.pallas_call(
        functools.partial(_dil_kernel, ns=ns, qb=qb),
        grid=(batch, dilation // ns, rows // qb),
        in_specs=[
            pl.BlockSpec((1, ns, qb, DIL_COLS), lambda b, r, n: (b, r, n, 0)),
            pl.BlockSpec((1, ns, L, DIL_COLS),
                         lambda b, r, n: (b, r, jnp.maximum(n * per - 1, 0), 0)),
            _resident(bias.shape),
        ],
        out_specs=[pl.BlockSpec((1, ns, qb, DIL_OUT), lambda b, r, n: (b, r, n, 0))] * 2,
        out_shape=[out_sds, out_sds],
        compiler_params=_cparams(("parallel", "parallel", "arbitrary")),
        name="dil_attn_d%d" % dilation,
    )(dg, dg, bias)


def _merge_kernel(x_ref, gn_ref, wgate_ref, a_ref, b_ref, o0_ref, o1_ref, o2_ref,
                  l0_ref, l1_ref, l2_ref, wa_ref, wb_ref, wc_ref, wo_ref, out_ref, *scratch):
    x = x_ref[0]
    h = _rms(x, gn_ref[...]).astype(BF16)
    tm = x.shape[0]
    scratch = list(scratch)

    def token_order(ref):
        dil = ref.shape[1]
        if dil == 1:
            return ref[0, 0]
        buf = scratch.pop()
        for c in range(buf.shape[0]):
            for r in range(dil):
                buf[c, pl.ds(r, tm // dil, stride=dil), :] = ref[0, r, :, c * LANES:(c + 1) * LANES]
        return jnp.concatenate([buf[c] for c in range(buf.shape[0])], axis=1)

    lses = [token_order(r) for r in (l0_ref, l1_ref, l2_ref)]
    outs = [token_order(r) for r in (o0_ref, o1_ref, o2_ref)]
    mx = jnp.maximum(jnp.maximum(lses[0], lses[1]), lses[2])
    es = [jnp.exp(l - mx) for l in lses]
    den = es[0] + es[1] + es[2]
    c = (es[0] / den) * outs[0] + (es[1] / den) * outs[1] + (es[2] / den) * outs[2]

    c16 = c.astype(BF16)
    a_t, b_t = a_ref[0], b_ref[0]
    chunks = []
    for n in range(D_MODEL // MERGE_CHUNK):
        cols = slice(n * MERGE_CHUNK, (n + 1) * MERGE_CHUNK)
        projected = (_dot_tn(a_t, wa_ref[:, cols]), _dot_tn(b_t, wb_ref[:, cols]),
                     _dot(c16, wc_ref[:, cols]))
        part = None
        for k, proj in enumerate(projected):
            gcols = slice(k * D_MODEL + n * MERGE_CHUNK, k * D_MODEL + (n + 1) * MERGE_CHUNK)
            term = jax.nn.sigmoid(_dot(h, wgate_ref[:, gcols])) * proj
            part = term if part is None else part + term
        chunks.append(part.astype(BF16))
    out_ref[0] = x + _dot(jnp.concatenate(chunks, axis=1), wo_ref[...])


def _merge(x3, gain, wgate, a, b, dil_outs, wa, wb, wc, wo):
    batch, seq, _ = x3.shape
    tm = TM_MERGE
    tok = lambda c: pl.BlockSpec((1, tm, c), lambda b, i: (b, i, 0))
    feat = lambda c: pl.BlockSpec((1, c, tm), lambda b, i: (b, 0, i))
    stream = lambda arr: pl.BlockSpec((1, arr.shape[1], tm // arr.shape[1], DIL_OUT),
                                      lambda b, i: (b, 0, i, 0))
    (o0, l0), (o1, l1), (o2, l2) = dil_outs
    streams = (o0, o1, o2, l0, l1, l2)
    n_reordered = sum(arr.shape[1] > 1 for arr in streams)
    return pl.pallas_call(
        _merge_kernel,
        grid=(batch, seq // tm),
        in_specs=[tok(D_MODEL), _resident((1, D_MODEL)), _resident(wgate.shape),
                  feat(MLA_OUT), feat(MOBA_OUT)]
        + [stream(arr) for arr in streams]
        + [_resident(wa.shape), _resident(wb.shape), _resident(wc.shape), _resident(wo.shape)],
        out_specs=tok(D_MODEL),
        out_shape=jax.ShapeDtypeStruct((batch, seq, D_MODEL), F32),
        scratch_shapes=[pltpu.VMEM((DIL_OUT // LANES, tm, LANES), F32)] * n_reordered,
        compiler_params=_cparams(("parallel", "parallel")),
        name="merge",
    )(x3, gain, wgate, a, b, *streams, wa, wb, wc, wo)


def _alibi_slopes():
    return (2.0 ** (-8.0 * np.arange(1, N_ALIBI + 1, dtype=np.float32) / N_ALIBI)).astype(np.float32)


def _rope_tables(seq):
    pos = jnp.arange(seq, dtype=F32)
    inv = ROPE_THETA ** (-jnp.arange(0, MLA_ROPE, 2, dtype=F32) / MLA_ROPE)
    ang = pos[:, None] * inv[None, :]
    cos, sin = jnp.cos(ang), jnp.sin(ang)
    pad = jnp.zeros((seq, LANES - MLA_QK), F32)
    cos_t = jnp.concatenate([jnp.ones((seq, MLA_NOPE), F32), cos, cos, pad], axis=1)
    sin_t = jnp.concatenate([jnp.zeros((seq, MLA_NOPE), F32), -sin, sin, pad], axis=1)
    return cos_t, sin_t, cos_t.T, sin_t.T


def _dil_bias(group, dilation):
    L = DIL_L
    slopes = _alibi_slopes()[group * DIL_GROUP_HEADS:(group + 1) * DIL_GROUP_HEADS]
    steps = L + np.arange(L)[:, None] - np.arange(2 * L)[None, :]
    valid = (steps >= 0) & (steps <= L)
    dist = (steps * dilation).astype(np.float32)
    bias = np.where(valid[None], -slopes[:, None, None] * dist[None], np.float32(NEG_INF))
    return jnp.asarray(bias.astype(np.float32))


def _moba_key_aux(seq):
    slopes = _alibi_slopes()[DIL_HEADS:].astype(np.float64) * LOG2E
    pos = jnp.asarray((slopes[None, :] * np.arange(seq)[:, None]).astype(np.float32))
    aux = jnp.zeros((seq, MOBA_HEADS, LANES), F32)
    rest = pos
    for part in range(AUX_POS_PARTS):
        piece = rest.astype(BF16).astype(F32)
        aux = aux.at[:, :, AUX_POS + part].set(piece)
        rest = rest - piece
    onehot = jax.nn.one_hot(jnp.arange(seq) // MOBA_BLOCK, seq // MOBA_BLOCK, dtype=F32)
    aux = aux.at[:, :, AUX_SEL:AUX_SEL + seq // MOBA_BLOCK].set(onehot[:, None, :])
    return aux.reshape(seq, MOBA_KPAD).astype(BF16)


def _layer_weights(w_in, w_uq, w_ukv, q_norm, kv_norm):
    d = w_in.shape[0]
    z = lambda c: jnp.zeros((d, c), w_in.dtype)
    i0 = MLA_Q_RANK + MLA_KV_RANK
    kr = w_in[:, i0:i0 + MLA_ROPE]
    hr = MLA_ROPE // 2
    kr_pad = jnp.concatenate([z(MLA_NOPE), kr, z(LANES - MLA_QK)], axis=1)
    kr_swap = jnp.concatenate([z(MLA_NOPE), kr[:, hr:], kr[:, :hr], z(LANES - MLA_QK)], axis=1)
    moba = w_in[:, MLA_IN:MLA_IN + MOBA_IN]
    bq, bk, bv = (moba[:, c * MOBA_OUT:(c + 1) * MOBA_OUT] for c in range(3))
    bk = jnp.concatenate([bk.reshape(d, MOBA_HEADS, HEAD_DIM),
                          jnp.zeros((d, MOBA_HEADS, LANES - HEAD_DIM), w_in.dtype)],
                         axis=2).reshape(d, MOBA_KPAD)
    dil = w_in[:, MLA_IN + MOBA_IN:MLA_IN + MOBA_IN + DIL_IN].reshape(d, 3, DIL_HEADS, HEAD_DIM)
    groups = [dil[:, :, g * DIL_GROUP_HEADS:(g + 1) * DIL_GROUP_HEADS].reshape(d, DIL_COLS)
              for g in range(len(DIL_PATTERNS))]
    w1 = jnp.concatenate([w_in[:, :i0], kr_pad, kr_swap, bk] + groups, axis=1).astype(BF16)
    wt = jnp.concatenate([bq, bv], axis=1).T.astype(BF16)
    wgate = w_in[:, MLA_IN + MOBA_IN + DIL_IN:].astype(BF16)

    r = w_uq.shape[0]
    uq = w_uq.reshape(r, MLA_HEADS, MLA_QK)
    zq = jnp.zeros((r, MLA_HEADS, LANES - MLA_QK), w_uq.dtype)
    zn = jnp.zeros((r, MLA_HEADS, MLA_NOPE), w_uq.dtype)
    uq_pad = jnp.concatenate([uq, zq], axis=2).reshape(r, MLA_QPAD)
    uq_swap = jnp.concatenate(
        [zn, uq[:, :, MLA_NOPE + hr:], uq[:, :, MLA_NOPE:MLA_NOPE + hr], zq], axis=2
    ).reshape(r, MLA_QPAD)
    wuqt = jnp.concatenate([uq_pad, uq_swap], axis=1).T.astype(BF16)

    rk = w_ukv.shape[0]
    ukv = w_ukv.reshape(rk, MLA_HEADS, MLA_NOPE + MLA_V)
    zk = jnp.zeros((rk, MLA_HEADS, LANES - MLA_NOPE), w_ukv.dtype)
    wk = jnp.concatenate([ukv[:, :, :MLA_NOPE], zk], axis=2).reshape(rk, MLA_QPAD).astype(BF16)
    wvt = ukv[:, :, MLA_NOPE:].reshape(rk, MLA_OUT).T.astype(BF16)
    return dict(w1=w1, wt=wt, wgate=wgate, wuqt=wuqt, wk=wk, wvt=wvt,
                qn=q_norm.reshape(1, -1), kvn=kv_norm.reshape(1, -1))


def kernel(x, ffn1_norm, ffn1_w_gate, ffn1_w_up, ffn1_w_down, mix_norm, w_in, q_norm, w_uq, kv_norm, w_ukv, w_br_mla, w_br_moba, w_br_dil, w_out, ffn2_norm, ffn2_w_gate, ffn2_w_up, ffn2_w_down, final_norm):
    batch, seq, d = x.shape
    assert d == D_MODEL and seq % (DIL_PATTERNS[-1][1] * DIL_L) == 0 and seq % TM_PROJ == 0
    n = batch * seq
    x2 = x.reshape(n, d)
    tables = _rope_tables(seq) + (_moba_key_aux(seq),)
    dil_bias = [_dil_bias(g, dil) for g, (_, dil) in enumerate(DIL_PATTERNS)]
    final_gain = final_norm.reshape(1, d)
    bf = lambda w: w.astype(BF16)
    row = lambda v: v.reshape(1, -1)

    for l in range(DEPTH):
        x2 = _ffn(x2, row(ffn1_norm[l]), bf(ffn1_w_gate[l]), bf(ffn1_w_up[l]), bf(ffn1_w_down[l]),
                  final_gain, False)
        lw = _layer_weights(w_in[l], w_uq[l], w_ukv[l], q_norm[l], kv_norm[l])
        mqt, mk, mvt, bqt, bk, bvt, d0, d1, d2 = _inproj(
            x2.reshape(batch, seq, d), row(mix_norm[l]), lw, tables)
        a = _mla(mqt, mk, mvt)
        b = _moba(bqt, bk, bvt)
        dil_outs = [_dil(dg, dil_bias[g]) for g, dg in enumerate((d0, d1, d2))]
        x2 = _merge(x2.reshape(batch, seq, d), row(mix_norm[l]), lw["wgate"], a, b, dil_outs,
                    bf(w_br_mla[l]), bf(w_br_moba[l]), bf(w_br_dil[l]), bf(w_out[l])).reshape(n, d)
        x2 = _ffn(x2, row(ffn2_norm[l]), bf(ffn2_w_gate[l]), bf(ffn2_w_up[l]), bf(ffn2_w_down[l]),
                  final_gain, l == DEPTH - 1)
    return x2.reshape(batch, seq, d)
```

```python
import functools

import numpy as np
import jax
import jax.numpy as jnp
from jax import lax
from jax.experimental import pallas as pl
from jax.experimental.pallas import tpu as pltpu

F32 = jnp.float32
BF16 = jnp.bfloat16

D_MODEL = 1024
DEPTH = 4
HEAD_DIM = 64
MLA_HEADS = 6
MLA_NOPE = 64
MLA_ROPE = 32
MLA_V = 64
MLA_Q_RANK = 256
MLA_KV_RANK = 128
ROPE_THETA = 10000.0
MOBA_HEADS = 6
MOBA_BLOCK = 256
MOBA_TOPK = 3
DIL_PATTERNS = ((128, 1), (512, 4), (2048, 16))
DIL_GROUP_HEADS = 4
DIL_HEADS = DIL_GROUP_HEADS * len(DIL_PATTERNS)
N_ALIBI = DIL_HEADS + MOBA_HEADS
N_BRANCHES = 3
D_FF = 2816
NORM_EPS = 1e-6
NEG_INF = -1e30
MLA_IN = MLA_Q_RANK + MLA_KV_RANK + MLA_ROPE
MOBA_IN = 3 * MOBA_HEADS * HEAD_DIM
DIL_IN = 3 * DIL_HEADS * HEAD_DIM
MLA_OUT = MLA_HEADS * MLA_V
MOBA_OUT = MOBA_HEADS * HEAD_DIM
DIL_OUT = DIL_GROUP_HEADS * HEAD_DIM

LANES = 128
SUBLANES = 8
BF16_ROWS = 16
MLA_QK = MLA_NOPE + MLA_ROPE
MLA_QPAD = MLA_HEADS * LANES
LOG2E = 1.4426950408889634
MLA_SCALE = MLA_QK ** -0.5
HEAD_SCALE = HEAD_DIM ** -0.5
DIL_L = DIL_PATTERNS[0][0] // DIL_PATTERNS[0][1]
DIL_COLS = 3 * DIL_GROUP_HEADS * HEAD_DIM

C_CQ = 0
C_CKV = C_CQ + MLA_Q_RANK
C_KR = C_CKV + MLA_KV_RANK
C_KR2 = C_KR + LANES
C_MOBA_K = C_KR2 + LANES
MOBA_KPAD = MOBA_HEADS * LANES
C_DIL = C_MOBA_K + MOBA_KPAD
C_END = C_DIL + DIL_IN

AUX_POS = HEAD_DIM
AUX_POS_PARTS = 3
AUX_SEL = HEAD_DIM + BF16_ROWS

VMEM_LIMIT = 56 * 1024 * 1024

TM_FFN = 512
TM_PROJ = 512
TM_MERGE = 512
FF_CHUNK = 256
MERGE_CHUNK = 256
ATT_TQ = 512
ATT_TK = MOBA_BLOCK
MLA_NH = 6
MOBA_NH = 6
DIL_STEP_ROWS = 1024


def _cparams(sem):
    return pltpu.CompilerParams(dimension_semantics=sem, vmem_limit_bytes=VMEM_LIMIT)


def _resident(shape):
    nd = len(shape)
    return pl.BlockSpec(shape, lambda *_: (0,) * nd, pipeline_mode=pl.Buffered(1))


def _layer_block(stack, layer):
    shape = stack.shape[1:]
    return pl.BlockSpec((None,) + shape, lambda *_: (layer,) + (0,) * len(shape),
                        pipeline_mode=pl.Buffered(1))


def _rms(x, gain):
    ms = jnp.mean(x * x, axis=-1, keepdims=True)
    return x * lax.rsqrt(ms + NORM_EPS) * gain


def _dot(a, b):
    return jnp.dot(a, b, preferred_element_type=F32)


def _dot_nt(a, b):
    return lax.dot_general(a, b, (((1,), (1,)), ((), ())), preferred_element_type=F32)


def _dot_tn(a, b):
    return lax.dot_general(a, b, (((0,), (0,)), ((), ())), preferred_element_type=F32)


def _ffn_kernel(x_ref, g_ref, wg_ref, wu_ref, wd_ref, fg_ref, o_ref, *, final):
    x = x_ref[...]
    h = _rms(x, g_ref[...]).astype(BF16)
    acc = jnp.zeros(x.shape, F32)
    for c in range(D_FF // FF_CHUNK):
        sl = slice(c * FF_CHUNK, (c + 1) * FF_CHUNK)
        a = _dot(h, wg_ref[:, sl])
        u = _dot(h, wu_ref[:, sl])
        g = (a * jax.nn.sigmoid(a) * u).astype(BF16)
        acc = acc + _dot(g, wd_ref[sl, :])
    y = x + 0.5 * acc
    if final:
        y = _rms(y, fg_ref[...])
    o_ref[...] = y


def _ffn(x2, layer, gain, wg, wu, wd, final_gain, final):
    n = x2.shape[0]
    return pl.pallas_call(
        functools.partial(_ffn_kernel, final=final),
        grid=(n // TM_FFN,),
        in_specs=[
            pl.BlockSpec((TM_FFN, D_MODEL), lambda i: (i, 0)),
            _layer_block(gain, layer),
            _layer_block(wg, layer),
            _layer_block(wu, layer),
            _layer_block(wd, layer),
            _resident((1, D_MODEL)),
        ],
        out_specs=pl.BlockSpec((TM_FFN, D_MODEL), lambda i: (i, 0)),
        out_shape=jax.ShapeDtypeStruct((n, D_MODEL), F32),
        compiler_params=_cparams(("parallel",)),
        name="ffn_final" if final else "ffn",
    )(x2, gain, wg, wu, wd, final_gain)


def _inproj_kernel(x_ref, gn_ref, w1_ref, wt_ref, qn_ref, wuqt_ref, kvn_ref, wk_ref, wvt_ref,
                   cos_ref, sin_ref, cost_ref, sint_ref, kaux_ref,
                   mqt_ref, mk_ref, mvt_ref, bqt_ref, bk_ref, bvt_ref, d0_ref, d1_ref, d2_ref,
                   stage_ref):
    x = x_ref[0]
    h = _rms(x, gn_ref[...]).astype(BF16)

    pm = _dot(h, w1_ref[:, C_CQ:C_MOBA_K])
    cq = _rms(pm[:, C_CQ:C_CKV], qn_ref[...]).astype(BF16)
    ckv = _rms(pm[:, C_CKV:C_KR], kvn_ref[...]).astype(BF16)
    k_rope = pm[:, C_KR:C_KR2] * cos_ref[...] + pm[:, C_KR2:C_MOBA_K] * sin_ref[...]
    kk = _dot(ckv, wk_ref[...])
    qqt = _dot_nt(wuqt_ref[...], cq)
    cost = cost_ref[...]
    sint = sint_ref[...]
    for hh in range(MLA_HEADS):
        sl = slice(hh * LANES, (hh + 1) * LANES)
        sl2 = slice(MLA_QPAD + hh * LANES, MLA_QPAD + (hh + 1) * LANES)
        mqt_ref[0, sl, :] = ((qqt[sl] * cost + qqt[sl2] * sint) * (MLA_SCALE * LOG2E)).astype(BF16)
        mk_ref[0, :, sl] = (kk[:, sl] + k_rope).astype(BF16)
    mvt_ref[0] = _dot_nt(wvt_ref[...], ckv).astype(BF16)

    bt = _dot_nt(wt_ref[...], h)
    bqt_ref[0] = (bt[:MOBA_OUT] * (HEAD_SCALE * LOG2E)).astype(BF16)
    bvt_ref[0] = bt[MOBA_OUT:].astype(BF16)
    bk_ref[0] = (_dot(h, w1_ref[:, C_MOBA_K:C_DIL]) + kaux_ref[...].astype(F32)).astype(BF16)

    tm = x.shape[0]
    for g, d_ref in enumerate((d0_ref, d1_ref, d2_ref)):
        dil = DIL_PATTERNS[g][1]
        dd = _dot(h, w1_ref[:, C_DIL + g * DIL_COLS:C_DIL + (g + 1) * DIL_COLS])
        for c in range(DIL_COLS // LANES):
            cols = slice(c * LANES, (c + 1) * LANES)
            blk = dd[:, cols]
            stage_ref[c] = blk * HEAD_SCALE if c < DIL_OUT // LANES else blk
            for r in range(dil):
                d_ref[0, r, :, cols] = stage_ref[c, pl.ds(r, tm // dil, stride=dil), :].astype(BF16)


def _inproj(x3, gain, lw, tables):
    batch, seq, _ = x3.shape
    tm = TM_PROJ
    cos_t, sin_t, cos_tt, sin_tt, kaux = tables
    tok = lambda c: pl.BlockSpec((1, tm, c), lambda b, i: (b, i, 0))
    feat = lambda c: pl.BlockSpec((1, c, tm), lambda b, i: (b, 0, i))
    tok_sds = lambda c: jax.ShapeDtypeStruct((batch, seq, c), BF16)
    feat_sds = lambda c: jax.ShapeDtypeStruct((batch, c, seq), BF16)
    dils = [dil for _, dil in DIL_PATTERNS]
    stream = lambda dil: pl.BlockSpec((1, dil, tm // dil, DIL_COLS), lambda b, i: (b, 0, i, 0))
    stream_sds = lambda dil: jax.ShapeDtypeStruct((batch, dil, seq // dil, DIL_COLS), BF16)
    return pl.pallas_call(
        _inproj_kernel,
        grid=(batch, seq // tm),
        in_specs=[
            tok(D_MODEL),
            _resident((1, D_MODEL)),
            _resident(lw["w1"].shape),
            _resident(lw["wt"].shape),
            _resident((1, MLA_Q_RANK)),
            _resident(lw["wuqt"].shape),
            _resident((1, MLA_KV_RANK)),
            _resident(lw["wk"].shape),
            _resident(lw["wvt"].shape),
            pl.BlockSpec((tm, LANES), lambda b, i: (i, 0)),
            pl.BlockSpec((tm, LANES), lambda b, i: (i, 0)),
            pl.BlockSpec((LANES, tm), lambda b, i: (0, i)),
            pl.BlockSpec((LANES, tm), lambda b, i: (0, i)),
            pl.BlockSpec((tm, MOBA_KPAD), lambda b, i: (i, 0)),
        ],
        out_specs=[feat(MLA_QPAD), tok(MLA_QPAD), feat(MLA_OUT),
                   feat(MOBA_OUT), tok(MOBA_KPAD), feat(MOBA_OUT)] + [stream(dil) for dil in dils],
        out_shape=[feat_sds(MLA_QPAD), tok_sds(MLA_QPAD), feat_sds(MLA_OUT),
                   feat_sds(MOBA_OUT), tok_sds(MOBA_KPAD), feat_sds(MOBA_OUT)]
        + [stream_sds(dil) for dil in dils],
        scratch_shapes=[pltpu.VMEM((DIL_COLS // LANES, tm, LANES), F32)],
        compiler_params=_cparams(("parallel", "parallel")),
        name="inproj",
    )(x3, gain, lw["w1"], lw["wt"], lw["qn"], lw["wuqt"], lw["kvn"], lw["wk"], lw["wvt"],
      cos_t, sin_t, cos_tt, sin_tt, kaux)


ROW_M, ROW_ALPHA, ROW_L, ROW_SMAX = range(4)
FLASH_UNROLL = 8
SUM_ROWS = 16


def _flash_heads(nh, n_full, n_diag, tq, tk, dv, score_fn, value_fn, mask_fn,
                 s_buf, acc_buf, stat_buf):
    i, t = n_full, tq
    heads = range(nh)
    ones = jnp.ones((SUM_ROWS, tk), BF16)
    col_max = lambda s: jnp.max(s, axis=0, keepdims=True)
    stat = lambda h, r: stat_buf[h, r:r + 1, :]

    def set_stat(h, r, v):
        stat_buf[h, r:r + 1, :] = v

    def values(h, j, p):
        pv = _dot(jnp.concatenate([value_fn(j, h), ones], axis=0), p)
        alpha = stat(h, ROW_ALPHA)
        acc_buf[h] = alpha * acc_buf[h] + pv[:dv]
        set_stat(h, ROW_L, alpha * stat(h, ROW_L) + pv[dv:dv + 1])

    def softmax(h, s, s_max):
        m = stat(h, ROW_M)
        m_new = jnp.maximum(m, s_max)
        alpha = jnp.exp2(m - m_new)
        p = jnp.exp2(s - m_new)
        set_stat(h, ROW_M, m_new)
        set_stat(h, ROW_ALPHA, alpha)
        return p.astype(BF16)

    for h in heads:
        sc = score_fn(0, h)
        s_buf[h] = sc
        set_stat(h, ROW_SMAX, col_max(sc))
        set_stat(h, ROW_M, jnp.full((1, t), NEG_INF, F32))
        set_stat(h, ROW_L, jnp.zeros((1, t), F32))
        acc_buf[h] = jnp.zeros((dv, t), F32)

    def block_step(j):
        for h in heads:
            sc = score_fn(j + 1, h)
            values(h, j, softmax(h, s_buf[h], stat(h, ROW_SMAX)))
            s_buf[h] = sc
            set_stat(h, ROW_SMAX, col_max(sc))

    @pl.loop(0, i // FLASH_UNROLL)
    def _(jj):
        for u in range(FLASH_UNROLL):
            block_step(FLASH_UNROLL * jj + u)

    done = i - i % FLASH_UNROLL
    chunk = FLASH_UNROLL // 2
    while chunk >= n_diag:
        @pl.when(i % (2 * chunk) >= chunk)
        def _(done=done, chunk=chunk):
            for u in range(chunk):
                block_step(done + u)
        done = done + jnp.where(i % (2 * chunk) >= chunk, chunk, 0)
        chunk //= 2

    for d in range(n_diag):
        for h in heads:
            sc = score_fn(i + d + 1, h) if d + 1 < n_diag else None
            s = mask_fn(d, s_buf[h])
            values(h, i + d, softmax(h, s, col_max(s)))
            if sc is not None:
                s_buf[h] = sc
    return [acc_buf[h] / stat(h, ROW_L) for h in heads]


def _store_feature_major(o_ref, outs):
    dv = outs[0].shape[0]
    for h, out in enumerate(outs):
        o_ref[0, h * dv:(h + 1) * dv, :] = out.astype(o_ref.dtype)


def _flash_scratch(nh, tq, tk, dv):
    return [pltpu.VMEM((nh, tk, tq), F32), pltpu.VMEM((nh, dv, tq), F32),
            pltpu.VMEM((nh, SUBLANES, tq), F32)]


def _causal_masker(tq, tk):
    ahead = (lax.broadcasted_iota(jnp.int32, (tk, tq), 1)
             - lax.broadcasted_iota(jnp.int32, (tk, tq), 0))
    return lambda d, s: jnp.where(ahead >= d * tk, s, NEG_INF)


def _mla_kernel(qt_ref, k_ref, vt_ref, o_ref, *flash_scratch):
    tq, tk = ATT_TQ, ATT_TK
    n_diag = tq // tk
    nh = MLA_NH
    qs = [qt_ref[0, h * LANES:h * LANES + MLA_QK, :] for h in range(nh)]

    def scores(j, h):
        start = pl.multiple_of(j * tk, tk)
        return _dot(k_ref[0, pl.ds(start, tk), h * LANES:h * LANES + MLA_QK], qs[h])

    def values(j, h):
        return vt_ref[0, h * MLA_V:(h + 1) * MLA_V, pl.ds(pl.multiple_of(j * tk, tk), tk)]

    outs = _flash_heads(nh, n_diag * pl.program_id(2), n_diag, tq, tk, MLA_V, scores, values,
                        _causal_masker(tq, tk), *flash_scratch)
    _store_feature_major(o_ref, outs)


def _mla(mqt, mk, mvt):
    batch, seq, _ = mk.shape
    tq = ATT_TQ
    nh = MLA_NH
    return pl.pallas_call(
        _mla_kernel,
        grid=(batch, MLA_HEADS // nh, seq // tq),
        in_specs=[pl.BlockSpec((1, nh * LANES, tq), lambda b, g, i: (b, g, i)),
                  pl.BlockSpec((1, seq, nh * LANES), lambda b, g, i: (b, 0, g)),
                  pl.BlockSpec((1, nh * MLA_V, seq), lambda b, g, i: (b, g, 0))],
        out_specs=pl.BlockSpec((1, nh * MLA_V, tq), lambda b, g, i: (b, g, i)),
        out_shape=jax.ShapeDtypeStruct((batch, MLA_OUT, seq), BF16),
        scratch_shapes=_flash_scratch(nh, tq, ATT_TK, MLA_V),
        compiler_params=_cparams(("parallel", "parallel", "arbitrary")),
        name="mla_attn",
    )(mqt, mk, mvt)


def _moba_kernel(qt_ref, k_ref, vt_ref, o_ref, kmh_ref, kml_ref, *flash_scratch, nblk):
    i = pl.program_id(2)
    tq, tk = ATT_TQ, MOBA_BLOCK
    n_diag = tq // tk
    nh = MOBA_NH

    @pl.when(i == 0)
    def _():
        kf = k_ref[0].astype(F32).reshape(nblk, tk, nh * LANES)
        km = jnp.sum(kf, axis=1) * (1.0 / tk)
        hi = km.astype(BF16)
        kmh_ref[...] = hi
        kml_ref[...] = (km - hi.astype(F32)).astype(BF16)

    blk = lax.broadcasted_iota(jnp.int32, (nblk, 1), 0)
    blk_f = blk.astype(F32)
    own = n_diag * i + lax.broadcasted_iota(jnp.int32, (1, tq), 1) // tk
    pos_rows = lax.broadcasted_iota(jnp.int32, (AUX_SEL - AUX_POS, 1), 0) < AUX_POS_PARTS
    ones_rows = jnp.broadcast_to(jnp.where(pos_rows, 1.0, 0.0), (AUX_SEL - AUX_POS, tq)).astype(BF16)
    pad_q = jnp.zeros((LANES - HEAD_DIM, tq), BF16)
    qs = []
    for h in range(nh):
        q = qt_ref[0, h * HEAD_DIM:(h + 1) * HEAD_DIM, :]
        q_plain = jnp.concatenate([q, pad_q], axis=0)
        cols = slice(h * LANES, (h + 1) * LANES)
        gate = _dot(kmh_ref[:, cols], q_plain) + _dot(kml_ref[:, cols], q_plain)
        gate = jnp.where(blk < own, gate, NEG_INF)
        chosen = blk == own
        for _ in range(min(MOBA_TOPK, nblk)):
            best = jnp.max(gate, axis=0, keepdims=True)
            first = jnp.min(jnp.where(gate == best, blk_f, float(nblk)), axis=0, keepdims=True)
            pick = blk_f == first
            chosen = chosen | (pick & (blk < own))
            gate = jnp.where(pick, -jnp.inf, gate)
        sel = jnp.where(chosen, 0.0, NEG_INF).astype(BF16)
        qs.append(jnp.concatenate([q, ones_rows, sel], axis=0))

    def scores(j, h):
        start = pl.multiple_of(j * tk, tk)
        return _dot(k_ref[0, pl.ds(start, tk), h * LANES:h * LANES + AUX_SEL + nblk], qs[h])

    def values(j, h):
        return vt_ref[0, h * HEAD_DIM:(h + 1) * HEAD_DIM, pl.ds(pl.multiple_of(j * tk, tk), tk)]

    outs = _flash_heads(nh, n_diag * i, n_diag, tq, tk, HEAD_DIM, scores, values,
                        _causal_masker(tq, tk), *flash_scratch)
    _store_feature_major(o_ref, outs)


def _moba(bqt, bk, bvt):
    batch, seq, _ = bk.shape
    tq = ATT_TQ
    nh = MOBA_NH
    nblk = seq // MOBA_BLOCK
    assert AUX_SEL + nblk <= LANES
    cols = nh * HEAD_DIM
    return pl.pallas_call(
        functools.partial(_moba_kernel, nblk=nblk),
        grid=(batch, MOBA_HEADS // nh, seq // tq),
        in_specs=[
            pl.BlockSpec((1, cols, tq), lambda b, g, i: (b, g, i)),
            pl.BlockSpec((1, seq, nh * LANES), lambda b, g, i: (b, 0, g)),
            pl.BlockSpec((1, cols, seq), lambda b, g, i: (b, g, 0)),
        ],
        out_specs=pl.BlockSpec((1, cols, tq), lambda b, g, i: (b, g, i)),
        out_shape=jax.ShapeDtypeStruct((batch, MOBA_OUT, seq), BF16),
        scratch_shapes=[
            pltpu.VMEM((nblk, nh * LANES), BF16),
            pltpu.VMEM((nblk, nh * LANES), BF16),
        ] + _flash_scratch(nh, tq, MOBA_BLOCK, HEAD_DIM),
        compiler_params=_cparams(("parallel", "parallel", "arbitrary")),
        name="moba_attn",
    )(bqt, bk, bvt)


def _dil_kernel(own_ref, prev_ref, bias_ref, o_ref, lse_ref, *, ns, qb):
    n = pl.program_id(2)
    L = DIL_L
    low = lax.broadcasted_iota(jnp.int32, (1, LANES), 1) < HEAD_DIM
    first_cols = lax.broadcasted_iota(jnp.int32, (1, 2 * L), 1) < L
    problems = [(si, pair, blk) for si in range(ns) for pair in range(DIL_OUT // LANES)
                for blk in range(qb // L)]

    def window(si, blk, cols):
        if blk == 0:
            return jnp.concatenate([prev_ref[0, si, :, cols], own_ref[0, si, :L, cols]], axis=0)
        return own_ref[0, si, (blk - 1) * L:(blk + 1) * L, cols]

    scores = []
    for si, pair, blk in problems:
        qp = own_ref[0, si, blk * L:(blk + 1) * L, pair * LANES:(pair + 1) * LANES]
        zero = jnp.zeros_like(qp)
        keys = window(si, blk, slice(DIL_OUT + pair * LANES, DIL_OUT + (pair + 1) * LANES))
        scores.append([_dot_nt(jnp.where(low, qp, zero), keys),
                       _dot_nt(jnp.where(low, zero, qp), keys)])

    probs = []
    for (si, pair, blk), sc in zip(problems, scores):
        per_head = []
        for hh in range(2):
            s = sc[hh] + bias_ref[2 * pair + hh]
            if blk == 0:
                s = jnp.where(first_cols & (n == 0), NEG_INF, s)
            m = jnp.max(s, axis=-1, keepdims=True)
            e = jnp.exp(s - m)
            den = jnp.sum(e, axis=-1, keepdims=True)
            per_head.append((e.astype(BF16), den, m + jnp.log(den)))
        probs.append(per_head)

    for (si, pair, blk), per_head in zip(problems, probs):
        vals = window(si, blk, slice(2 * DIL_OUT + pair * LANES, 2 * DIL_OUT + (pair + 1) * LANES))
        o_pair = [_dot(e, vals) / den for e, den, _ in per_head]
        lse_pair = [jnp.broadcast_to(lse, (L, LANES)) for _, _, lse in per_head]
        dst = (0, si, slice(blk * L, (blk + 1) * L), slice(pair * LANES, (pair + 1) * LANES))
        o_ref[dst] = jnp.where(low, o_pair[0], o_pair[1])
        lse_ref[dst] = jnp.where(low, lse_pair[0], lse_pair[1])


def _dil(dg, bias):
    batch, dilation, rows, _ = dg.shape
    L = DIL_L
    qb = min(DIL_STEP_ROWS, rows)
    ns = min(DIL_STEP_ROWS // qb, dilation)
    per = qb // L
    out_sds = jax.ShapeDtypeStruct((batch, dilation, rows, DIL_OUT), F32)
    return pl.pallas_call(
        functools.partial(_dil_kernel, ns=ns, qb=qb),
        grid=(batch, dilation // ns, rows // qb),
        in_specs=[
            pl.BlockSpec((1, ns, qb, DIL_COLS), lambda b, r, n: (b, r, n, 0)),
            pl.BlockSpec((1, ns, L, DIL_COLS),
                         lambda b, r, n: (b, r, jnp.maximum(n * per - 1, 0), 0)),
            _resident(bias.shape),
        ],
        out_specs=[pl.BlockSpec((1, ns, qb, DIL_OUT), lambda b, r, n: (b, r, n, 0))] * 2,
        out_shape=[out_sds, out_sds],
        compiler_params=_cparams(("parallel", "parallel", "arbitrary")),
        name="dil_attn_d%d" % dilation,
    )(dg, dg, bias)


def _merge_kernel(x_ref, gn_ref, wgate_ref, a_ref, b_ref, o0_ref, o1_ref, o2_ref,
                  l0_ref, l1_ref, l2_ref, wa_ref, wb_ref, wc_ref, wo_ref, out_ref, *scratch):
    x = x_ref[0]
    h = _rms(x, gn_ref[...]).astype(BF16)
    tm = x.shape[0]
    scratch = list(scratch)

    def token_order(ref):
        dil = ref.shape[1]
        if dil == 1:
            return ref[0, 0]
        buf = scratch.pop()
        for c in range(buf.shape[0]):
            for r in range(dil):
                buf[c, pl.ds(r, tm // dil, stride=dil), :] = ref[0, r, :, c * LANES:(c + 1) * LANES]
        return jnp.concatenate([buf[c] for c in range(buf.shape[0])], axis=1)

    lses = [token_order(r) for r in (l0_ref, l1_ref, l2_ref)]
    outs = [token_order(r) for r in (o0_ref, o1_ref, o2_ref)]
    mx = jnp.maximum(jnp.maximum(lses[0], lses[1]), lses[2])
    es = [jnp.exp(l - mx) for l in lses]
    den = es[0] + es[1] + es[2]
    c = (es[0] / den) * outs[0] + (es[1] / den) * outs[1] + (es[2] / den) * outs[2]

    c16 = c.astype(BF16)
    a_t, b_t = a_ref[0], b_ref[0]
    chunks = []
    for n in range(D_MODEL // MERGE_CHUNK):
        cols = slice(n * MERGE_CHUNK, (n + 1) * MERGE_CHUNK)
        projected = (_dot_tn(a_t, wa_ref[:, cols]), _dot_tn(b_t, wb_ref[:, cols]),
                     _dot(c16, wc_ref[:, cols]))
        part = None
        for k, proj in enumerate(projected):
            gcols = slice(k * D_MODEL + n * MERGE_CHUNK, k * D_MODEL + (n + 1) * MERGE_CHUNK)
            term = jax.nn.sigmoid(_dot(h, wgate_ref[:, gcols])) * proj
            part = term if part is None else part + term
        chunks.append(part.astype(BF16))
    out_ref[0] = x + _dot(jnp.concatenate(chunks, axis=1), wo_ref[...])


def _merge(x3, layer, gain, wgate, a, b, dil_outs, wa, wb, wc, wo):
    batch, seq, _ = x3.shape
    tm = TM_MERGE
    tok = lambda c: pl.BlockSpec((1, tm, c), lambda b, i: (b, i, 0))
    feat = lambda c: pl.BlockSpec((1, c, tm), lambda b, i: (b, 0, i))
    stream = lambda arr: pl.BlockSpec((1, arr.shape[1], tm // arr.shape[1], DIL_OUT),
                                      lambda b, i: (b, 0, i, 0))
    (o0, l0), (o1, l1), (o2, l2) = dil_outs
    streams = (o0, o1, o2, l0, l1, l2)
    n_reordered = sum(arr.shape[1] > 1 for arr in streams)
    return pl.pallas_call(
        _merge_kernel,
        grid=(batch, seq // tm),
        in_specs=[tok(D_MODEL), _layer_block(gain, layer), _resident(wgate.shape),
                  feat(MLA_OUT), feat(MOBA_OUT)]
        + [stream(arr) for arr in streams]
        + [_layer_block(w, layer) for w in (wa, wb, wc, wo)],
        out_specs=tok(D_MODEL),
        out_shape=jax.ShapeDtypeStruct((batch, seq, D_MODEL), F32),
        scratch_shapes=[pltpu.VMEM((DIL_OUT // LANES, tm, LANES), F32)] * n_reordered,
        compiler_params=_cparams(("parallel", "parallel")),
        name="merge",
    )(x3, gain, wgate, a, b, *streams, wa, wb, wc, wo)


def _alibi_slopes():
    return (2.0 ** (-8.0 * np.arange(1, N_ALIBI + 1, dtype=np.float32) / N_ALIBI)).astype(np.float32)


def _rope_tables(seq):
    pos = jnp.arange(seq, dtype=F32)
    inv = ROPE_THETA ** (-jnp.arange(0, MLA_ROPE, 2, dtype=F32) / MLA_ROPE)
    ang = pos[:, None] * inv[None, :]
    cos, sin = jnp.cos(ang), jnp.sin(ang)
    pad = jnp.zeros((seq, LANES - MLA_QK), F32)
    cos_t = jnp.concatenate([jnp.ones((seq, MLA_NOPE), F32), cos, cos, pad], axis=1)
    sin_t = jnp.concatenate([jnp.zeros((seq, MLA_NOPE), F32), -sin, sin, pad], axis=1)
    return cos_t, sin_t, cos_t.T, sin_t.T


def _dil_bias(group, dilation):
    L = DIL_L
    slopes = _alibi_slopes()[group * DIL_GROUP_HEADS:(group + 1) * DIL_GROUP_HEADS]
    steps = L + np.arange(L)[:, None] - np.arange(2 * L)[None, :]
    valid = (steps >= 0) & (steps <= L)
    dist = (steps * dilation).astype(np.float32)
    bias = np.where(valid[None], -slopes[:, None, None] * dist[None], np.float32(NEG_INF))
    return jnp.asarray(bias.astype(np.float32))


def _moba_key_aux(seq):
    slopes = _alibi_slopes()[DIL_HEADS:].astype(np.float64) * LOG2E
    pos = jnp.asarray((slopes[None, :] * np.arange(seq)[:, None]).astype(np.float32))
    aux = jnp.zeros((seq, MOBA_HEADS, LANES), F32)
    rest = pos
    for part in range(AUX_POS_PARTS):
        piece = rest.astype(BF16).astype(F32)
        aux = aux.at[:, :, AUX_POS + part].set(piece)
        rest = rest - piece
    onehot = jax.nn.one_hot(jnp.arange(seq) // MOBA_BLOCK, seq // MOBA_BLOCK, dtype=F32)
    aux = aux.at[:, :, AUX_SEL:AUX_SEL + seq // MOBA_BLOCK].set(onehot[:, None, :])
    return aux.reshape(seq, MOBA_KPAD).astype(BF16)


def _layer_weights(w_in, w_uq, w_ukv, q_norm, kv_norm):
    d = w_in.shape[0]
    z = lambda c: jnp.zeros((d, c), w_in.dtype)
    i0 = MLA_Q_RANK + MLA_KV_RANK
    kr = w_in[:, i0:i0 + MLA_ROPE]
    hr = MLA_ROPE // 2
    kr_pad = jnp.concatenate([z(MLA_NOPE), kr, z(LANES - MLA_QK)], axis=1)
    kr_swap = jnp.concatenate([z(MLA_NOPE), kr[:, hr:], kr[:, :hr], z(LANES - MLA_QK)], axis=1)
    moba = w_in[:, MLA_IN:MLA_IN + MOBA_IN]
    bq, bk, bv = (moba[:, c * MOBA_OUT:(c + 1) * MOBA_OUT] for c in range(3))
    bk = jnp.concatenate([bk.reshape(d, MOBA_HEADS, HEAD_DIM),
                          jnp.zeros((d, MOBA_HEADS, LANES - HEAD_DIM), w_in.dtype)],
                         axis=2).reshape(d, MOBA_KPAD)
    dil = w_in[:, MLA_IN + MOBA_IN:MLA_IN + MOBA_IN + DIL_IN].reshape(d, 3, DIL_HEADS, HEAD_DIM)
    groups = [dil[:, :, g * DIL_GROUP_HEADS:(g + 1) * DIL_GROUP_HEADS].reshape(d, DIL_COLS)
              for g in range(len(DIL_PATTERNS))]
    w1 = jnp.concatenate([w_in[:, :i0], kr_pad, kr_swap, bk] + groups, axis=1).astype(BF16)
    wt = jnp.concatenate([bq, bv], axis=1).T.astype(BF16)
    wgate = w_in[:, MLA_IN + MOBA_IN + DIL_IN:].astype(BF16)

    r = w_uq.shape[0]
    uq = w_uq.reshape(r, MLA_HEADS, MLA_QK)
    zq = jnp.zeros((r, MLA_HEADS, LANES - MLA_QK), w_uq.dtype)
    zn = jnp.zeros((r, MLA_HEADS, MLA_NOPE), w_uq.dtype)
    uq_pad = jnp.concatenate([uq, zq], axis=2).reshape(r, MLA_QPAD)
    uq_swap = jnp.concatenate(
        [zn, uq[:, :, MLA_NOPE + hr:], uq[:, :, MLA_NOPE:MLA_NOPE + hr], zq], axis=2
    ).reshape(r, MLA_QPAD)
    wuqt = jnp.concatenate([uq_pad, uq_swap], axis=1).T.astype(BF16)

    rk = w_ukv.shape[0]
    ukv = w_ukv.reshape(rk, MLA_HEADS, MLA_NOPE + MLA_V)
    zk = jnp.zeros((rk, MLA_HEADS, LANES - MLA_NOPE), w_ukv.dtype)
    wk = jnp.concatenate([ukv[:, :, :MLA_NOPE], zk], axis=2).reshape(rk, MLA_QPAD).astype(BF16)
    wvt = ukv[:, :, MLA_NOPE:].reshape(rk, MLA_OUT).T.astype(BF16)
    return dict(w1=w1, wt=wt, wgate=wgate, wuqt=wuqt, wk=wk, wvt=wvt,
                qn=q_norm.reshape(1, -1), kvn=kv_norm.reshape(1, -1))


def kernel(x, ffn1_norm, ffn1_w_gate, ffn1_w_up, ffn1_w_down, mix_norm, w_in, q_norm, w_uq, kv_norm, w_ukv, w_br_mla, w_br_moba, w_br_dil, w_out, ffn2_norm, ffn2_w_gate, ffn2_w_up, ffn2_w_down, final_norm):
    batch, seq, d = x.shape
    assert d == D_MODEL and seq % (DIL_PATTERNS[-1][1] * DIL_L) == 0 and seq % TM_PROJ == 0
    n = batch * seq
    x2 = x.reshape(n, d)
    tables = _rope_tables(seq) + (_moba_key_aux(seq),)
    dil_bias = [_dil_bias(g, dil) for g, (_, dil) in enumerate(DIL_PATTERNS)]
    final_gain = final_norm.reshape(1, d)
    bf = lambda w: w.astype(BF16)
    rows = lambda v: v.reshape(DEPTH, 1, -1)

    ffn1 = (rows(ffn1_norm), bf(ffn1_w_gate), bf(ffn1_w_up), bf(ffn1_w_down))
    ffn2 = (rows(ffn2_norm), bf(ffn2_w_gate), bf(ffn2_w_up), bf(ffn2_w_down))
    mix_gain = rows(mix_norm)
    branch_w = (bf(w_br_mla), bf(w_br_moba), bf(w_br_dil), bf(w_out))
    w_in_b, w_uq_b, w_ukv_b = bf(w_in), bf(w_uq), bf(w_ukv)

    for l in range(DEPTH):
        x2 = _ffn(x2, l, *ffn1, final_gain, False)
        lw = _layer_weights(w_in_b[l], w_uq_b[l], w_ukv_b[l], q_norm[l], kv_norm[l])
        mqt, mk, mvt, bqt, bk, bvt, d0, d1, d2 = _inproj(
            x2.reshape(batch, seq, d), mix_gain[l], lw, tables)
        a = _mla(mqt, mk, mvt)
        b = _moba(bqt, bk, bvt)
        dil_outs = [_dil(dg, dil_bias[g]) for g, dg in enumerate((d0, d1, d2))]
        x2 = _merge(x2.reshape(batch, seq, d), l, mix_gain, lw["wgate"], a, b, dil_outs,
                    *branch_w).reshape(n, d)
        x2 = _ffn(x2, l, *ffn2, final_gain, l == DEPTH - 1)
    return x2.reshape(batch, seq, d)
```

```python
import functools

import numpy as np
import jax
import jax.numpy as jnp
from jax import lax
from jax.experimental import pallas as pl
from jax.experimental.pallas import tpu as pltpu

F32 = jnp.float32
BF16 = jnp.bfloat16

D_MODEL = 1024
DEPTH = 4
HEAD_DIM = 64
MLA_HEADS = 6
MLA_NOPE = 64
MLA_ROPE = 32
MLA_V = 64
MLA_Q_RANK = 256
MLA_KV_RANK = 128
ROPE_THETA = 10000.0
MOBA_HEADS = 6
MOBA_BLOCK = 256
MOBA_TOPK = 3
DIL_PATTERNS = ((128, 1), (512, 4), (2048, 16))
DIL_GROUP_HEADS = 4
DIL_HEADS = DIL_GROUP_HEADS * len(DIL_PATTERNS)
N_ALIBI = DIL_HEADS + MOBA_HEADS
N_BRANCHES = 3
D_FF = 2816
NORM_EPS = 1e-6
NEG_INF = -1e30
MLA_IN = MLA_Q_RANK + MLA_KV_RANK + MLA_ROPE
MOBA_IN = 3 * MOBA_HEADS * HEAD_DIM
DIL_IN = 3 * DIL_HEADS * HEAD_DIM
MLA_OUT = MLA_HEADS * MLA_V
MOBA_OUT = MOBA_HEADS * HEAD_DIM
DIL_OUT = DIL_GROUP_HEADS * HEAD_DIM

LANES = 128
SUBLANES = 8
BF16_ROWS = 16
MLA_QK = MLA_NOPE + MLA_ROPE
MLA_QPAD = MLA_HEADS * LANES
LOG2E = 1.4426950408889634
MLA_SCALE = MLA_QK ** -0.5
HEAD_SCALE = HEAD_DIM ** -0.5
DIL_L = DIL_PATTERNS[0][0] // DIL_PATTERNS[0][1]
DIL_COLS = 3 * DIL_GROUP_HEADS * HEAD_DIM

C_CQ = 0
C_CKV = C_CQ + MLA_Q_RANK
C_KR = C_CKV + MLA_KV_RANK
C_KR2 = C_KR + LANES
C_MOBA_K = C_KR2 + LANES
MOBA_KPAD = MOBA_HEADS * LANES
C_DIL = C_MOBA_K + MOBA_OUT
C_END = C_DIL + DIL_IN

AUX_POS = HEAD_DIM
AUX_POS_PARTS = 3
AUX_SEL = HEAD_DIM + BF16_ROWS

VMEM_LIMIT = 56 * 1024 * 1024

TM_FFN = 512
TM_PROJ = 512
TM_MERGE = 512
FF_CHUNK = 256
MERGE_CHUNK = 256
ATT_TQ = 512
ATT_TK = MOBA_BLOCK
MLA_NH = 6
MOBA_NH = 6
DIL_STEP_ROWS = 1024


def _cparams(sem):
    return pltpu.CompilerParams(dimension_semantics=sem, vmem_limit_bytes=VMEM_LIMIT)


def _resident(shape):
    nd = len(shape)
    return pl.BlockSpec(shape, lambda *_: (0,) * nd, pipeline_mode=pl.Buffered(1))


def _layer_block(stack, layer):
    shape = stack.shape[1:]
    return pl.BlockSpec((None,) + shape, lambda *_: (layer,) + (0,) * len(shape),
                        pipeline_mode=pl.Buffered(1))


def _rms(x, gain):
    ms = jnp.mean(x * x, axis=-1, keepdims=True)
    return x * lax.rsqrt(ms + NORM_EPS) * gain


def _dot(a, b):
    return jnp.dot(a, b, preferred_element_type=F32)


def _dot_nt(a, b):
    return lax.dot_general(a, b, (((1,), (1,)), ((), ())), preferred_element_type=F32)


def _dot_tn(a, b):
    return lax.dot_general(a, b, (((0,), (0,)), ((), ())), preferred_element_type=F32)


def _ffn_kernel(x_ref, g_ref, wg_ref, wu_ref, wd_ref, fg_ref, o_ref, *, final):
    x = x_ref[...]
    h = _rms(x, g_ref[...]).astype(BF16)
    acc = jnp.zeros(x.shape, F32)
    for c in range(D_FF // FF_CHUNK):
        sl = slice(c * FF_CHUNK, (c + 1) * FF_CHUNK)
        a = _dot(h, wg_ref[:, sl])
        u = _dot(h, wu_ref[:, sl])
        g = (a * jax.nn.sigmoid(a) * u).astype(BF16)
        acc = acc + _dot(g, wd_ref[sl, :])
    y = x + 0.5 * acc
    if final:
        y = _rms(y, fg_ref[...])
    o_ref[...] = y


def _ffn(x2, layer, gain, wg, wu, wd, final_gain, final):
    n = x2.shape[0]
    return pl.pallas_call(
        functools.partial(_ffn_kernel, final=final),
        grid=(n // TM_FFN,),
        in_specs=[
            pl.BlockSpec((TM_FFN, D_MODEL), lambda i: (i, 0)),
            _layer_block(gain, layer),
            _layer_block(wg, layer),
            _layer_block(wu, layer),
            _layer_block(wd, layer),
            _resident((1, D_MODEL)),
        ],
        out_specs=pl.BlockSpec((TM_FFN, D_MODEL), lambda i: (i, 0)),
        out_shape=jax.ShapeDtypeStruct((n, D_MODEL), F32),
        compiler_params=_cparams(("parallel",)),
        name="ffn_final" if final else "ffn",
    )(x2, gain, wg, wu, wd, final_gain)


def _inproj_kernel(x_ref, gn_ref, w1_ref, wt_ref, qn_ref, wuqt_ref, kvn_ref, wk_ref, wvt_ref,
                   cos_ref, sin_ref, cost_ref, sint_ref, kaux_ref,
                   mqt_ref, mk_ref, mvt_ref, bqt_ref, bk_ref, bvt_ref, d0_ref, d1_ref, d2_ref,
                   stage_ref):
    x = x_ref[0]
    h = _rms(x, gn_ref[...]).astype(BF16)

    pm = _dot(h, w1_ref[:, C_CQ:C_MOBA_K])
    cq = _rms(pm[:, C_CQ:C_CKV], qn_ref[...]).astype(BF16)
    ckv = _rms(pm[:, C_CKV:C_KR], kvn_ref[...]).astype(BF16)
    k_rope = pm[:, C_KR:C_KR2] * cos_ref[...] + pm[:, C_KR2:C_MOBA_K] * sin_ref[...]
    kk = _dot(ckv, wk_ref[...])
    qqt = _dot_nt(wuqt_ref[...], cq)
    cost = cost_ref[...]
    sint = sint_ref[...]
    for hh in range(MLA_HEADS):
        sl = slice(hh * LANES, (hh + 1) * LANES)
        sl2 = slice(MLA_QPAD + hh * LANES, MLA_QPAD + (hh + 1) * LANES)
        mqt_ref[0, sl, :] = ((qqt[sl] * cost + qqt[sl2] * sint) * (MLA_SCALE * LOG2E)).astype(BF16)
        mk_ref[0, :, sl] = (kk[:, sl] + k_rope).astype(BF16)
    mvt_ref[0] = _dot_nt(wvt_ref[...], ckv).astype(BF16)

    bt = _dot_nt(wt_ref[...], h)
    bqt_ref[0] = (bt[:MOBA_OUT] * (HEAD_SCALE * LOG2E)).astype(BF16)
    bvt_ref[0] = bt[MOBA_OUT:].astype(BF16)
    bk = _dot(h, w1_ref[:, C_MOBA_K:C_DIL])
    key_lanes = lax.broadcasted_iota(jnp.int32, (1, LANES), 1) < HEAD_DIM
    for p in range(MOBA_HEADS // 2):
        pair = bk[:, p * LANES:(p + 1) * LANES]
        for hh, keys in enumerate((pair, pltpu.roll(pair, HEAD_DIM, axis=1))):
            slot = slice((2 * p + hh) * LANES, (2 * p + hh + 1) * LANES)
            bk_ref[0, :, slot] = jnp.where(key_lanes, keys, kaux_ref[:, slot].astype(F32)).astype(BF16)

    tm = x.shape[0]
    for g, d_ref in enumerate((d0_ref, d1_ref, d2_ref)):
        dil = DIL_PATTERNS[g][1]
        dd = _dot(h, w1_ref[:, C_DIL + g * DIL_COLS:C_DIL + (g + 1) * DIL_COLS])
        for c in range(DIL_COLS // LANES):
            cols = slice(c * LANES, (c + 1) * LANES)
            blk = dd[:, cols]
            stage_ref[c] = blk * HEAD_SCALE if c < DIL_OUT // LANES else blk
            for r in range(dil):
                d_ref[0, r, :, cols] = stage_ref[c, pl.ds(r, tm // dil, stride=dil), :].astype(BF16)


def _inproj(x3, gain, lw, tables):
    batch, seq, _ = x3.shape
    tm = TM_PROJ
    cos_t, sin_t, cos_tt, sin_tt, kaux = tables
    tok = lambda c: pl.BlockSpec((1, tm, c), lambda b, i: (b, i, 0))
    feat = lambda c: pl.BlockSpec((1, c, tm), lambda b, i: (b, 0, i))
    tok_sds = lambda c: jax.ShapeDtypeStruct((batch, seq, c), BF16)
    feat_sds = lambda c: jax.ShapeDtypeStruct((batch, c, seq), BF16)
    dils = [dil for _, dil in DIL_PATTERNS]
    stream = lambda dil: pl.BlockSpec((1, dil, tm // dil, DIL_COLS), lambda b, i: (b, 0, i, 0))
    stream_sds = lambda dil: jax.ShapeDtypeStruct((batch, dil, seq // dil, DIL_COLS), BF16)
    return pl.pallas_call(
        _inproj_kernel,
        grid=(batch, seq // tm),
        in_specs=[
            tok(D_MODEL),
            _resident((1, D_MODEL)),
            _resident(lw["w1"].shape),
            _resident(lw["wt"].shape),
            _resident((1, MLA_Q_RANK)),
            _resident(lw["wuqt"].shape),
            _resident((1, MLA_KV_RANK)),
            _resident(lw["wk"].shape),
            _resident(lw["wvt"].shape),
            pl.BlockSpec((tm, LANES), lambda b, i: (i, 0)),
            pl.BlockSpec((tm, LANES), lambda b, i: (i, 0)),
            pl.BlockSpec((LANES, tm), lambda b, i: (0, i)),
            pl.BlockSpec((LANES, tm), lambda b, i: (0, i)),
            pl.BlockSpec((tm, MOBA_KPAD), lambda b, i: (i, 0)),
        ],
        out_specs=[feat(MLA_QPAD), tok(MLA_QPAD), feat(MLA_OUT),
                   feat(MOBA_OUT), tok(MOBA_KPAD), feat(MOBA_OUT)] + [stream(dil) for dil in dils],
        out_shape=[feat_sds(MLA_QPAD), tok_sds(MLA_QPAD), feat_sds(MLA_OUT),
                   feat_sds(MOBA_OUT), tok_sds(MOBA_KPAD), feat_sds(MOBA_OUT)]
        + [stream_sds(dil) for dil in dils],
        scratch_shapes=[pltpu.VMEM((DIL_COLS // LANES, tm, LANES), F32)],
        compiler_params=_cparams(("parallel", "parallel")),
        name="inproj",
    )(x3, gain, lw["w1"], lw["wt"], lw["qn"], lw["wuqt"], lw["kvn"], lw["wk"], lw["wvt"],
      cos_t, sin_t, cos_tt, sin_tt, kaux)


ROW_M, ROW_ALPHA, ROW_L, ROW_SMAX = range(4)
FLASH_UNROLL = 8
SUM_ROWS = 16


def _flash_heads(nh, n_full, n_diag, tq, tk, dv, score_fn, value_fn, mask_fn,
                 s_buf, acc_buf, stat_buf):
    i, t = n_full, tq
    heads = range(nh)
    ones = jnp.ones((SUM_ROWS, tk), BF16)
    col_max = lambda s: jnp.max(s, axis=0, keepdims=True)
    stat = lambda h, r: stat_buf[h, r:r + 1, :]

    def set_stat(h, r, v):
        stat_buf[h, r:r + 1, :] = v

    def values(h, j, p):
        pv = _dot(jnp.concatenate([value_fn(j, h), ones], axis=0), p)
        alpha = stat(h, ROW_ALPHA)
        acc_buf[h] = alpha * acc_buf[h] + pv[:dv]
        set_stat(h, ROW_L, alpha * stat(h, ROW_L) + pv[dv:dv + 1])

    def softmax(h, s, s_max):
        m = stat(h, ROW_M)
        m_new = jnp.maximum(m, s_max)
        alpha = jnp.exp2(m - m_new)
        p = jnp.exp2(s - m_new)
        set_stat(h, ROW_M, m_new)
        set_stat(h, ROW_ALPHA, alpha)
        return p.astype(BF16)

    for h in heads:
        sc = score_fn(0, h)
        s_buf[h] = sc
        set_stat(h, ROW_SMAX, col_max(sc))
        set_stat(h, ROW_M, jnp.full((1, t), NEG_INF, F32))
        set_stat(h, ROW_L, jnp.zeros((1, t), F32))
        acc_buf[h] = jnp.zeros((dv, t), F32)

    def block_step(j):
        for h in heads:
            sc = score_fn(j + 1, h)
            values(h, j, softmax(h, s_buf[h], stat(h, ROW_SMAX)))
            s_buf[h] = sc
            set_stat(h, ROW_SMAX, col_max(sc))

    @pl.loop(0, i // FLASH_UNROLL)
    def _(jj):
        for u in range(FLASH_UNROLL):
            block_step(FLASH_UNROLL * jj + u)

    done = i - i % FLASH_UNROLL
    chunk = FLASH_UNROLL // 2
    while chunk >= n_diag:
        @pl.when(i % (2 * chunk) >= chunk)
        def _(done=done, chunk=chunk):
            for u in range(chunk):
                block_step(done + u)
        done = done + jnp.where(i % (2 * chunk) >= chunk, chunk, 0)
        chunk //= 2

    for d in range(n_diag):
        for h in heads:
            sc = score_fn(i + d + 1, h) if d + 1 < n_diag else None
            s = mask_fn(d, s_buf[h])
            values(h, i + d, softmax(h, s, col_max(s)))
            if sc is not None:
                s_buf[h] = sc
    return [acc_buf[h] / stat(h, ROW_L) for h in heads]


def _store_feature_major(o_ref, outs):
    dv = outs[0].shape[0]
    for h, out in enumerate(outs):
        o_ref[0, h * dv:(h + 1) * dv, :] = out.astype(o_ref.dtype)


def _flash_scratch(nh, tq, tk, dv):
    return [pltpu.VMEM((nh, tk, tq), F32), pltpu.VMEM((nh, dv, tq), F32),
            pltpu.VMEM((nh, SUBLANES, tq), F32)]


def _causal_masker(tq, tk):
    ahead = (lax.broadcasted_iota(jnp.int32, (tk, tq), 1)
             - lax.broadcasted_iota(jnp.int32, (tk, tq), 0))
    return lambda d, s: jnp.where(ahead >= d * tk, s, NEG_INF)


def _mla_kernel(qt_ref, k_ref, vt_ref, o_ref, *flash_scratch):
    tq, tk = ATT_TQ, ATT_TK
    n_diag = tq // tk
    nh = MLA_NH
    qs = [qt_ref[0, h * LANES:h * LANES + MLA_QK, :] for h in range(nh)]

    def scores(j, h):
        start = pl.multiple_of(j * tk, tk)
        return _dot(k_ref[0, pl.ds(start, tk), h * LANES:h * LANES + MLA_QK], qs[h])

    def values(j, h):
        return vt_ref[0, h * MLA_V:(h + 1) * MLA_V, pl.ds(pl.multiple_of(j * tk, tk), tk)]

    outs = _flash_heads(nh, n_diag * pl.program_id(2), n_diag, tq, tk, MLA_V, scores, values,
                        _causal_masker(tq, tk), *flash_scratch)
    _store_feature_major(o_ref, outs)


def _mla(mqt, mk, mvt):
    batch, seq, _ = mk.shape
    tq = ATT_TQ
    nh = MLA_NH
    return pl.pallas_call(
        _mla_kernel,
        grid=(batch, MLA_HEADS // nh, seq // tq),
        in_specs=[pl.BlockSpec((1, nh * LANES, tq), lambda b, g, i: (b, g, i)),
                  pl.BlockSpec((1, seq, nh * LANES), lambda b, g, i: (b, 0, g)),
                  pl.BlockSpec((1, nh * MLA_V, seq), lambda b, g, i: (b, g, 0))],
        out_specs=pl.BlockSpec((1, nh * MLA_V, tq), lambda b, g, i: (b, g, i)),
        out_shape=jax.ShapeDtypeStruct((batch, MLA_OUT, seq), BF16),
        scratch_shapes=_flash_scratch(nh, tq, ATT_TK, MLA_V),
        compiler_params=_cparams(("parallel", "parallel", "arbitrary")),
        name="mla_attn",
    )(mqt, mk, mvt)


def _moba_kernel(qt_ref, k_ref, vt_ref, o_ref, kmh_ref, kml_ref, *flash_scratch, nblk):
    i = pl.program_id(2)
    tq, tk = ATT_TQ, MOBA_BLOCK
    n_diag = tq // tk
    nh = MOBA_NH

    @pl.when(i == 0)
    def _():
        kf = k_ref[0].astype(F32).reshape(nblk, tk, nh * LANES)
        km = jnp.sum(kf, axis=1) * (1.0 / tk)
        hi = km.astype(BF16)
        kmh_ref[...] = hi
        kml_ref[...] = (km - hi.astype(F32)).astype(BF16)

    blk = lax.broadcasted_iota(jnp.int32, (nblk, 1), 0)
    blk_f = blk.astype(F32)
    own = n_diag * i + lax.broadcasted_iota(jnp.int32, (1, tq), 1) // tk
    pos_rows = lax.broadcasted_iota(jnp.int32, (AUX_SEL - AUX_POS, 1), 0) < AUX_POS_PARTS
    ones_rows = jnp.broadcast_to(jnp.where(pos_rows, 1.0, 0.0), (AUX_SEL - AUX_POS, tq)).astype(BF16)
    pad_q = jnp.zeros((LANES - HEAD_DIM, tq), BF16)
    qs = []
    for h in range(nh):
        q = qt_ref[0, h * HEAD_DIM:(h + 1) * HEAD_DIM, :]
        q_plain = jnp.concatenate([q, pad_q], axis=0)
        cols = slice(h * LANES, (h + 1) * LANES)
        gate = _dot(kmh_ref[:, cols], q_plain) + _dot(kml_ref[:, cols], q_plain)
        gate = jnp.where(blk < own, gate, NEG_INF)
        chosen = blk == own
        for _ in range(min(MOBA_TOPK, nblk)):
            best = jnp.max(gate, axis=0, keepdims=True)
            first = jnp.min(jnp.where(gate == best, blk_f, float(nblk)), axis=0, keepdims=True)
            pick = blk_f == first
            chosen = chosen | (pick & (blk < own))
            gate = jnp.where(pick, -jnp.inf, gate)
        sel = jnp.where(chosen, 0.0, NEG_INF).astype(BF16)
        qs.append(jnp.concatenate([q, ones_rows, sel], axis=0))

    def scores(j, h):
        start = pl.multiple_of(j * tk, tk)
        return _dot(k_ref[0, pl.ds(start, tk), h * LANES:h * LANES + AUX_SEL + nblk], qs[h])

    def values(j, h):
        return vt_ref[0, h * HEAD_DIM:(h + 1) * HEAD_DIM, pl.ds(pl.multiple_of(j * tk, tk), tk)]

    outs = _flash_heads(nh, n_diag * i, n_diag, tq, tk, HEAD_DIM, scores, values,
                        _causal_masker(tq, tk), *flash_scratch)
    _store_feature_major(o_ref, outs)


def _moba(bqt, bk, bvt):
    batch, seq, _ = bk.shape
    tq = ATT_TQ
    nh = MOBA_NH
    nblk = seq // MOBA_BLOCK
    assert AUX_SEL + nblk <= LANES
    cols = nh * HEAD_DIM
    return pl.pallas_call(
        functools.partial(_moba_kernel, nblk=nblk),
        grid=(batch, MOBA_HEADS // nh, seq // tq),
        in_specs=[
            pl.BlockSpec((1, cols, tq), lambda b, g, i: (b, g, i)),
            pl.BlockSpec((1, seq, nh * LANES), lambda b, g, i: (b, 0, g)),
            pl.BlockSpec((1, cols, seq), lambda b, g, i: (b, g, 0)),
        ],
        out_specs=pl.BlockSpec((1, cols, tq), lambda b, g, i: (b, g, i)),
        out_shape=jax.ShapeDtypeStruct((batch, MOBA_OUT, seq), BF16),
        scratch_shapes=[
            pltpu.VMEM((nblk, nh * LANES), BF16),
            pltpu.VMEM((nblk, nh * LANES), BF16),
        ] + _flash_scratch(nh, tq, MOBA_BLOCK, HEAD_DIM),
        compiler_params=_cparams(("parallel", "parallel", "arbitrary")),
        name="moba_attn",
    )(bqt, bk, bvt)


def _dil_kernel(own_ref, prev_ref, bias_ref, o_ref, lse_ref, *, ns, qb):
    n = pl.program_id(2)
    L = DIL_L
    low = lax.broadcasted_iota(jnp.int32, (1, LANES), 1) < HEAD_DIM
    first_cols = lax.broadcasted_iota(jnp.int32, (1, 2 * L), 1) < L
    problems = [(si, pair, blk) for si in range(ns) for pair in range(DIL_OUT // LANES)
                for blk in range(qb // L)]

    def window(si, blk, cols):
        if blk == 0:
            return jnp.concatenate([prev_ref[0, si, :, cols], own_ref[0, si, :L, cols]], axis=0)
        return own_ref[0, si, (blk - 1) * L:(blk + 1) * L, cols]

    scores = []
    for si, pair, blk in problems:
        qp = own_ref[0, si, blk * L:(blk + 1) * L, pair * LANES:(pair + 1) * LANES]
        zero = jnp.zeros_like(qp)
        keys = window(si, blk, slice(DIL_OUT + pair * LANES, DIL_OUT + (pair + 1) * LANES))
        scores.append([_dot_nt(jnp.where(low, qp, zero), keys),
                       _dot_nt(jnp.where(low, zero, qp), keys)])

    probs = []
    for (si, pair, blk), sc in zip(problems, scores):
        per_head = []
        for hh in range(2):
            s = sc[hh] + bias_ref[2 * pair + hh]
            if blk == 0:
                s = jnp.where(first_cols & (n == 0), NEG_INF, s)
            m = jnp.max(s, axis=-1, keepdims=True)
            e = jnp.exp(s - m)
            den = jnp.sum(e, axis=-1, keepdims=True)
            per_head.append((e.astype(BF16), den, m + jnp.log(den)))
        probs.append(per_head)

    for (si, pair, blk), per_head in zip(problems, probs):
        vals = window(si, blk, slice(2 * DIL_OUT + pair * LANES, 2 * DIL_OUT + (pair + 1) * LANES))
        o_pair = [_dot(e, vals) / den for e, den, _ in per_head]
        lse_pair = [jnp.broadcast_to(lse, (L, LANES)) for _, _, lse in per_head]
        dst = (0, si, slice(blk * L, (blk + 1) * L), slice(pair * LANES, (pair + 1) * LANES))
        o_ref[dst] = jnp.where(low, o_pair[0], o_pair[1])
        lse_ref[dst] = jnp.where(low, lse_pair[0], lse_pair[1])


def _dil(dg, bias):
    batch, dilation, rows, _ = dg.shape
    L = DIL_L
    qb = min(DIL_STEP_ROWS, rows)
    ns = min(DIL_STEP_ROWS // qb, dilation)
    per = qb // L
    out_sds = jax.ShapeDtypeStruct((batch, dilation, rows, DIL_OUT), F32)
    return pl.pallas_call(
        functools.partial(_dil_kernel, ns=ns, qb=qb),
        grid=(batch, dilation // ns, rows // qb),
        in_specs=[
            pl.BlockSpec((1, ns, qb, DIL_COLS), lambda b, r, n: (b, r, n, 0)),
            pl.BlockSpec((1, ns, L, DIL_COLS),
                         lambda b, r, n: (b, r, jnp.maximum(n * per - 1, 0), 0)),
            _resident(bias.shape),
        ],
        out_specs=[pl.BlockSpec((1, ns, qb, DIL_OUT), lambda b, r, n: (b, r, n, 0))] * 2,
        out_shape=[out_sds, out_sds],
        compiler_params=_cparams(("parallel", "parallel", "arbitrary")),
        name="dil_attn_d%d" % dilation,
    )(dg, dg, bias)


def _merge_kernel(x_ref, gn_ref, wgate_ref, a_ref, b_ref, o0_ref, o1_ref, o2_ref,
                  l0_ref, l1_ref, l2_ref, wa_ref, wb_ref, wc_ref, wo_ref, out_ref, *scratch):
    x = x_ref[0]
    h = _rms(x, gn_ref[...]).astype(BF16)
    tm = x.shape[0]
    scratch = list(scratch)

    def token_order(ref):
        dil = ref.shape[1]
        if dil == 1:
            return ref[0, 0]
        buf = scratch.pop()
        for c in range(buf.shape[0]):
            for r in range(dil):
                buf[c, pl.ds(r, tm // dil, stride=dil), :] = ref[0, r, :, c * LANES:(c + 1) * LANES]
        return jnp.concatenate([buf[c] for c in range(buf.shape[0])], axis=1)

    lses = [token_order(r) for r in (l0_ref, l1_ref, l2_ref)]
    outs = [token_order(r) for r in (o0_ref, o1_ref, o2_ref)]
    mx = jnp.maximum(jnp.maximum(lses[0], lses[1]), lses[2])
    es = [jnp.exp(l - mx) for l in lses]
    den = es[0] + es[1] + es[2]
    c = (es[0] / den) * outs[0] + (es[1] / den) * outs[1] + (es[2] / den) * outs[2]

    c16 = c.astype(BF16)
    a_t, b_t = a_ref[0], b_ref[0]
    chunks = []
    for n in range(D_MODEL // MERGE_CHUNK):
        cols = slice(n * MERGE_CHUNK, (n + 1) * MERGE_CHUNK)
        projected = (_dot_tn(a_t, wa_ref[:, cols]), _dot_tn(b_t, wb_ref[:, cols]),
                     _dot(c16, wc_ref[:, cols]))
        part = None
        for k, proj in enumerate(projected):
            gcols = slice(k * D_MODEL + n * MERGE_CHUNK, k * D_MODEL + (n + 1) * MERGE_CHUNK)
            term = jax.nn.sigmoid(_dot(h, wgate_ref[:, gcols])) * proj
            part = term if part is None else part + term
        chunks.append(part.astype(BF16))
    out_ref[0] = x + _dot(jnp.concatenate(chunks, axis=1), wo_ref[...])


def _merge(x3, layer, gain, wgate, a, b, dil_outs, wa, wb, wc, wo):
    batch, seq, _ = x3.shape
    tm = TM_MERGE
    tok = lambda c: pl.BlockSpec((1, tm, c), lambda b, i: (b, i, 0))
    feat = lambda c: pl.BlockSpec((1, c, tm), lambda b, i: (b, 0, i))
    stream = lambda arr: pl.BlockSpec((1, arr.shape[1], tm // arr.shape[1], DIL_OUT),
                                      lambda b, i: (b, 0, i, 0))
    (o0, l0), (o1, l1), (o2, l2) = dil_outs
    streams = (o0, o1, o2, l0, l1, l2)
    n_reordered = sum(arr.shape[1] > 1 for arr in streams)
    return pl.pallas_call(
        _merge_kernel,
        grid=(batch, seq // tm),
        in_specs=[tok(D_MODEL), _layer_block(gain, layer), _resident(wgate.shape),
                  feat(MLA_OUT), feat(MOBA_OUT)]
        + [stream(arr) for arr in streams]
        + [_layer_block(w, layer) for w in (wa, wb, wc, wo)],
        out_specs=tok(D_MODEL),
        out_shape=jax.ShapeDtypeStruct((batch, seq, D_MODEL), F32),
        scratch_shapes=[pltpu.VMEM((DIL_OUT // LANES, tm, LANES), F32)] * n_reordered,
        compiler_params=_cparams(("parallel", "parallel")),
        name="merge",
    )(x3, gain, wgate, a, b, *streams, wa, wb, wc, wo)


def _alibi_slopes():
    return (2.0 ** (-8.0 * np.arange(1, N_ALIBI + 1, dtype=np.float32) / N_ALIBI)).astype(np.float32)


def _rope_tables(seq):
    pos = jnp.arange(seq, dtype=F32)
    inv = ROPE_THETA ** (-jnp.arange(0, MLA_ROPE, 2, dtype=F32) / MLA_ROPE)
    ang = pos[:, None] * inv[None, :]
    cos, sin = jnp.cos(ang), jnp.sin(ang)
    pad = jnp.zeros((seq, LANES - MLA_QK), F32)
    cos_t = jnp.concatenate([jnp.ones((seq, MLA_NOPE), F32), cos, cos, pad], axis=1)
    sin_t = jnp.concatenate([jnp.zeros((seq, MLA_NOPE), F32), -sin, sin, pad], axis=1)
    return cos_t, sin_t, cos_t.T, sin_t.T


def _dil_bias(group, dilation):
    L = DIL_L
    slopes = _alibi_slopes()[group * DIL_GROUP_HEADS:(group + 1) * DIL_GROUP_HEADS]
    steps = L + np.arange(L)[:, None] - np.arange(2 * L)[None, :]
    valid = (steps >= 0) & (steps <= L)
    dist = (steps * dilation).astype(np.float32)
    bias = np.where(valid[None], -slopes[:, None, None] * dist[None], np.float32(NEG_INF))
    return jnp.asarray(bias.astype(np.float32))


def _moba_key_aux(seq):
    slopes = _alibi_slopes()[DIL_HEADS:].astype(np.float64) * LOG2E
    pos = jnp.asarray((slopes[None, :] * np.arange(seq)[:, None]).astype(np.float32))
    aux = jnp.zeros((seq, MOBA_HEADS, LANES), F32)
    rest = pos
    for part in range(AUX_POS_PARTS):
        piece = rest.astype(BF16).astype(F32)
        aux = aux.at[:, :, AUX_POS + part].set(piece)
        rest = rest - piece
    onehot = jax.nn.one_hot(jnp.arange(seq) // MOBA_BLOCK, seq // MOBA_BLOCK, dtype=F32)
    aux = aux.at[:, :, AUX_SEL:AUX_SEL + seq // MOBA_BLOCK].set(onehot[:, None, :])
    return aux.reshape(seq, MOBA_KPAD).astype(BF16)


def _layer_weights(w_in, w_uq, w_ukv, q_norm, kv_norm):
    d = w_in.shape[0]
    z = lambda c: jnp.zeros((d, c), w_in.dtype)
    i0 = MLA_Q_RANK + MLA_KV_RANK
    kr = w_in[:, i0:i0 + MLA_ROPE]
    hr = MLA_ROPE // 2
    kr_pad = jnp.concatenate([z(MLA_NOPE), kr, z(LANES - MLA_QK)], axis=1)
    kr_swap = jnp.concatenate([z(MLA_NOPE), kr[:, hr:], kr[:, :hr], z(LANES - MLA_QK)], axis=1)
    moba = w_in[:, MLA_IN:MLA_IN + MOBA_IN]
    bq, bk, bv = (moba[:, c * MOBA_OUT:(c + 1) * MOBA_OUT] for c in range(3))
    dil = w_in[:, MLA_IN + MOBA_IN:MLA_IN + MOBA_IN + DIL_IN].reshape(d, 3, DIL_HEADS, HEAD_DIM)
    groups = [dil[:, :, g * DIL_GROUP_HEADS:(g + 1) * DIL_GROUP_HEADS].reshape(d, DIL_COLS)
              for g in range(len(DIL_PATTERNS))]
    w1 = jnp.concatenate([w_in[:, :i0], kr_pad, kr_swap, bk] + groups, axis=1).astype(BF16)
    wt = jnp.concatenate([bq, bv], axis=1).T.astype(BF16)
    wgate = w_in[:, MLA_IN + MOBA_IN + DIL_IN:].astype(BF16)

    r = w_uq.shape[0]
    uq = w_uq.reshape(r, MLA_HEADS, MLA_QK)
    zq = jnp.zeros((r, MLA_HEADS, LANES - MLA_QK), w_uq.dtype)
    zn = jnp.zeros((r, MLA_HEADS, MLA_NOPE), w_uq.dtype)
    uq_pad = jnp.concatenate([uq, zq], axis=2).reshape(r, MLA_QPAD)
    uq_swap = jnp.concatenate(
        [zn, uq[:, :, MLA_NOPE + hr:], uq[:, :, MLA_NOPE:MLA_NOPE + hr], zq], axis=2
    ).reshape(r, MLA_QPAD)
    wuqt = jnp.concatenate([uq_pad, uq_swap], axis=1).T.astype(BF16)

    rk = w_ukv.shape[0]
    ukv = w_ukv.reshape(rk, MLA_HEADS, MLA_NOPE + MLA_V)
    zk = jnp.zeros((rk, MLA_HEADS, LANES - MLA_NOPE), w_ukv.dtype)
    wk = jnp.concatenate([ukv[:, :, :MLA_NOPE], zk], axis=2).reshape(rk, MLA_QPAD).astype(BF16)
    wvt = ukv[:, :, MLA_NOPE:].reshape(rk, MLA_OUT).T.astype(BF16)
    return dict(w1=w1, wt=wt, wgate=wgate, wuqt=wuqt, wk=wk, wvt=wvt,
                qn=q_norm.reshape(1, -1), kvn=kv_norm.reshape(1, -1))


def kernel(x, ffn1_norm, ffn1_w_gate, ffn1_w_up, ffn1_w_down, mix_norm, w_in, q_norm, w_uq, kv_norm, w_ukv, w_br_mla, w_br_moba, w_br_dil, w_out, ffn2_norm, ffn2_w_gate, ffn2_w_up, ffn2_w_down, final_norm):
    batch, seq, d = x.shape
    assert d == D_MODEL and seq % (DIL_PATTERNS[-1][1] * DIL_L) == 0 and seq % TM_PROJ == 0
    n = batch * seq
    x2 = x.reshape(n, d)
    tables = _rope_tables(seq) + (_moba_key_aux(seq),)
    dil_bias = [_dil_bias(g, dil) for g, (_, dil) in enumerate(DIL_PATTERNS)]
    final_gain = final_norm.reshape(1, d)
    bf = lambda w: w.astype(BF16)
    rows = lambda v: v.reshape(DEPTH, 1, -1)

    ffn1 = (rows(ffn1_norm), bf(ffn1_w_gate), bf(ffn1_w_up), bf(ffn1_w_down))
    ffn2 = (rows(ffn2_norm), bf(ffn2_w_gate), bf(ffn2_w_up), bf(ffn2_w_down))
    mix_gain = rows(mix_norm)
    branch_w = (bf(w_br_mla), bf(w_br_moba), bf(w_br_dil), bf(w_out))
    w_in_b, w_uq_b, w_ukv_b = bf(w_in), bf(w_uq), bf(w_ukv)

    for l in range(DEPTH):
        x2 = _ffn(x2, l, *ffn1, final_gain, False)
        lw = _layer_weights(w_in_b[l], w_uq_b[l], w_ukv_b[l], q_norm[l], kv_norm[l])
        mqt, mk, mvt, bqt, bk, bvt, d0, d1, d2 = _inproj(
            x2.reshape(batch, seq, d), mix_gain[l], lw, tables)
        a = _mla(mqt, mk, mvt)
        b = _moba(bqt, bk, bvt)
        dil_outs = [_dil(dg, dil_bias[g]) for g, dg in enumerate((d0, d1, d2))]
        x2 = _merge(x2.reshape(batch, seq, d), l, mix_gain, lw["wgate"], a, b, dil_outs,
                    *branch_w).reshape(n, d)
        x2 = _ffn(x2, l, *ffn2, final_gain, l == DEPTH - 1)
    return x2.reshape(batch, seq, d)
```
